```python
import jax, jax.numpy as jnp
from jax import lax
import numpy as np

D_MODEL = 2048
BATCH = 8
SEQ = 8192
DEPTH = 4

CHUNK = 64
N_MIXERS = 2
HEAD_DIM = 128
N_HEADS = D_MODEL // HEAD_DIM
Q_BLOCK = 128
POOL_WINDOWS = (2, 4, 8, 16)
N_POOL_GROUPS = len(POOL_WINDOWS)
POOL_GROUP = D_MODEL // N_POOL_GROUPS
D_FF = -(-8 * D_MODEL // (3 * 256)) * 256
N_FOX_LAYERS = (DEPTH + N_MIXERS - 1) // N_MIXERS
N_POOL_LAYERS = DEPTH // N_MIXERS
RMS_EPS = 1e-6
NEG_INF = -1e30

kernel_name = "fox_pool_hybrid_encoder"


def rmsnorm(x, g):
    xf = x.astype(jnp.float32)
    y = xf * lax.rsqrt(jnp.mean(xf * xf, axis=-1, keepdims=True) + RMS_EPS)
    return (y * g.astype(jnp.float32)).astype(x.dtype)


def forgetting_attention(h, w_in, b_f, g_q, g_k, w_out):
    B, S, D = h.shape
    proj = jnp.einsum('bsd,de->bse', h, w_in)
    q = proj[..., :D].reshape(B, S, N_HEADS, HEAD_DIM)
    k = proj[..., D:2 * D].reshape(B, S, N_HEADS, HEAD_DIM)
    v = proj[..., 2 * D:3 * D].reshape(B, S, N_HEADS, HEAD_DIM)
    f_logit = proj[..., 3 * D:]
    q = rmsnorm(q, g_q)
    k = rmsnorm(k, g_k)
    log_f = jax.nn.log_sigmoid((f_logit + b_f).astype(jnp.float32))
    c = jnp.cumsum(log_f, axis=1).transpose(0, 2, 1)
    nb = S // Q_BLOCK
    q_blocks = q.reshape(B, nb, Q_BLOCK, N_HEADS, HEAD_DIM).transpose(1, 0, 2, 3, 4)
    c_blocks = c.reshape(B, N_HEADS, nb, Q_BLOCK).transpose(2, 0, 1, 3)
    starts = jnp.arange(nb) * Q_BLOCK
    k_pos = jnp.arange(S)
    scale = HEAD_DIM ** -0.5

    def block(args):
        qb, cb, start = args
        s = jnp.einsum('bqhd,bkhd->bhqk', qb, k).astype(jnp.float32) * scale
        s = s + (cb[..., :, None] - c[..., None, :])
        q_pos = start + jnp.arange(Q_BLOCK)
        s = jnp.where(q_pos[:, None] >= k_pos[None, :], s, NEG_INF)
        p = jax.nn.softmax(s, axis=-1).astype(v.dtype)
        return jnp.einsum('bhqk,bkhd->bqhd', p, v)

    o = lax.map(block, (q_blocks, c_blocks, starts))
    o = o.transpose(1, 0, 2, 3, 4).reshape(B, S, D)
    return jnp.einsum('bsd,de->bse', o, w_out)


def multiscale_pool(h, w, b, scale):
    B, S, D = h.shape
    hf = h.astype(jnp.float32)
    cs = jnp.concatenate([jnp.zeros((B, 1, D), jnp.float32), jnp.cumsum(hf, axis=1)], axis=1)
    t = jnp.arange(S)
    means = []
    for g, win in enumerate(POOL_WINDOWS):
        csg = cs[..., g * POOL_GROUP:(g + 1) * POOL_GROUP]
        lo = jnp.maximum(t + 1 - win, 0)
        cnt = (t + 1 - lo).astype(jnp.float32)
        means.append((csg[:, t + 1] - csg[:, lo]) / cnt[None, :, None])
    y = (jnp.concatenate(means, axis=-1) - hf).astype(h.dtype)
    y = jnp.einsum('bsgc,gce->bsge', y.reshape(B, S, N_POOL_GROUPS, POOL_GROUP), w).reshape(B, S, D)
    return (y + b) * scale


def swiglu(h, w_gu, w_down):
    gu = jnp.einsum('bsd,df->bsf', h, w_gu)
    gate, up = gu[..., :D_FF], gu[..., D_FF:]
    return jnp.einsum('bsf,fd->bsd', jax.nn.silu(gate) * up, w_down)


def _fwd_setup_inputs(seed: int = 0) -> dict:
    key = jax.random.key(seed)
    ks = jax.random.split(key, 13)
    f32 = jnp.float32
    D = D_MODEL
    x = jax.random.normal(ks[0], (BATCH, SEQ, D), f32)
    mix_norm_g = 1.0 + 0.02 * jax.random.normal(ks[1], (DEPTH, D), f32)
    ffn_norm_g = 1.0 + 0.02 * jax.random.normal(ks[2], (DEPTH, D), f32)
    fox_w_in = jax.random.normal(ks[3], (N_FOX_LAYERS, D, 3 * D + N_HEADS), f32) * D ** -0.5
    fox_b_f = 2.0 + 0.5 * jax.random.normal(ks[4], (N_FOX_LAYERS, N_HEADS), f32)
    fox_q_norm_g = 1.0 + 0.02 * jax.random.normal(ks[5], (N_FOX_LAYERS, HEAD_DIM), f32)
    fox_k_norm_g = 1.0 + 0.02 * jax.random.normal(ks[6], (N_FOX_LAYERS, HEAD_DIM), f32)
    fox_w_out = jax.random.normal(ks[7], (N_FOX_LAYERS, D, D), f32) * D ** -0.5
    pool_w = jax.random.normal(ks[8], (N_POOL_LAYERS, N_POOL_GROUPS, POOL_GROUP, POOL_GROUP), f32) * POOL_GROUP ** -0.5
    pool_b = 0.01 * jax.random.normal(ks[9], (N_POOL_LAYERS, D), f32)
    pool_scale = 1.0 + 0.02 * jax.random.normal(ks[10], (N_POOL_LAYERS, D), f32)
    ffn_w_gate_up = jax.random.normal(ks[11], (DEPTH, D, 2 * D_FF), f32) * D ** -0.5
    ffn_w_down = jax.random.normal(ks[12], (DEPTH, D_FF, D), f32) * D_FF ** -0.5
    return {"x": x, "mix_norm_g": mix_norm_g, "ffn_norm_g": ffn_norm_g,
            "fox_w_in": fox_w_in, "fox_b_f": fox_b_f, "fox_q_norm_g": fox_q_norm_g,
            "fox_k_norm_g": fox_k_norm_g, "fox_w_out": fox_w_out,
            "pool_w": pool_w, "pool_b": pool_b, "pool_scale": pool_scale,
            "ffn_w_gate_up": ffn_w_gate_up, "ffn_w_down": ffn_w_down}


def _fwd_reference(x, mix_norm_g, ffn_norm_g, fox_w_in, fox_b_f, fox_q_norm_g, fox_k_norm_g,
              fox_w_out, pool_w, pool_b, pool_scale, ffn_w_gate_up, ffn_w_down):
    for i in range(DEPTH):
        j = i // N_MIXERS
        h = rmsnorm(x, mix_norm_g[i])
        if i % N_MIXERS == 0:
            x = x + forgetting_attention(h, fox_w_in[j], fox_b_f[j], fox_q_norm_g[j],
                                         fox_k_norm_g[j], fox_w_out[j])
        else:
            x = x + multiscale_pool(h, pool_w[j], pool_b[j], pool_scale[j])
        h = rmsnorm(x, ffn_norm_g[i])
        x = x + swiglu(h, ffn_w_gate_up[i], ffn_w_down[i])
    return x


import jax as _jax
import jax.numpy as _jnp

TWIN_FORMAT = 'train_step'
FWD_PARAMS = ['x', 'mix_norm_g', 'ffn_norm_g', 'fox_w_in', 'fox_b_f', 'fox_q_norm_g', 'fox_k_norm_g', 'fox_w_out', 'pool_w', 'pool_b', 'pool_scale', 'ffn_w_gate_up', 'ffn_w_down']
TWIN_WEIGHTS = ['mix_norm_g', 'ffn_norm_g', 'fox_w_in', 'fox_b_f', 'fox_q_norm_g', 'fox_k_norm_g', 'fox_w_out', 'pool_w', 'pool_b', 'pool_scale', 'ffn_w_gate_up', 'ffn_w_down']
TWIN_DIFF_INPUT = 'x'
TWIN_INPUTS = ['x', 'mix_norm_g', 'ffn_norm_g', 'fox_w_in', 'fox_b_f', 'fox_q_norm_g', 'fox_k_norm_g', 'fox_w_out', 'pool_w', 'pool_b', 'pool_scale', 'ffn_w_gate_up', 'ffn_w_down', 'loss_target', 'm_mix_norm_g', 'm_ffn_norm_g', 'm_fox_w_in', 'm_fox_b_f', 'm_fox_q_norm_g', 'm_fox_k_norm_g', 'm_fox_w_out', 'm_pool_w', 'm_pool_b', 'm_pool_scale', 'm_ffn_w_gate_up', 'm_ffn_w_down', 'v_mix_norm_g', 'v_ffn_norm_g', 'v_fox_w_in', 'v_fox_b_f', 'v_fox_q_norm_g', 'v_fox_k_norm_g', 'v_fox_w_out', 'v_pool_w', 'v_pool_b', 'v_pool_scale', 'v_ffn_w_gate_up', 'v_ffn_w_down']
TWIN_OUTPUTS = ['loss', 'grad_x', 'grad_mix_norm_g', 'grad_ffn_norm_g', 'grad_fox_w_in', 'grad_fox_b_f', 'grad_fox_q_norm_g', 'grad_fox_k_norm_g', 'grad_fox_w_out', 'grad_pool_w', 'grad_pool_b', 'grad_pool_scale', 'grad_ffn_w_gate_up', 'grad_ffn_w_down', 'delta_mix_norm_g', 'delta_ffn_norm_g', 'delta_fox_w_in', 'delta_fox_b_f', 'delta_fox_q_norm_g', 'delta_fox_k_norm_g', 'delta_fox_w_out', 'delta_pool_w', 'delta_pool_b', 'delta_pool_scale', 'delta_ffn_w_gate_up', 'delta_ffn_w_down', 'new_m_mix_norm_g', 'new_m_ffn_norm_g', 'new_m_fox_w_in', 'new_m_fox_b_f', 'new_m_fox_q_norm_g', 'new_m_fox_k_norm_g', 'new_m_fox_w_out', 'new_m_pool_w', 'new_m_pool_b', 'new_m_pool_scale', 'new_m_ffn_w_gate_up', 'new_m_ffn_w_down', 'new_v_mix_norm_g', 'new_v_ffn_norm_g', 'new_v_fox_w_in', 'new_v_fox_b_f', 'new_v_fox_q_norm_g', 'new_v_fox_k_norm_g', 'new_v_fox_w_out', 'new_v_pool_w', 'new_v_pool_b', 'new_v_pool_scale', 'new_v_ffn_w_gate_up', 'new_v_ffn_w_down']
TWIN_LEAF_KINDS = {'loss': 'loss', 'grad_x': 'grad_x', 'grad_mix_norm_g': 'grad_w', 'grad_ffn_norm_g': 'grad_w', 'grad_fox_w_in': 'grad_w', 'grad_fox_b_f': 'grad_w', 'grad_fox_q_norm_g': 'grad_w', 'grad_fox_k_norm_g': 'grad_w', 'grad_fox_w_out': 'grad_w', 'grad_pool_w': 'grad_w', 'grad_pool_b': 'grad_w', 'grad_pool_scale': 'grad_w', 'grad_ffn_w_gate_up': 'grad_w', 'grad_ffn_w_down': 'grad_w', 'delta_mix_norm_g': 'delta_w', 'delta_ffn_norm_g': 'delta_w', 'delta_fox_w_in': 'delta_w', 'delta_fox_b_f': 'delta_w', 'delta_fox_q_norm_g': 'delta_w', 'delta_fox_k_norm_g': 'delta_w', 'delta_fox_w_out': 'delta_w', 'delta_pool_w': 'delta_w', 'delta_pool_b': 'delta_w', 'delta_pool_scale': 'delta_w', 'delta_ffn_w_gate_up': 'delta_w', 'delta_ffn_w_down': 'delta_w', 'new_m_mix_norm_g': 'new_m', 'new_m_ffn_norm_g': 'new_m', 'new_m_fox_w_in': 'new_m', 'new_m_fox_b_f': 'new_m', 'new_m_fox_q_norm_g': 'new_m', 'new_m_fox_k_norm_g': 'new_m', 'new_m_fox_w_out': 'new_m', 'new_m_pool_w': 'new_m', 'new_m_pool_b': 'new_m', 'new_m_pool_scale': 'new_m', 'new_m_ffn_w_gate_up': 'new_m', 'new_m_ffn_w_down': 'new_m', 'new_v_mix_norm_g': 'new_v', 'new_v_ffn_norm_g': 'new_v', 'new_v_fox_w_in': 'new_v', 'new_v_fox_b_f': 'new_v', 'new_v_fox_q_norm_g': 'new_v', 'new_v_fox_k_norm_g': 'new_v', 'new_v_fox_w_out': 'new_v', 'new_v_pool_w': 'new_v', 'new_v_pool_b': 'new_v', 'new_v_pool_scale': 'new_v', 'new_v_ffn_w_gate_up': 'new_v', 'new_v_ffn_w_down': 'new_v'}


def _forward(args):
    return _fwd_reference(*[args[k] for k in FWD_PARAMS])


def _output_shape():
    def fwd():
        inp = _fwd_setup_inputs(0)
        return _fwd_reference(*[inp[k] for k in FWD_PARAMS])
    out = _jax.eval_shape(fwd)
    return out.shape, out.dtype

N_MICROBATCH = 1
ADAM_LR = 0.001
ADAM_B1 = 0.9
ADAM_B2 = 0.999
ADAM_EPS = 1e-08
ADAM_WD = 0.01
ADAM_STEP = 10
PER_EXAMPLE_BATCH_AXIS = {'x': 0, 'loss_target': 0}
SHARED_INPUTS = []
_WEIGHT_DTYPES = {'mix_norm_g': _jnp.float32, 'ffn_norm_g': _jnp.float32, 'fox_w_in': _jnp.float32, 'fox_b_f': _jnp.float32, 'fox_q_norm_g': _jnp.float32, 'fox_k_norm_g': _jnp.float32, 'fox_w_out': _jnp.float32, 'pool_w': _jnp.float32, 'pool_b': _jnp.float32, 'pool_scale': _jnp.float32, 'ffn_w_gate_up': _jnp.float32, 'ffn_w_down': _jnp.float32}
MOMENT_SCALE = {'mix_norm_g': 1.861248e+01, 'ffn_norm_g': 2.465553e+01, 'fox_w_in': 2.925589e-01, 'fox_b_f': 2.019361e+02, 'fox_q_norm_g': 3.163139e+01, 'fox_k_norm_g': 3.159122e+01, 'fox_w_out': 3.134622e-01, 'pool_w': 2.047683e+00, 'pool_b': 9.621562e+00, 'pool_scale': 2.554501e+01, 'ffn_w_gate_up': 2.048787e-01, 'ffn_w_down': 3.613071e-01}


def _to_microbatches(a, axis):
    t = _jnp.moveaxis(a, axis, 0)
    t = t.reshape((N_MICROBATCH, t.shape[0] // N_MICROBATCH) + t.shape[1:])
    return _jnp.moveaxis(t, 1, axis + 1)


def setup_inputs(seed: int = 0) -> dict:
    inp = _fwd_setup_inputs(seed)
    key = _jax.random.fold_in(_jax.random.key(seed), 7919)
    shape, _ = _output_shape()
    out = dict(inp)
    out["loss_target"] = _jax.random.normal(_jax.random.fold_in(key, 0), shape, _jnp.float32)
    for i, name in enumerate(TWIN_WEIGHTS):
        w = inp[name].astype(_jnp.float32)
        if MOMENT_SCALE is None:
            s = _jnp.sqrt(_jnp.mean(_jnp.square(w)) + 1e-30)
        else:
            s = MOMENT_SCALE[name]
        km, kv = _jax.random.split(_jax.random.fold_in(key, i + 1))
        out[name] = w
        out["m_" + name] = s * _jax.random.normal(km, w.shape, _jnp.float32)
        out["v_" + name] = (s * s) * _jax.random.uniform(kv, w.shape, _jnp.float32, 0.5, 1.5)
    if N_MICROBATCH > 1:
        for name, axis in PER_EXAMPLE_BATCH_AXIS.items():
            out[name] = _to_microbatches(out[name], axis)
    return {'x': out['x'], 'mix_norm_g': out['mix_norm_g'], 'ffn_norm_g': out['ffn_norm_g'], 'fox_w_in': out['fox_w_in'], 'fox_b_f': out['fox_b_f'], 'fox_q_norm_g': out['fox_q_norm_g'], 'fox_k_norm_g': out['fox_k_norm_g'], 'fox_w_out': out['fox_w_out'], 'pool_w': out['pool_w'], 'pool_b': out['pool_b'], 'pool_scale': out['pool_scale'], 'ffn_w_gate_up': out['ffn_w_gate_up'], 'ffn_w_down': out['ffn_w_down'], 'loss_target': out['loss_target'], 'm_mix_norm_g': out['m_mix_norm_g'], 'm_ffn_norm_g': out['m_ffn_norm_g'], 'm_fox_w_in': out['m_fox_w_in'], 'm_fox_b_f': out['m_fox_b_f'], 'm_fox_q_norm_g': out['m_fox_q_norm_g'], 'm_fox_k_norm_g': out['m_fox_k_norm_g'], 'm_fox_w_out': out['m_fox_w_out'], 'm_pool_w': out['m_pool_w'], 'm_pool_b': out['m_pool_b'], 'm_pool_scale': out['m_pool_scale'], 'm_ffn_w_gate_up': out['m_ffn_w_gate_up'], 'm_ffn_w_down': out['m_ffn_w_down'], 'v_mix_norm_g': out['v_mix_norm_g'], 'v_ffn_norm_g': out['v_ffn_norm_g'], 'v_fox_w_in': out['v_fox_w_in'], 'v_fox_b_f': out['v_fox_b_f'], 'v_fox_q_norm_g': out['v_fox_q_norm_g'], 'v_fox_k_norm_g': out['v_fox_k_norm_g'], 'v_fox_w_out': out['v_fox_w_out'], 'v_pool_w': out['v_pool_w'], 'v_pool_b': out['v_pool_b'], 'v_pool_scale': out['v_pool_scale'], 'v_ffn_w_gate_up': out['v_ffn_w_gate_up'], 'v_ffn_w_down': out['v_ffn_w_down']}


def _loss(weights, diff, rest, loss_target):
    with _jax.named_scope("forward"):
        args = {**rest, TWIN_DIFF_INPUT: diff, **{k: w.astype(_WEIGHT_DTYPES[k]) for k, w in weights.items()}}
        y = _forward(args)
    with _jax.named_scope("loss_head"):
        err = _jnp.square(y.astype(_jnp.float32) - loss_target)
        return 0.5 * _jnp.sum(_jnp.mean(err, axis=-1)) if err.ndim else 0.5 * err


def _adamw(w, g, m, v):
    m = ADAM_B1 * m + (1.0 - ADAM_B1) * g
    v = ADAM_B2 * v + (1.0 - ADAM_B2) * _jnp.square(g)
    m_hat = m / (1.0 - ADAM_B1 ** ADAM_STEP)
    v_hat = v / (1.0 - ADAM_B2 ** ADAM_STEP)
    delta = -ADAM_LR * (m_hat / (_jnp.sqrt(v_hat) + ADAM_EPS) + ADAM_WD * w)
    return delta, m, v


def reference(x, mix_norm_g, ffn_norm_g, fox_w_in, fox_b_f, fox_q_norm_g, fox_k_norm_g, fox_w_out, pool_w, pool_b, pool_scale, ffn_w_gate_up, ffn_w_down, loss_target, m_mix_norm_g, m_ffn_norm_g, m_fox_w_in, m_fox_b_f, m_fox_q_norm_g, m_fox_k_norm_g, m_fox_w_out, m_pool_w, m_pool_b, m_pool_scale, m_ffn_w_gate_up, m_ffn_w_down, v_mix_norm_g, v_ffn_norm_g, v_fox_w_in, v_fox_b_f, v_fox_q_norm_g, v_fox_k_norm_g, v_fox_w_out, v_pool_w, v_pool_b, v_pool_scale, v_ffn_w_gate_up, v_ffn_w_down):
    given = dict(x=x, mix_norm_g=mix_norm_g, ffn_norm_g=ffn_norm_g, fox_w_in=fox_w_in, fox_b_f=fox_b_f, fox_q_norm_g=fox_q_norm_g, fox_k_norm_g=fox_k_norm_g, fox_w_out=fox_w_out, pool_w=pool_w, pool_b=pool_b, pool_scale=pool_scale, ffn_w_gate_up=ffn_w_gate_up, ffn_w_down=ffn_w_down, loss_target=loss_target, m_mix_norm_g=m_mix_norm_g, m_ffn_norm_g=m_ffn_norm_g, m_fox_w_in=m_fox_w_in, m_fox_b_f=m_fox_b_f, m_fox_q_norm_g=m_fox_q_norm_g, m_fox_k_norm_g=m_fox_k_norm_g, m_fox_w_out=m_fox_w_out, m_pool_w=m_pool_w, m_pool_b=m_pool_b, m_pool_scale=m_pool_scale, m_ffn_w_gate_up=m_ffn_w_gate_up, m_ffn_w_down=m_ffn_w_down, v_mix_norm_g=v_mix_norm_g, v_ffn_norm_g=v_ffn_norm_g, v_fox_w_in=v_fox_w_in, v_fox_b_f=v_fox_b_f, v_fox_q_norm_g=v_fox_q_norm_g, v_fox_k_norm_g=v_fox_k_norm_g, v_fox_w_out=v_fox_w_out, v_pool_w=v_pool_w, v_pool_b=v_pool_b, v_pool_scale=v_pool_scale, v_ffn_w_gate_up=v_ffn_w_gate_up, v_ffn_w_down=v_ffn_w_down)
    weights = {n: given[n] for n in TWIN_WEIGHTS}
    shared = {n: given[n] for n in SHARED_INPUTS}
    per_example = {n: given[n] for n in ['x']}
    grad_fn = _jax.value_and_grad(_loss, argnums=(0, 1))

    def one_microbatch(ex, loss_target):
        ex = dict(ex)
        diff = ex.pop(TWIN_DIFF_INPUT)
        return grad_fn(weights, diff, {**shared, **ex}, loss_target)

    if N_MICROBATCH == 1:
        loss, (grad_w, grad_x) = one_microbatch(per_example, given["loss_target"])
    else:
        def body(carry, xs):
            loss_sum, grad_sum = carry
            l_k, (gw_k, gx_k) = one_microbatch(xs[0], xs[1])
            with _jax.named_scope("update"):
                return (loss_sum + l_k, _jax.tree.map(_jnp.add, grad_sum, gw_k)), gx_k

        init = (_jnp.zeros((), _jnp.float32), _jax.tree.map(_jnp.zeros_like, weights))
        (loss, grad_w), grad_x = _jax.lax.scan(body, init, (per_example, given["loss_target"]))
    with _jax.named_scope("update"):
        delta_w, new_m, new_v = {}, {}, {}
        for n in TWIN_WEIGHTS:
            delta_w[n], new_m[n], new_v[n] = _adamw(weights[n], grad_w[n], given["m_" + n], given["v_" + n])
    return (loss, grad_x, *[grad_w[n] for n in TWIN_WEIGHTS], *[delta_w[n] for n in TWIN_WEIGHTS],
            *[new_m[n] for n in TWIN_WEIGHTS], *[new_v[n] for n in TWIN_WEIGHTS])
```

```python
import functools

import jax
import jax.numpy as jnp
from jax import lax
from jax.experimental import pallas as pl
from jax.experimental.pallas import tpu as pltpu

F32 = jnp.float32
BF16 = jnp.bfloat16
MESH = pl.DeviceIdType.MESH

HEAD_DIM = 128
RMS_EPS = 1e-6
NEG_INF = -1e30
POOL_WINDOWS = (2, 4, 8, 16)
POOL_HALO = 16
LANES = 128
N_CHIPS = 4
N_DEV = 8
VMEM_LIMIT = 56 * 1024 * 1024

ADAM_LR = 0.001
ADAM_B1 = 0.9
ADAM_B2 = 0.999
ADAM_EPS = 1e-08
ADAM_WD = 0.01
ADAM_STEP = 10

NN = (((1,), (0,)), ((), ()))
NT = (((1,), (1,)), ((), ()))
TN = (((0,), (0,)), ((), ()))


def _params(*sem):
    return pltpu.CompilerParams(dimension_semantics=sem if sem else None, vmem_limit_bytes=VMEM_LIMIT)


def _tile(n, pref, unit=LANES):
    best = None
    t = unit
    while t <= min(n, pref):
        if n % t == 0:
            best = t
        t += unit
    return best if best is not None else n


def _row_tile(rows, cols, itemsize, budget):
    best = None
    for t in range(16, rows + 1, 16):
        if rows % t == 0 and t * cols * itemsize <= budget:
            best = t
    return best if best is not None else rows


def _dot(a, b, dims):
    return lax.dot_general(a, b, dims, preferred_element_type=F32)


def _matmul(a, b, mode, out_dtype, name, *, M, N, K, tm=1024, tn=1024, tk=2048, res=None,
            a_spec=None, b_spec=None):
    tm, tn, tk = _tile(M, tm), _tile(N, tn), _tile(K, tk)
    nk = K // tk
    dims = {"nn": NN, "nt": NT, "tn": TN}[mode]
    if a_spec is None:
        a_spec = (pl.BlockSpec((tk, tm), lambda i, j, k: (k, i)) if mode == "tn"
                  else pl.BlockSpec((tm, tk), lambda i, j, k: (i, k)))
    else:
        a_spec = a_spec(tm, tn, tk)
    if b_spec is None:
        b_spec = (pl.BlockSpec((tn, tk), lambda i, j, k: (j, k)) if mode == "nt"
                  else pl.BlockSpec((tk, tn), lambda i, j, k: (k, j)))
    else:
        b_spec = b_spec(tm, tn, tk)
    o_spec = pl.BlockSpec((tm, tn), lambda i, j, k: (i, j))
    has_res = res is not None

    def body(a_ref, b_ref, *rest):
        if has_res:
            r_ref, o_ref, acc = rest
        else:
            o_ref, acc = rest

        def finish(total):
            if has_res:
                total = total + r_ref[...]
            o_ref[...] = total.astype(o_ref.dtype)

        part = _dot(a_ref[...], b_ref[...], dims)
        if nk == 1:
            finish(part)
        else:
            k = pl.program_id(2)

            @pl.when(k == 0)
            def _():
                acc[...] = part

            @pl.when(k > 0)
            def _():
                acc[...] += part

            @pl.when(k == nk - 1)
            def _():
                finish(acc[...])

    operands = [a, b] + ([res] if has_res else [])
    in_specs = [a_spec, b_spec] + ([o_spec] if has_res else [])
    return pl.pallas_call(
        body, name=name, grid=(M // tm, N // tn, nk), in_specs=in_specs, out_specs=o_spec,
        out_shape=jax.ShapeDtypeStruct((M, N), out_dtype),
        scratch_shapes=[pltpu.VMEM((tm, tn) if nk > 1 else (8, LANES), F32)],
        compiler_params=_params("parallel", "parallel", "arbitrary"),
    )(*operands)


def _rmsnorm_fwd(x, g, out_dtype, name):
    S, D = x.shape
    tm = _tile(S, 512, 16)

    def body(x_ref, g_ref, o_ref):
        xv = x_ref[...]
        r = lax.rsqrt(jnp.mean(xv * xv, axis=-1, keepdims=True) + RMS_EPS)
        o_ref[...] = ((xv * r) * g_ref[...]).astype(o_ref.dtype)

    row = pl.BlockSpec((tm, D), lambda i: (i, 0))
    return pl.pallas_call(
        body, name=name, grid=(S // tm,), in_specs=[row, pl.BlockSpec((1, D), lambda i: (0, 0))],
        out_specs=row, out_shape=jax.ShapeDtypeStruct((S, D), out_dtype),
        compiler_params=_params("parallel"),
    )(x, g)


def _rmsnorm_bwd(x, g, dh, dres, name):
    S, D = x.shape
    tm = _tile(S, 256, 16)

    def body(x_ref, g_ref, dh_ref, dres_ref, dx_ref, dxb_ref, dg_ref):
        xv = x_ref[...]
        r = lax.rsqrt(jnp.mean(xv * xv, axis=-1, keepdims=True) + RMS_EPS)
        xhat = xv * r
        dhv = dh_ref[...]
        dxhat = dhv * g_ref[...]
        dx = dres_ref[...] + r * (dxhat - xhat * jnp.mean(dxhat * xhat, axis=-1, keepdims=True))
        dx_ref[...] = dx
        dxb_ref[...] = dx.astype(BF16)

        @pl.when(pl.program_id(0) == 0)
        def _():
            dg_ref[...] = jnp.zeros_like(dg_ref)

        dg_ref[...] += jnp.sum(dhv * xhat, axis=0, keepdims=True)

    row = pl.BlockSpec((tm, D), lambda i: (i, 0))
    vec = pl.BlockSpec((1, D), lambda i: (0, 0))
    return pl.pallas_call(
        body, name=name, grid=(S // tm,), in_specs=[row, vec, row, row], out_specs=[row, row, vec],
        out_shape=[jax.ShapeDtypeStruct((S, D), F32), jax.ShapeDtypeStruct((S, D), BF16),
                   jax.ShapeDtypeStruct((1, D), F32)],
        compiler_params=_params("arbitrary"),
    )(x, g, dh, dres)


def _loss_grad(y, target, name):
    S, D = y.shape
    tm = _tile(S, 256, 16)

    def body(y_ref, t_ref, dy_ref, dyb_ref, sq_ref):
        e = y_ref[...] - t_ref[...]
        dy = e / D
        dy_ref[...] = dy
        dyb_ref[...] = dy.astype(BF16)

        @pl.when(pl.program_id(0) == 0)
        def _():
            sq_ref[...] = jnp.zeros_like(sq_ref)

        total = jnp.sum(jnp.sum(e * e, axis=1, keepdims=True), axis=0, keepdims=True)
        sq_ref[...] += jnp.broadcast_to(total, sq_ref.shape)

    row = pl.BlockSpec((tm, D), lambda i: (i, 0))
    vec = pl.BlockSpec((1, LANES), lambda i: (0, 0))
    return pl.pallas_call(
        body, name=name, grid=(S // tm,), in_specs=[row, row], out_specs=[row, row, vec],
        out_shape=[jax.ShapeDtypeStruct((S, D), F32), jax.ShapeDtypeStruct((S, D), BF16),
                   jax.ShapeDtypeStruct((1, LANES), F32)],
        compiler_params=_params("arbitrary"),
    )(y, target)


def _ffn_up(hb, w_gu, name):
    S, D = hb.shape
    F = w_gu.shape[1] // 2
    tm, tn, tk = _tile(S, 1024), _tile(F, 512), _tile(D, 2048)
    nk, nj = D // tk, F // tn

    def body(a_ref, wg_ref, wu_ref, gu_ref, act_ref, accg, accu):
        pg = _dot(a_ref[...], wg_ref[...], NN)
        pu = _dot(a_ref[...], wu_ref[...], NN)

        def finish(g, u):
            gu_ref[0] = g
            gu_ref[1] = u
            act_ref[...] = ((g / (1.0 + jnp.exp(-g))) * u).astype(BF16)

        if nk == 1:
            finish(pg, pu)
        else:
            k = pl.program_id(2)

            @pl.when(k == 0)
            def _():
                accg[...] = pg
                accu[...] = pu

            @pl.when(k > 0)
            def _():
                accg[...] += pg
                accu[...] += pu

            @pl.when(k == nk - 1)
            def _():
                finish(accg[...], accu[...])

    acc_shape = (tm, tn) if nk > 1 else (8, LANES)
    return pl.pallas_call(
        body, name=name, grid=(S // tm, nj, nk),
        in_specs=[pl.BlockSpec((tm, tk), lambda i, j, k: (i, k)),
                  pl.BlockSpec((tk, tn), lambda i, j, k: (k, j)),
                  pl.BlockSpec((tk, tn), lambda i, j, k: (k, j + nj))],
        out_specs=[pl.BlockSpec((2, tm, tn), lambda i, j, k: (0, i, j)),
                   pl.BlockSpec((tm, tn), lambda i, j, k: (i, j))],
        out_shape=[jax.ShapeDtypeStruct((2, S, F), F32), jax.ShapeDtypeStruct((S, F), BF16)],
        scratch_shapes=[pltpu.VMEM(acc_shape, F32), pltpu.VMEM(acc_shape, F32)],
        compiler_params=_params("parallel", "parallel", "arbitrary"),
    )(hb, w_gu, w_gu)


def _ffn_act_bwd(dyb, w_down, gu, name):
    S, D = dyb.shape
    F = w_down.shape[0]
    tm, tn, tk = _tile(S, 1024), _tile(F, 512), _tile(D, 2048)
    nk = D // tk

    def body(a_ref, w_ref, gu_ref, dgu_ref, acc):
        part = _dot(a_ref[...], w_ref[...], NT)

        def finish(dact):
            g = gu_ref[0]
            u = gu_ref[1]
            sg = 1.0 / (1.0 + jnp.exp(-g))
            dgu_ref[0] = (dact * u * (sg * (1.0 + g * (1.0 - sg)))).astype(BF16)
            dgu_ref[1] = (dact * (g * sg)).astype(BF16)

        if nk == 1:
            finish(part)
        else:
            k = pl.program_id(2)

            @pl.when(k == 0)
            def _():
                acc[...] = part

            @pl.when(k > 0)
            def _():
                acc[...] += part

            @pl.when(k == nk - 1)
            def _():
                finish(acc[...])

    gu_spec = pl.BlockSpec((2, tm, tn), lambda i, j, k: (0, i, j))
    return pl.pallas_call(
        body, name=name, grid=(S // tm, F // tn, nk),
        in_specs=[pl.BlockSpec((tm, tk), lambda i, j, k: (i, k)),
                  pl.BlockSpec((tn, tk), lambda i, j, k: (j, k)), gu_spec],
        out_specs=gu_spec, out_shape=jax.ShapeDtypeStruct((2, S, F), BF16),
        scratch_shapes=[pltpu.VMEM((tm, tn) if nk > 1 else (8, LANES), F32)],
        compiler_params=_params("parallel", "parallel", "arbitrary"),
    )(dyb, w_down, gu)


def _qkv_post(proj, gq, gk, name):
    S, D3 = proj.shape
    D = D3 // 3
    tm = _tile(S, 256, 16)

    def body(q_ref, k_ref, v_ref, gq_ref, gk_ref, qn_ref, kn_ref, vb_ref):
        for src, g_ref, dst in ((q_ref, gq_ref, qn_ref), (k_ref, gk_ref, kn_ref)):
            for h in range(D // HEAD_DIM):
                cols = slice(h * HEAD_DIM, (h + 1) * HEAD_DIM)
                t = src[:, cols]
                r = lax.rsqrt(jnp.mean(t * t, axis=-1, keepdims=True) + RMS_EPS)
                dst[:, cols] = ((t * r) * g_ref[...]).astype(BF16)
        vb_ref[...] = v_ref[...].astype(BF16)

    part = lambda n: pl.BlockSpec((tm, D), lambda i: (i, n))
    vec = pl.BlockSpec((1, HEAD_DIM), lambda i: (0, 0))
    row = pl.BlockSpec((tm, D), lambda i: (i, 0))
    return pl.pallas_call(
        body, name=name, grid=(S // tm,), in_specs=[part(0), part(1), part(2), vec, vec],
        out_specs=[row, row, row], out_shape=[jax.ShapeDtypeStruct((S, D), BF16)] * 3,
        compiler_params=_params("parallel"),
    )(proj, proj, proj, gq, gk)


def _qkv_post_bwd(proj, dqn, dkn, dvb, gq, gk, name):
    S, D = dqn.shape
    tm = _tile(S, 256, 16)

    def body(q_ref, k_ref, dqn_ref, dkn_ref, dvb_ref, gq_ref, gk_ref, dp_ref, dgq_ref, dgk_ref):
        @pl.when(pl.program_id(0) == 0)
        def _():
            dgq_ref[...] = jnp.zeros_like(dgq_ref)
            dgk_ref[...] = jnp.zeros_like(dgk_ref)

        for n, (src, dsrc, g_ref, dg_ref) in enumerate(((q_ref, dqn_ref, gq_ref, dgq_ref),
                                                         (k_ref, dkn_ref, gk_ref, dgk_ref))):
            dg = jnp.zeros((1, HEAD_DIM), F32)
            for h in range(D // HEAD_DIM):
                cols = slice(h * HEAD_DIM, (h + 1) * HEAD_DIM)
                t = src[:, cols]
                r = lax.rsqrt(jnp.mean(t * t, axis=-1, keepdims=True) + RMS_EPS)
                that = t * r
                dn = dsrc[:, cols]
                dhat = dn * g_ref[...]
                dt = r * (dhat - that * jnp.mean(dhat * that, axis=-1, keepdims=True))
                dp_ref[:, n * D + h * HEAD_DIM:n * D + (h + 1) * HEAD_DIM] = dt.astype(BF16)
                dg = dg + jnp.sum(dn * that, axis=0, keepdims=True)
            dg_ref[...] += dg
        dp_ref[:, 2 * D:3 * D] = dvb_ref[...]

    part = lambda n: pl.BlockSpec((tm, D), lambda i: (i, n))
    row = pl.BlockSpec((tm, D), lambda i: (i, 0))
    vec = pl.BlockSpec((1, HEAD_DIM), lambda i: (0, 0))
    return pl.pallas_call(
        body, name=name, grid=(S // tm,), in_specs=[part(0), part(1), row, row, row, vec, vec],
        out_specs=[pl.BlockSpec((tm, 3 * D), lambda i: (i, 0)), vec, vec],
        out_shape=[jax.ShapeDtypeStruct((S, 3 * D), BF16), jax.ShapeDtypeStruct((1, HEAD_DIM), F32),
                   jax.ShapeDtypeStruct((1, HEAD_DIM), F32)],
        compiler_params=_params("arbitrary"),
    )(proj, proj, dqn, dkn, dvb, gq, gk)


def _log_sigmoid(z):
    return -(jnp.maximum(-z, 0.0) + jnp.log(1.0 + jnp.exp(-jnp.abs(z))))


def _fgate_fwd(fl, bf, name):
    S = fl.shape[0]
    T = _tile(S, 256, 16)

    def body(fl_ref, bf_ref, c_ref, carry):
        @pl.when(pl.program_id(0) == 0)
        def _():
            carry[...] = jnp.zeros_like(carry)

        logf = _log_sigmoid(fl_ref[...] + bf_ref[...])
        r = lax.broadcasted_iota(jnp.int32, (T, T), 0)
        c = lax.broadcasted_iota(jnp.int32, (T, T), 1)
        tri = (r >= c).astype(F32)
        cum = jnp.dot(tri, logf, precision=lax.Precision.HIGHEST, preferred_element_type=F32) + carry[...]
        c_ref[...] = cum
        carry[...] = cum[T - 1:T, :]

    row = pl.BlockSpec((T, LANES), lambda i: (i, 0))
    return pl.pallas_call(
        body, name=name, grid=(S // T,), in_specs=[row, pl.BlockSpec((1, LANES), lambda i: (0, 0))],
        out_specs=row, out_shape=jax.ShapeDtypeStruct((S, LANES), F32),
        scratch_shapes=[pltpu.VMEM((1, LANES), F32)], compiler_params=_params("arbitrary"),
    )(fl, bf)


def _fgate_bwd(dc, fl, bf, name):
    S = fl.shape[0]
    T = _tile(S, 256, 16)
    nb = S // T

    def body(dc_ref, fl_ref, bf_ref, dfl_ref, dbf_ref, carry):
        @pl.when(pl.program_id(0) == 0)
        def _():
            carry[...] = jnp.zeros_like(carry)
            dbf_ref[...] = jnp.zeros_like(dbf_ref)

        r = lax.broadcasted_iota(jnp.int32, (T, T), 0)
        c = lax.broadcasted_iota(jnp.int32, (T, T), 1)
        triu = (c >= r).astype(F32)
        dlogf = jnp.dot(triu, dc_ref[...], precision=lax.Precision.HIGHEST,
                        preferred_element_type=F32) + carry[...]
        carry[...] = dlogf[0:1, :]
        z = fl_ref[...] + bf_ref[...]
        dz = dlogf * (1.0 / (1.0 + jnp.exp(z)))
        dfl_ref[...] = dz.astype(BF16)
        dbf_ref[...] += jnp.sum(dz, axis=0, keepdims=True)

    row = pl.BlockSpec((T, LANES), lambda i: (nb - 1 - i, 0))
    vec = pl.BlockSpec((1, LANES), lambda i: (0, 0))
    return pl.pallas_call(
        body, name=name, grid=(nb,), in_specs=[row, row, vec], out_specs=[row, vec],
        out_shape=[jax.ShapeDtypeStruct((S, LANES), BF16), jax.ShapeDtypeStruct((1, LANES), F32)],
        scratch_shapes=[pltpu.VMEM((1, LANES), F32)], compiler_params=_params("arbitrary"),
    )(dc, fl, bf)


def _attn_logits(q_ref, k_ref, cq_ref, ck_ref, masked, T):
    s = _dot(q_ref[...], k_ref[...], NT) * (HEAD_DIM ** -0.5)
    s = s + (cq_ref[...] - ck_ref[...])
    if masked:
        r = lax.broadcasted_iota(jnp.int32, (T, T), 0)
        c = lax.broadcasted_iota(jnp.int32, (T, T), 1)
        s = jnp.where(r >= c, s, NEG_INF)
    return s


def _flash_fwd(qn, kn, vb, c_col, c_row, name):
    S, D = qn.shape
    H = D // HEAD_DIM
    T = _tile(S, 512)
    nb = S // T

    def body(q_ref, k_ref, v_ref, cq_ref, ck_ref, o_ref, lse_ref, m_s, l_s, acc_s):
        i = pl.program_id(1)
        j = pl.program_id(2)

        @pl.when(j == 0)
        def _():
            m_s[...] = jnp.full_like(m_s, NEG_INF)
            l_s[...] = jnp.zeros_like(l_s)
            acc_s[...] = jnp.zeros_like(acc_s)

        def step(masked):
            s = _attn_logits(q_ref, k_ref, cq_ref, ck_ref, masked, T)
            m_prev = m_s[...]
            m_new = jnp.maximum(m_prev, jnp.max(s, axis=1, keepdims=True))
            p = jnp.exp(s - m_new)
            alpha = jnp.exp(m_prev - m_new)
            l_s[...] = alpha * l_s[...] + jnp.sum(p, axis=1, keepdims=True)
            acc_s[...] = alpha * acc_s[...] + _dot(p.astype(BF16), v_ref[...], NN)
            m_s[...] = m_new

        @pl.when(j < i)
        def _():
            step(False)

        @pl.when(j == i)
        def _():
            step(True)
            o_ref[...] = (acc_s[...] / l_s[...]).astype(BF16)
            lse_ref[...] = m_s[...] + jnp.log(l_s[...])

    qspec = pl.BlockSpec((T, HEAD_DIM), lambda h, i, j: (i, h))
    kspec = pl.BlockSpec((T, HEAD_DIM), lambda h, i, j: (jnp.minimum(j, i), h))
    col = pl.BlockSpec((None, T, 1), lambda h, i, j: (h, i, 0))
    return pl.pallas_call(
        body, name=name, grid=(H, nb, nb),
        in_specs=[qspec, kspec, kspec, col,
                  pl.BlockSpec((None, 1, T), lambda h, i, j: (h, 0, jnp.minimum(j, i)))],
        out_specs=[qspec, col],
        out_shape=[jax.ShapeDtypeStruct((S, D), BF16), jax.ShapeDtypeStruct((H, S, 1), F32)],
        scratch_shapes=[pltpu.VMEM((T, 1), F32), pltpu.VMEM((T, 1), F32), pltpu.VMEM((T, HEAD_DIM), F32)],
        compiler_params=_params("parallel", "parallel", "arbitrary"),
    )(qn, kn, vb, c_col, c_row)


def _attn_delta(do, o, name):
    S, D = do.shape
    H = D // HEAD_DIM
    tm = _tile(S, 256, 16)

    def body(do_ref, o_ref, dl_ref, dob_ref):
        for h in range(H):
            cols = slice(h * HEAD_DIM, (h + 1) * HEAD_DIM)
            dl_ref[h] = jnp.sum(do_ref[:, cols] * o_ref[:, cols].astype(F32), axis=1, keepdims=True)
        dob_ref[...] = do_ref[...].astype(BF16)

    row = pl.BlockSpec((tm, D), lambda i: (i, 0))
    return pl.pallas_call(
        body, name=name, grid=(S // tm,), in_specs=[row, row],
        out_specs=[pl.BlockSpec((H, tm, 1), lambda i: (0, i, 0)), row],
        out_shape=[jax.ShapeDtypeStruct((H, S, 1), F32), jax.ShapeDtypeStruct((S, D), BF16)],
        compiler_params=_params("parallel"),
    )(do, o)


def _flash_bwd(qn, kn, vb, dob, lse, delta, c_col, c_row, name):
    S, D = qn.shape
    H = D // HEAD_DIM
    T = _tile(S, 512)
    nb = S // T
    scale = HEAD_DIM ** -0.5

    def body(q_ref, k_ref, v_ref, do_ref, lse_ref, dl_ref, cq_ref, ck_ref,
             dq_ref, dk_ref, dv_ref, dcq_ref, dck_ref, dv_acc):
        j = pl.program_id(1)
        i = pl.program_id(2)

        @pl.when((j == 0) & (i == 0))
        def _():
            dq_ref[...] = jnp.zeros_like(dq_ref)
            dcq_ref[...] = jnp.zeros_like(dcq_ref)

        @pl.when(i == j)
        def _():
            dk_ref[...] = jnp.zeros_like(dk_ref)
            dck_ref[...] = jnp.zeros_like(dck_ref)
            dv_acc[...] = jnp.zeros_like(dv_acc)

        def step(masked):
            s = _attn_logits(q_ref, k_ref, cq_ref, ck_ref, masked, T)
            p = jnp.exp(s - lse_ref[...])
            dv_acc[...] += _dot(p.astype(BF16), do_ref[...], TN)
            dp = _dot(do_ref[...], v_ref[...], NT)
            ds = p * (dp - dl_ref[...])
            dsb = ds.astype(BF16)
            dk_ref[...] += _dot(dsb, q_ref[...], TN) * scale
            rows = pl.ds(pl.multiple_of(i * T, T), T)
            dq_ref[rows, :] += _dot(dsb, k_ref[...], NN) * scale
            dcq_ref[rows, :] += jnp.sum(ds, axis=1, keepdims=True)
            dck_ref[...] += jnp.sum(ds, axis=0, keepdims=True)

        @pl.when(i > j)
        def _():
            step(False)

        @pl.when(i == j)
        def _():
            step(True)

        @pl.when(i == nb - 1)
        def _():
            dv_ref[...] = dv_acc[...].astype(BF16)

    qspec = pl.BlockSpec((T, HEAD_DIM), lambda h, j, i: (jnp.maximum(i, j), h))
    kspec = pl.BlockSpec((T, HEAD_DIM), lambda h, j, i: (j, h))
    qcol = pl.BlockSpec((None, T, 1), lambda h, j, i: (h, jnp.maximum(i, j), 0))
    krow = pl.BlockSpec((None, 1, T), lambda h, j, i: (h, 0, j))
    return pl.pallas_call(
        body, name=name, grid=(H, nb, nb),
        in_specs=[qspec, kspec, kspec, qspec, qcol, qcol, qcol, krow],
        out_specs=[pl.BlockSpec((S, HEAD_DIM), lambda h, j, i: (0, h)), kspec, kspec,
                   pl.BlockSpec((None, S, 1), lambda h, j, i: (h, 0, 0)), krow],
        out_shape=[jax.ShapeDtypeStruct((S, D), F32), jax.ShapeDtypeStruct((S, D), F32),
                   jax.ShapeDtypeStruct((S, D), BF16), jax.ShapeDtypeStruct((H, S, 1), F32),
                   jax.ShapeDtypeStruct((H, 1, S), F32)],
        scratch_shapes=[pltpu.VMEM((T, HEAD_DIM), F32)],
        compiler_params=_params("parallel", "arbitrary", "arbitrary"),
    )(qn, kn, vb, dob, lse, delta, c_col, c_row)


def _pool_counts(first_row, tm, win):
    t = first_row + lax.broadcasted_iota(jnp.int32, (tm, 1), 0)
    return jnp.minimum(t + 1, win).astype(F32)


def _pool_fwd(h, x, wp, b, scale, name):
    S, D = h.shape
    G = D // len(POOL_WINDOWS)
    tm = _tile(S, 256, 16)

    def body(h_ref, halo_ref, x_ref, w_ref, b_ref, s_ref, y_ref, o_ref, ext):
        i = pl.program_id(0)
        ext[POOL_HALO:, :] = h_ref[...]

        @pl.when(i == 0)
        def _():
            ext[0:POOL_HALO, :] = jnp.zeros((POOL_HALO, D), F32)

        @pl.when(i > 0)
        def _():
            ext[0:POOL_HALO, :] = halo_ref[...]

        for g, win in enumerate(POOL_WINDOWS):
            cols = slice(g * G, (g + 1) * G)
            tot = ext[POOL_HALO:POOL_HALO + tm, cols]
            for k in range(1, win):
                tot = tot + ext[POOL_HALO - k:POOL_HALO - k + tm, cols]
            y = (tot / _pool_counts(i * tm, tm, win) - h_ref[:, cols]).astype(BF16)
            y_ref[:, cols] = y
            z = _dot(y, w_ref[g], NN)
            o_ref[:, cols] = x_ref[:, cols] + (z + b_ref[:, cols]) * s_ref[:, cols]

    row = pl.BlockSpec((tm, D), lambda i: (i, 0))
    vec = pl.BlockSpec((1, D), lambda i: (0, 0))
    halo = pl.BlockSpec((POOL_HALO, D), lambda i: (jnp.maximum(i * (tm // POOL_HALO) - 1, 0), 0))
    return pl.pallas_call(
        body, name=name, grid=(S // tm,),
        in_specs=[row, halo, row, pl.BlockSpec((len(POOL_WINDOWS), G, G), lambda i: (0, 0, 0)), vec, vec],
        out_specs=[row, row],
        out_shape=[jax.ShapeDtypeStruct((S, D), BF16), jax.ShapeDtypeStruct((S, D), F32)],
        scratch_shapes=[pltpu.VMEM((tm + POOL_HALO, D), F32)], compiler_params=_params("parallel"),
    )(h, h, x, wp, b, scale)


def _pool_bwd_mix(dout, yb, wp, b, scale, name):
    S, D = dout.shape
    NG = len(POOL_WINDOWS)
    G = D // NG
    tm = _tile(S, 256, 16)

    def body(do_ref, y_ref, w_ref, b_ref, s_ref, dyc_ref, dw_ref, db_ref, ds_ref):
        i = pl.program_id(0)

        @pl.when(i == 0)
        def _():
            dw_ref[...] = jnp.zeros_like(dw_ref)
            db_ref[...] = jnp.zeros_like(db_ref)
            ds_ref[...] = jnp.zeros_like(ds_ref)

        for g, win in enumerate(POOL_WINDOWS):
            cols = slice(g * G, (g + 1) * G)
            y = y_ref[:, cols]
            dz = do_ref[:, cols]
            zb = _dot(y, w_ref[g], NN) + b_ref[:, cols]
            ds_ref[:, cols] += jnp.sum(dz * zb, axis=0, keepdims=True)
            dzb = dz * s_ref[:, cols]
            db_ref[:, cols] += jnp.sum(dzb, axis=0, keepdims=True)
            dzb16 = dzb.astype(BF16)
            dw_ref[g] += _dot(y, dzb16, TN)
            dy = _dot(dzb16, w_ref[g], NT)
            dyc_ref[:, cols] = dy / _pool_counts(i * tm, tm, win)

    row = pl.BlockSpec((tm, D), lambda i: (i, 0))
    vec = pl.BlockSpec((1, D), lambda i: (0, 0))
    wspec = pl.BlockSpec((NG, G, G), lambda i: (0, 0, 0))
    return pl.pallas_call(
        body, name=name, grid=(S // tm,), in_specs=[row, row, wspec, vec, vec],
        out_specs=[row, wspec, vec, vec],
        out_shape=[jax.ShapeDtypeStruct((S, D), F32), jax.ShapeDtypeStruct((NG, G, G), F32),
                   jax.ShapeDtypeStruct((1, D), F32), jax.ShapeDtypeStruct((1, D), F32)],
        compiler_params=_params("arbitrary"),
    )(dout, yb, wp, b, scale)


def _pool_bwd_window(dyc, name):
    S, D = dyc.shape
    G = D // len(POOL_WINDOWS)
    tm = _tile(S, 256, 16)
    nb = S // tm

    def body(d_ref, halo_ref, dh_ref, ext):
        i = pl.program_id(0)
        ext[0:tm, :] = d_ref[...]

        @pl.when(i == nb - 1)
        def _():
            ext[tm:tm + POOL_HALO, :] = jnp.zeros((POOL_HALO, D), F32)

        @pl.when(i < nb - 1)
        def _():
            ext[tm:tm + POOL_HALO, :] = halo_ref[...]

        for g, win in enumerate(POOL_WINDOWS):
            cols = slice(g * G, (g + 1) * G)
            tot = ext[0:tm, cols] * (1.0 - _pool_counts(i * tm, tm, win))
            for k in range(1, win):
                tot = tot + ext[k:k + tm, cols]
            dh_ref[:, cols] = tot

    row = pl.BlockSpec((tm, D), lambda i: (i, 0))
    halo = pl.BlockSpec((POOL_HALO, D),
                        lambda i: (jnp.minimum((i + 1) * (tm // POOL_HALO), S // POOL_HALO - 1), 0))
    return pl.pallas_call(
        body, name=name, grid=(nb,), in_specs=[row, halo], out_specs=row,
        out_shape=jax.ShapeDtypeStruct((S, D), F32),
        scratch_shapes=[pltpu.VMEM((tm + POOL_HALO, D), F32)], compiler_params=_params("parallel"),
    )(dyc, dyc)


def _adamw(w, g, m, v, name):
    R, C = w.shape
    tr = _row_tile(R, C, 4, 1 << 20)

    def body(w_ref, g_ref, m_ref, v_ref, d_ref, nm_ref, nv_ref):
        gv = g_ref[...]
        m_new = ADAM_B1 * m_ref[...] + (1.0 - ADAM_B1) * gv
        v_new = ADAM_B2 * v_ref[...] + (1.0 - ADAM_B2) * (gv * gv)
        m_hat = m_new / (1.0 - ADAM_B1 ** ADAM_STEP)
        v_hat = v_new / (1.0 - ADAM_B2 ** ADAM_STEP)
        d_ref[...] = -ADAM_LR * (m_hat / (jnp.sqrt(v_hat) + ADAM_EPS) + ADAM_WD * w_ref[...])
        nm_ref[...] = m_new
        nv_ref[...] = v_new

    row = pl.BlockSpec((tr, C), lambda i: (i, 0))
    return pl.pallas_call(
        body, name=name, grid=(R // tr,), in_specs=[row] * 4, out_specs=[row] * 3,
        out_shape=[jax.ShapeDtypeStruct((R, C), F32)] * 3, compiler_params=_params("parallel"),
    )(w, g, m, v)


def _sum_core_halves(g, r1, c_idx, name):
    _, _, Rh, C = g.shape
    tr = _row_tile(Rh, C, 4, 2 << 20)

    def body(c_ref, g_ref, r_ref, o_ref):
        o_ref[...] = g_ref[...] + r_ref[...]

    grid_spec = pltpu.PrefetchScalarGridSpec(
        num_scalar_prefetch=1, grid=(N_CHIPS, Rh // tr),
        in_specs=[pl.BlockSpec((None, None, tr, C), lambda s, r, c_ref: (s, c_ref[0], r, 0)),
                  pl.BlockSpec((None, tr, C), lambda s, r, c_ref: (s, r, 0))],
        out_specs=pl.BlockSpec((None, tr, C), lambda s, r, c_ref: (s, r, 0)))
    return pl.pallas_call(
        body, name=name, grid_spec=grid_spec, out_shape=jax.ShapeDtypeStruct((N_CHIPS, Rh, C), F32),
        compiler_params=_params("parallel", "parallel"),
    )(c_idx, g, r1)


def _sum_chips(r2, name):
    _, Rh, C = r2.shape
    tr = _row_tile(Rh, C, 4, 1 << 20)

    def body(r_ref, o_ref):
        o_ref[...] = ((r_ref[0] + r_ref[1]) + r_ref[2]) + r_ref[3]

    return pl.pallas_call(
        body, name=name, grid=(Rh // tr,), in_specs=[pl.BlockSpec((N_CHIPS, tr, C), lambda r: (0, r, 0))],
        out_specs=pl.BlockSpec((tr, C), lambda r: (r, 0)), out_shape=jax.ShapeDtypeStruct((Rh, C), F32),
        compiler_params=_params("parallel"),
    )(r2)


ANY = pl.BlockSpec(memory_space=pl.ANY)


def _place():
    x, y, c = lax.axis_index("x"), lax.axis_index("y"), lax.axis_index("c")
    chips = [(1 - x, y), (x, 1 - y), (1 - x, 1 - y)]
    return x, y, c, chips, [2 * cx + cy for cx, cy in chips]


def _allgather_chips(shards, name):
    n = len(shards)

    def body(*refs):
        ins, outs = refs[:n], refs[n:2 * n]
        send_sems, recv_sems, local_sems = refs[2 * n:]
        x, y, c, chips, chip_idx = _place()
        me = 2 * x + y
        sibling = (x, y, 1 - c)

        def copy(t, k, src, dst, to):
            return pltpu.make_async_remote_copy(src_ref=src, dst_ref=dst, send_sem=send_sems.at[t, k],
                                                recv_sem=recv_sems.at[t, k], device_id=to, device_id_type=MESH)

        mine, first, passed = [], [], []
        for t in range(n):
            lc = pltpu.make_async_copy(ins[t], outs[t].at[me], local_sems.at[t])
            lc.start()
            mine.append(lc)
            for j, chip in enumerate(chips):
                cp = copy(t, j, ins[t].at[c], outs[t].at[me, c], (*chip, c))
                cp.start()
                first.append(cp)
        for t in range(n):
            for j, chip in enumerate(chips):
                landed = outs[t].at[chip_idx[j], c]
                copy(t, j, landed, landed, (*chip, c)).wait_recv()
                fw = copy(t, 3 + j, landed, landed, sibling)
                fw.start()
                passed.append(fw)
        for t in range(n):
            for j in range(3):
                other = outs[t].at[chip_idx[j], 1 - c]
                copy(t, 3 + j, other, other, sibling).wait_recv()
        for cp in first + passed:
            cp.wait_send()
        for lc in mine:
            lc.wait()

    return pl.pallas_call(
        body, name=name, in_specs=[ANY] * n, out_specs=[ANY] * n,
        out_shape=[jax.ShapeDtypeStruct((N_CHIPS,) + s.shape, s.dtype) for s in shards],
        scratch_shapes=[pltpu.SemaphoreType.DMA((n, 6)), pltpu.SemaphoreType.DMA((n, 6)),
                        pltpu.SemaphoreType.DMA((n,))],
    )(*shards)


def _send_other_half_to_sibling(gs, name):
    n = len(gs)

    def body(*refs):
        ins, outs = refs[:n], refs[n:2 * n]
        send_sems, recv_sems = refs[2 * n:]
        x, y, c, _, _ = _place()
        copies = []
        for t in range(n):
            for s in range(N_CHIPS):
                cp = pltpu.make_async_remote_copy(
                    src_ref=ins[t].at[s, 1 - c], dst_ref=outs[t].at[s], send_sem=send_sems.at[t, s],
                    recv_sem=recv_sems.at[t, s], device_id=(x, y, 1 - c), device_id_type=MESH)
                cp.start()
                copies.append(cp)
        for cp in copies:
            cp.wait()

    return pl.pallas_call(
        body, name=name, in_specs=[ANY] * n, out_specs=[ANY] * n,
        out_shape=[jax.ShapeDtypeStruct((N_CHIPS,) + g.shape[2:], g.dtype) for g in gs],
        scratch_shapes=[pltpu.SemaphoreType.DMA((n, N_CHIPS)), pltpu.SemaphoreType.DMA((n, N_CHIPS))],
    )(*gs)


def _exchange_chips(hs, name):
    n = len(hs)

    def body(*refs):
        ins, outs = refs[:n], refs[n:2 * n]
        send_sems, recv_sems, local_sems = refs[2 * n:]
        x, y, c, chips, chip_idx = _place()
        me = 2 * x + y
        started = []
        for t in range(n):
            lc = pltpu.make_async_copy(ins[t].at[me], outs[t].at[me], local_sems.at[t])
            lc.start()
            started.append(lc)
        sends = []
        for t in range(n):
            for j, chip in enumerate(chips):
                cp = pltpu.make_async_remote_copy(
                    src_ref=ins[t].at[chip_idx[j]], dst_ref=outs[t].at[me], send_sem=send_sems.at[t, j],
                    recv_sem=recv_sems.at[t, j], device_id=(*chip, c), device_id_type=MESH)
                cp.start()
                sends.append(cp)
        for t in range(n):
            for j, chip in enumerate(chips):
                slot = outs[t].at[chip_idx[j]]
                pltpu.make_async_remote_copy(
                    src_ref=slot, dst_ref=slot, send_sem=send_sems.at[t, j], recv_sem=recv_sems.at[t, j],
                    device_id=(*chip, c), device_id_type=MESH).wait_recv()
        for cp in sends:
            cp.wait_send()
        for lc in started:
            lc.wait()

    return pl.pallas_call(
        body, name=name, in_specs=[ANY] * n, out_specs=[ANY] * n,
        out_shape=[jax.ShapeDtypeStruct(h.shape, h.dtype) for h in hs],
        scratch_shapes=[pltpu.SemaphoreType.DMA((n, 3)), pltpu.SemaphoreType.DMA((n, 3)),
                        pltpu.SemaphoreType.DMA((n,))],
    )(*hs)


def _join_core_halves(rs, name):
    n = len(rs)

    def body(*refs):
        ins, outs = refs[:n], refs[n:2 * n]
        send_sems, recv_sems, local_sems = refs[2 * n:]
        x, y, c, _, _ = _place()
        started, sends = [], []
        for t in range(n):
            lc = pltpu.make_async_copy(ins[t], outs[t].at[c], local_sems.at[t])
            lc.start()
            started.append(lc)
            cp = pltpu.make_async_remote_copy(
                src_ref=ins[t], dst_ref=outs[t].at[c], send_sem=send_sems.at[t], recv_sem=recv_sems.at[t],
                device_id=(x, y, 1 - c), device_id_type=MESH)
            cp.start()
            sends.append(cp)
        for t in range(n):
            other = outs[t].at[1 - c]
            pltpu.make_async_remote_copy(
                src_ref=other, dst_ref=other, send_sem=send_sems.at[t], recv_sem=recv_sems.at[t],
                device_id=(x, y, 1 - c), device_id_type=MESH).wait_recv()
        for cp in sends:
            cp.wait_send()
        for lc in started:
            lc.wait()

    return pl.pallas_call(
        body, name=name, in_specs=[ANY] * n, out_specs=[ANY] * n,
        out_shape=[jax.ShapeDtypeStruct((2,) + r.shape, r.dtype) for r in rs],
        scratch_shapes=[pltpu.SemaphoreType.DMA((n,)), pltpu.SemaphoreType.DMA((n,)),
                        pltpu.SemaphoreType.DMA((n,))],
    )(*rs)


def _allreduce_small(vec, name):
    R = vec.shape[0]

    def body(v_ref, o_ref, buf, send_sems, recv_sems):
        x, y, c = lax.axis_index("x"), lax.axis_index("y"), lax.axis_index("c")
        me = 4 * x + 2 * y + c
        buf[me] = v_ref[...]
        peers = []
        for k in range(1, N_DEV):
            px = 1 - x if k & 4 else x
            py = 1 - y if k & 2 else y
            pc = 1 - c if k & 1 else c
            peers.append(((px, py, pc), 4 * px + 2 * py + pc))
        sends = []
        for k, (peer, _) in enumerate(peers):
            cp = pltpu.make_async_remote_copy(
                src_ref=v_ref, dst_ref=buf.at[me], send_sem=send_sems.at[k], recv_sem=recv_sems.at[k],
                device_id=peer, device_id_type=MESH)
            cp.start()
            sends.append(cp)
        for k, (peer, idx) in enumerate(peers):
            pltpu.make_async_remote_copy(
                src_ref=v_ref, dst_ref=buf.at[idx], send_sem=send_sems.at[k], recv_sem=recv_sems.at[k],
                device_id=peer, device_id_type=MESH).wait_recv()
        for cp in sends:
            cp.wait_send()
        total = buf[0]
        for d in range(1, N_DEV):
            total = total + buf[d]
        o_ref[...] = total

    vm = pl.BlockSpec(memory_space=pltpu.VMEM)
    return pl.pallas_call(
        body, name=name, in_specs=[vm], out_specs=vm, out_shape=jax.ShapeDtypeStruct((R, LANES), F32),
        scratch_shapes=[pltpu.VMEM((N_DEV, R, LANES), F32), pltpu.SemaphoreType.DMA((N_DEV - 1,)),
                        pltpu.SemaphoreType.DMA((N_DEV - 1,))],
    )(vec)


def _halves(a):
    lead = 1
    for d in a.shape[:-1]:
        lead *= d
    return a.reshape(2, lead // 2, a.shape[-1])


def _cols_from_shards(g):
    return g.transpose(1, 0, 2).reshape(g.shape[1], N_CHIPS * g.shape[2])


def _shards_from_cols(w):
    return w.reshape(w.shape[0], N_CHIPS, w.shape[1] // N_CHIPS).transpose(1, 0, 2)


def _pack(parts, rows):
    flat = jnp.concatenate([p.reshape(-1).astype(F32) for p in parts])
    return jnp.pad(flat, (0, rows * LANES - flat.shape[0])).reshape(rows, LANES)


def _unpack(packed, shapes):
    flat = packed.reshape(-1)
    out, off = [], 0
    for s in shapes:
        n = 1
        for d in s:
            n *= d
        out.append(flat[off:off + n].reshape(s))
        off += n
    return out


def _packed_rows(shapes):
    n = 0
    for s in shapes:
        k = 1
        for d in s:
            k *= d
        n += k
    return -(-n // (8 * LANES)) * 8


def _pad_lanes(a):
    return jnp.pad(a, ((0, 0), (0, LANES - a.shape[1])))


def kernel(x, mix_norm_g, ffn_norm_g, fox_w_in, fox_b_f, fox_q_norm_g, fox_k_norm_g, fox_w_out, pool_w, pool_b, pool_scale, ffn_w_gate_up, ffn_w_down, loss_target, m_mix_norm_g, m_ffn_norm_g, m_fox_w_in, m_fox_b_f, m_fox_q_norm_g, m_fox_k_norm_g, m_fox_w_out, m_pool_w, m_pool_b, m_pool_scale, m_ffn_w_gate_up, m_ffn_w_down, v_mix_norm_g, v_ffn_norm_g, v_fox_w_in, v_fox_b_f, v_fox_q_norm_g, v_fox_k_norm_g, v_fox_w_out, v_pool_w, v_pool_b, v_pool_scale, v_ffn_w_gate_up, v_ffn_w_down):
    _, S, D = x.shape
    H = D // HEAD_DIM
    depth = mix_norm_g.shape[0]
    n_pool = pool_w.shape[0]
    NG = len(POOL_WINDOWS)
    G = D // NG
    F = ffn_w_down.shape[1] * N_CHIPS
    ax, ay, ac = lax.axis_index("x"), lax.axis_index("y"), lax.axis_index("c")
    chip = 2 * ax + ay
    c_idx = jnp.reshape(ac, (1,)).astype(jnp.int32)
    xs = x[0]
    target = loss_target[0]

    Dq = D // N_CHIPS
    small_fwd_shapes = [(n_pool, D), (n_pool, D)]
    placed = []
    for p in (pool_b, pool_scale):
        full = lax.dynamic_update_slice(jnp.zeros((n_pool, D), F32), p, (0, chip * Dq))
        placed.append(jnp.where(ac == 0, full, jnp.zeros_like(full)))
    pool_b_full, pool_scale_full = _unpack(
        _allreduce_small(_pack(placed, _packed_rows(small_fwd_shapes)), "gather_pool_vectors"), small_fwd_shapes)

    weights = []
    for i in range(depth):
        j = i // 2
        mixer = ([fox_w_in[j], fox_w_out[j]] if i % 2 == 0 else [pool_w[j]])
        shards = [_halves(w.astype(BF16)) for w in mixer + [ffn_w_gate_up[i], ffn_w_down[i]]]
        got = _allgather_chips(shards, f"allgather_weights_{'fox' if i % 2 == 0 else 'pool'}")
        lw = {}
        if i % 2 == 0:
            w_in = _cols_from_shards(got[0].reshape(N_CHIPS, D, -1))
            lw["w_qkv"] = w_in[:, :3 * D]
            lw["w_f"] = _pad_lanes(w_in[:, 3 * D:])
            lw["w_out"] = got[1].reshape(D, D)
        else:
            lw["w_pool"] = got[0].reshape(N_CHIPS, NG, G // N_CHIPS, G).transpose(1, 0, 2, 3).reshape(NG, G, G)
        lw["w_gu"] = _cols_from_shards(got[-2].reshape(N_CHIPS, D, -1))
        lw["w_down"] = got[-1].reshape(F, D)
        weights.append(lw)

    saved = []
    cur = xs
    for i in range(depth):
        j = i // 2
        lw = weights[i]
        sv = {"x": cur}
        g_mix = mix_norm_g[i][None]
        if i % 2 == 0:
            h1b = _rmsnorm_fwd(cur, g_mix, BF16, "rmsnorm_fwd_bf16")
            proj = _matmul(h1b, lw["w_qkv"], "nn", F32, "qkv_proj", M=S, N=3 * D, K=D)
            fl = _matmul(h1b, lw["w_f"], "nn", F32, "forget_proj", M=S, N=LANES, K=D)
            bf = _pad_lanes(fox_b_f[j][None])
            qn, kn, vb = _qkv_post(proj, fox_q_norm_g[j][None], fox_k_norm_g[j][None], "qk_norm")
            cum = _fgate_fwd(fl, bf, "forget_cumsum")
            c_t = cum[:, :H].T
            c_col, c_row = c_t[:, :, None], c_t[:, None, :]
            o, lse = _flash_fwd(qn, kn, vb, c_col, c_row, "fox_attention_fwd")
            x1 = _matmul(o, lw["w_out"], "nn", F32, "attn_out_proj", M=S, N=D, K=D, res=cur)
            sv.update(h1b=h1b, proj=proj, fl=fl, bf=bf, qn=qn, kn=kn, vb=vb, c_col=c_col, c_row=c_row,
                      o=o, lse=lse)
        else:
            h1 = _rmsnorm_fwd(cur, g_mix, F32, "rmsnorm_fwd_f32")
            yb, x1 = _pool_fwd(h1, cur, lw["w_pool"], pool_b_full[j][None], pool_scale_full[j][None], "pool_fwd")
            sv.update(yb=yb)
        h2b = _rmsnorm_fwd(x1, ffn_norm_g[i][None], BF16, "rmsnorm_fwd_bf16")
        gu, act = _ffn_up(h2b, lw["w_gu"], "ffn_gate_up")
        x2 = _matmul(act, lw["w_down"], "nn", F32, "ffn_down", M=S, N=D, K=F, tk=512, res=x1)
        sv.update(x1=x1, h2b=h2b, gu=gu, act=act)
        saved.append(sv)
        cur = x2

    dcur, dcur_b, sq = _loss_grad(cur, target, "loss_grad")
    loss = lax.psum(sq[0, 0] * (0.5 / D), ("x", "y", "c"))

    g_mix_rows, g_ffn_rows = [None] * depth, [None] * depth
    g_bf, g_gq, g_gk = [None] * (depth - n_pool), [None] * (depth - n_pool), [None] * (depth - n_pool)
    g_pb, g_ps = [None] * n_pool, [None] * n_pool
    reduced = [None] * depth
    nkh = None
    for i in reversed(range(depth)):
        j = i // 2
        lw, sv = weights[i], saved[i]
        dgu = _ffn_act_bwd(dcur_b, lw["w_down"], sv["gu"], "ffn_act_bwd")
        d_w_down = _matmul(sv["act"], dcur_b, "tn", F32, "ffn_down_dw", M=F, N=D, K=S, tm=512, tk=512)
        dh2 = _matmul(
            dgu, lw["w_gu"], "nt", F32, "ffn_up_dx", M=S, N=D, K=2 * F, tk=512,
            a_spec=lambda tm, tn, tk: pl.BlockSpec(
                (None, tm, tk), lambda i_, j_, k_: (k_ // (F // tk), i_, k_ % (F // tk))))
        d_w_gu = _matmul(
            sv["h2b"], dgu, "tn", F32, "ffn_up_dw", M=D, N=2 * F, K=S, tn=512, tk=512,
            b_spec=lambda tm, tn, tk: pl.BlockSpec(
                (None, tk, tn), lambda i_, j_, k_: (j_ // (F // tn), k_, j_ % (F // tn))))
        dx1, dx1b, g_ffn_rows[i] = _rmsnorm_bwd(sv["x1"], ffn_norm_g[i][None], dh2, dcur, "rmsnorm_bwd")
        g_mix = mix_norm_g[i][None]
        if i % 2 == 0:
            do = _matmul(dx1b, lw["w_out"], "nt", F32, "attn_out_dx", M=S, N=D, K=D)
            d_w_out = _matmul(sv["o"], dx1b, "tn", F32, "attn_out_dw", M=D, N=D, K=S, tk=512)
            delta, dob = _attn_delta(do, sv["o"], "attn_delta")
            dqn, dkn, dvb, dcq, dck = _flash_bwd(sv["qn"], sv["kn"], sv["vb"], dob, sv["lse"], delta,
                                            sv["c_col"], sv["c_row"], "fox_attention_bwd")
            dproj, dgq, dgk = _qkv_post_bwd(sv["proj"], dqn, dkn, dvb, fox_q_norm_g[j][None],
                                            fox_k_norm_g[j][None], "qk_norm_bwd")
            dc = _pad_lanes((dcq[:, :, 0] - dck[:, 0, :]).T)
            dfl, dbf = _fgate_bwd(dc, sv["fl"], sv["bf"], "forget_cumsum_bwd")
            d_w_qkv = _matmul(sv["h1b"], dproj, "tn", F32, "qkv_dw", M=D, N=3 * D, K=S, tk=512)
            d_w_f = _matmul(sv["h1b"], dfl, "tn", F32, "forget_dw", M=D, N=LANES, K=S, tk=512)
            dh1f = _matmul(dfl, lw["w_f"], "nt", F32, "forget_dx", M=S, N=D, K=LANES)
            dh1 = _matmul(dproj, lw["w_qkv"], "nt", F32, "qkv_dx", M=S, N=D, K=3 * D, tk=512, res=dh1f)
            d_w_in = jnp.concatenate([d_w_qkv, d_w_f[:, :H]], axis=1)
            big = [_shards_from_cols(d_w_in), d_w_out.reshape(N_CHIPS, D // N_CHIPS, D)]
            g_bf[j], g_gq[j], g_gk[j] = dbf[0, :H], dgq[0], dgk[0]
        else:
            dyc, d_wp, dpb, dps = _pool_bwd_mix(dx1, sv["yb"], lw["w_pool"], pool_b_full[j][None],
                                                pool_scale_full[j][None], "pool_bwd_mix")
            dh1 = _pool_bwd_window(dyc, "pool_bwd_window")
            big = [d_wp.reshape(NG, N_CHIPS, G // N_CHIPS, G).transpose(1, 0, 2, 3)]
            g_pb[j], g_ps[j] = dpb[0], dps[0]
        big += [_shards_from_cols(d_w_gu), d_w_down.reshape(N_CHIPS, F // N_CHIPS, D)]
        dcur, dcur_b, g_mix_rows[i] = _rmsnorm_bwd(sv["x"], g_mix, dh1, dx1, "rmsnorm_bwd")

        kind = "fox" if i % 2 == 0 else "pool"
        gs = [b.reshape(N_CHIPS, 2, -1, b.shape[-1]) for b in big]
        r1 = _send_other_half_to_sibling(gs, f"grads_to_sibling_{kind}")
        hs = [_sum_core_halves(g, r, c_idx, "sum_core_halves") for g, r in zip(gs, r1)]
        r2 = _exchange_chips(hs, f"grads_to_chips_{kind}")
        rs = [_sum_chips(r, "sum_chips") for r in r2]
        reduced[i] = _join_core_halves(rs, f"grads_join_{kind}")

    small_shapes = [(depth, D), (depth, D), fox_b_f.shape, fox_q_norm_g.shape, fox_k_norm_g.shape,
                    (n_pool, D), (n_pool, D)]
    small = [jnp.stack(g_mix_rows)[:, 0], jnp.stack(g_ffn_rows)[:, 0], jnp.stack(g_bf), jnp.stack(g_gq),
             jnp.stack(g_gk), jnp.stack(g_pb), jnp.stack(g_ps)]
    (gr_mix, gr_ffn, gr_bf, gr_gq, gr_gk, gr_pb_full, gr_ps_full) = _unpack(
        _allreduce_small(_pack(small, _packed_rows(small_shapes)), "allreduce_small_grads"), small_shapes)
    gr_pb = lax.dynamic_slice(gr_pb_full, (0, chip * Dq), (n_pool, Dq))
    gr_ps = lax.dynamic_slice(gr_ps_full, (0, chip * Dq), (n_pool, Dq))

    fox_layers = [i for i in range(depth) if i % 2 == 0]
    pool_layers = [i for i in range(depth) if i % 2 == 1]
    gr_w_in = jnp.stack([reduced[i][0].reshape(fox_w_in.shape[1:]) for i in fox_layers])
    gr_w_out = jnp.stack([reduced[i][1].reshape(fox_w_out.shape[1:]) for i in fox_layers])
    gr_pool_w = jnp.stack([reduced[i][0].reshape(pool_w.shape[1:]) for i in pool_layers])
    gr_gu = jnp.stack([reduced[i][-2].reshape(ffn_w_gate_up.shape[1:]) for i in range(depth)])
    gr_down = jnp.stack([reduced[i][-1].reshape(ffn_w_down.shape[1:]) for i in range(depth)])

    def update_big(w, g, m, v, name):
        flat = lambda a: a.reshape(-1, a.shape[-1])
        return [o.reshape(w.shape) for o in _adamw(flat(w), flat(g), flat(m), flat(v), name)]

    names = ["mix_norm_g", "ffn_norm_g", "fox_w_in", "fox_b_f", "fox_q_norm_g", "fox_k_norm_g", "fox_w_out",
             "pool_w", "pool_b", "pool_scale", "ffn_w_gate_up", "ffn_w_down"]
    ws = dict(zip(names, [mix_norm_g, ffn_norm_g, fox_w_in, fox_b_f, fox_q_norm_g, fox_k_norm_g, fox_w_out,
                          pool_w, pool_b, pool_scale, ffn_w_gate_up, ffn_w_down]))
    ms = dict(zip(names, [m_mix_norm_g, m_ffn_norm_g, m_fox_w_in, m_fox_b_f, m_fox_q_norm_g, m_fox_k_norm_g,
                          m_fox_w_out, m_pool_w, m_pool_b, m_pool_scale, m_ffn_w_gate_up, m_ffn_w_down]))
    vs = dict(zip(names, [v_mix_norm_g, v_ffn_norm_g, v_fox_w_in, v_fox_b_f, v_fox_q_norm_g, v_fox_k_norm_g,
                          v_fox_w_out, v_pool_w, v_pool_b, v_pool_scale, v_ffn_w_gate_up, v_ffn_w_down]))
    grads = dict(mix_norm_g=gr_mix, ffn_norm_g=gr_ffn, fox_w_in=gr_w_in, fox_b_f=gr_bf, fox_q_norm_g=gr_gq,
                 fox_k_norm_g=gr_gk, fox_w_out=gr_w_out, pool_w=gr_pool_w, pool_b=gr_pb, pool_scale=gr_ps,
                 ffn_w_gate_up=gr_gu, ffn_w_down=gr_down)
    big_names = ["fox_w_in", "fox_w_out", "pool_w", "ffn_w_gate_up", "ffn_w_down"]
    small_names = [n for n in names if n not in big_names]
    delta, new_m, new_v = {}, {}, {}
    for n in big_names:
        delta[n], new_m[n], new_v[n] = update_big(ws[n], grads[n], ms[n], vs[n], "adamw_" + n)
    shapes = [ws[n].shape for n in small_names]
    rows = _packed_rows(shapes)
    packed = _adamw(_pack([ws[n] for n in small_names], rows), _pack([grads[n] for n in small_names], rows),
                    _pack([ms[n] for n in small_names], rows), _pack([vs[n] for n in small_names], rows),
                    "adamw_small")
    for dst, pk in zip((delta, new_m, new_v), packed):
        for n, a in zip(small_names, _unpack(pk, shapes)):
            dst[n] = a

    return (loss, dcur[None], *[grads[n] for n in names], *[delta[n] for n in names],
            *[new_m[n] for n in names], *[new_v[n] for n in names])
```

```python
import functools

import numpy as np
import jax
import jax.numpy as jnp
from jax import lax
from jax.experimental import pallas as pl
from jax.experimental.pallas import tpu as pltpu

F32 = jnp.float32
BF16 = jnp.bfloat16
MESH = pl.DeviceIdType.MESH

HEAD_DIM = 128
RMS_EPS = 1e-6
NEG_INF = -1e30
POOL_WINDOWS = (2, 4, 8, 16)
POOL_HALO = 16
LANES = 128
N_CHIPS = 4
N_DEV = 8
VMEM_LIMIT = 56 * 1024 * 1024

ADAM_LR = 0.001
ADAM_B1 = 0.9
ADAM_B2 = 0.999
ADAM_EPS = 1e-08
ADAM_WD = 0.01
ADAM_STEP = 10

NN = (((1,), (0,)), ((), ()))
NT = (((1,), (1,)), ((), ()))
TN = (((0,), (0,)), ((), ()))


def _params(*sem):
    return pltpu.CompilerParams(dimension_semantics=sem if sem else None, vmem_limit_bytes=VMEM_LIMIT)


def _tile(n, pref, unit=LANES):
    best = None
    t = unit
    while t <= min(n, pref):
        if n % t == 0:
            best = t
        t += unit
    return best if best is not None else n


def _row_tile(rows, cols, itemsize, budget):
    best = None
    for t in range(16, rows + 1, 16):
        if rows % t == 0 and t * cols * itemsize <= budget:
            best = t
    return best if best is not None else rows


def _dot(a, b, dims):
    return lax.dot_general(a, b, dims, preferred_element_type=F32)


def _matmul(a, b, mode, out_dtype, name, *, M, N, K, tm=1024, tn=1024, tk=2048, res=None,
            a_spec=None, b_spec=None):
    tm, tn, tk = _tile(M, tm), _tile(N, tn), _tile(K, tk)
    nk = K // tk
    dims = {"nn": NN, "nt": NT, "tn": TN}[mode]
    if a_spec is None:
        a_spec = (pl.BlockSpec((tk, tm), lambda i, j, k: (k, i)) if mode == "tn"
                  else pl.BlockSpec((tm, tk), lambda i, j, k: (i, k)))
    else:
        a_spec = a_spec(tm, tn, tk)
    if b_spec is None:
        b_spec = (pl.BlockSpec((tn, tk), lambda i, j, k: (j, k)) if mode == "nt"
                  else pl.BlockSpec((tk, tn), lambda i, j, k: (k, j)))
    else:
        b_spec = b_spec(tm, tn, tk)
    o_spec = pl.BlockSpec((tm, tn), lambda i, j, k: (i, j))
    has_res = res is not None

    def body(a_ref, b_ref, *rest):
        if has_res:
            r_ref, o_ref, acc = rest
        else:
            o_ref, acc = rest

        def finish(total):
            if has_res:
                total = total + r_ref[...]
            o_ref[...] = total.astype(o_ref.dtype)

        part = _dot(a_ref[...], b_ref[...], dims)
        if nk == 1:
            finish(part)
        else:
            k = pl.program_id(2)

            @pl.when(k == 0)
            def _():
                acc[...] = part

            @pl.when(k > 0)
            def _():
                acc[...] += part

            @pl.when(k == nk - 1)
            def _():
                finish(acc[...])

    operands = [a, b] + ([res] if has_res else [])
    in_specs = [a_spec, b_spec] + ([o_spec] if has_res else [])
    return pl.pallas_call(
        body, name=name, grid=(M // tm, N // tn, nk), in_specs=in_specs, out_specs=o_spec,
        out_shape=jax.ShapeDtypeStruct((M, N), out_dtype),
        scratch_shapes=[pltpu.VMEM((tm, tn) if nk > 1 else (8, LANES), F32)],
        compiler_params=_params("parallel", "parallel", "arbitrary"),
    )(*operands)


def _rmsnorm_fwd(x, g, out_dtype, name):
    S, D = x.shape
    tm = _tile(S, 512, 16)

    def body(x_ref, g_ref, o_ref):
        xv = x_ref[...]
        r = lax.rsqrt(jnp.mean(xv * xv, axis=-1, keepdims=True) + RMS_EPS)
        o_ref[...] = ((xv * r) * g_ref[...]).astype(o_ref.dtype)

    row = pl.BlockSpec((tm, D), lambda i: (i, 0))
    return pl.pallas_call(
        body, name=name, grid=(S // tm,), in_specs=[row, pl.BlockSpec((1, D), lambda i: (0, 0))],
        out_specs=row, out_shape=jax.ShapeDtypeStruct((S, D), out_dtype),
        compiler_params=_params("parallel"),
    )(x, g)


def _rmsnorm_bwd(x, g, dh, dres, name):
    S, D = x.shape
    tm = _tile(S, 256, 16)

    def body(x_ref, g_ref, dh_ref, dres_ref, dx_ref, dxb_ref, dg_ref):
        xv = x_ref[...]
        r = lax.rsqrt(jnp.mean(xv * xv, axis=-1, keepdims=True) + RMS_EPS)
        xhat = xv * r
        dhv = dh_ref[...]
        dxhat = dhv * g_ref[...]
        dx = dres_ref[...] + r * (dxhat - xhat * jnp.mean(dxhat * xhat, axis=-1, keepdims=True))
        dx_ref[...] = dx
        dxb_ref[...] = dx.astype(BF16)

        @pl.when(pl.program_id(0) == 0)
        def _():
            dg_ref[...] = jnp.zeros_like(dg_ref)

        dg_ref[...] += jnp.sum(dhv * xhat, axis=0, keepdims=True)

    row = pl.BlockSpec((tm, D), lambda i: (i, 0))
    vec = pl.BlockSpec((1, D), lambda i: (0, 0))
    return pl.pallas_call(
        body, name=name, grid=(S // tm,), in_specs=[row, vec, row, row], out_specs=[row, row, vec],
        out_shape=[jax.ShapeDtypeStruct((S, D), F32), jax.ShapeDtypeStruct((S, D), BF16),
                   jax.ShapeDtypeStruct((1, D), F32)],
        compiler_params=_params("arbitrary"),
    )(x, g, dh, dres)


def _loss_grad(y, target, name):
    S, D = y.shape
    tm = _tile(S, 256, 16)

    def body(y_ref, t_ref, dy_ref, dyb_ref, sq_ref):
        e = y_ref[...] - t_ref[...]
        dy = e / D
        dy_ref[...] = dy
        dyb_ref[...] = dy.astype(BF16)

        @pl.when(pl.program_id(0) == 0)
        def _():
            sq_ref[...] = jnp.zeros_like(sq_ref)

        total = jnp.sum(jnp.sum(e * e, axis=1, keepdims=True), axis=0, keepdims=True)
        sq_ref[...] += jnp.broadcast_to(total, sq_ref.shape)

    row = pl.BlockSpec((tm, D), lambda i: (i, 0))
    vec = pl.BlockSpec((1, LANES), lambda i: (0, 0))
    return pl.pallas_call(
        body, name=name, grid=(S // tm,), in_specs=[row, row], out_specs=[row, row, vec],
        out_shape=[jax.ShapeDtypeStruct((S, D), F32), jax.ShapeDtypeStruct((S, D), BF16),
                   jax.ShapeDtypeStruct((1, LANES), F32)],
        compiler_params=_params("arbitrary"),
    )(y, target)


def _ffn_up(hb, w_gu, name):
    S, D = hb.shape
    F = w_gu.shape[1] // 2
    tm, tn, tk = _tile(S, 1024), _tile(F, 512), _tile(D, 2048)
    nk, nj = D // tk, F // tn

    def body(a_ref, wg_ref, wu_ref, gu_ref, act_ref, accg, accu):
        pg = _dot(a_ref[...], wg_ref[...], NN)
        pu = _dot(a_ref[...], wu_ref[...], NN)

        def finish(g, u):
            gu_ref[0] = g
            gu_ref[1] = u
            act_ref[...] = ((g / (1.0 + jnp.exp(-g))) * u).astype(BF16)

        if nk == 1:
            finish(pg, pu)
        else:
            k = pl.program_id(2)

            @pl.when(k == 0)
            def _():
                accg[...] = pg
                accu[...] = pu

            @pl.when(k > 0)
            def _():
                accg[...] += pg
                accu[...] += pu

            @pl.when(k == nk - 1)
            def _():
                finish(accg[...], accu[...])

    acc_shape = (tm, tn) if nk > 1 else (8, LANES)
    return pl.pallas_call(
        body, name=name, grid=(S // tm, nj, nk),
        in_specs=[pl.BlockSpec((tm, tk), lambda i, j, k: (i, k)),
                  pl.BlockSpec((tk, tn), lambda i, j, k: (k, j)),
                  pl.BlockSpec((tk, tn), lambda i, j, k: (k, j + nj))],
        out_specs=[pl.BlockSpec((2, tm, tn), lambda i, j, k: (0, i, j)),
                   pl.BlockSpec((tm, tn), lambda i, j, k: (i, j))],
        out_shape=[jax.ShapeDtypeStruct((2, S, F), F32), jax.ShapeDtypeStruct((S, F), BF16)],
        scratch_shapes=[pltpu.VMEM(acc_shape, F32), pltpu.VMEM(acc_shape, F32)],
        compiler_params=_params("parallel", "parallel", "arbitrary"),
    )(hb, w_gu, w_gu)


def _ffn_act_bwd(dyb, w_down, gu, name):
    S, D = dyb.shape
    F = w_down.shape[0]
    tm, tn, tk = _tile(S, 1024), _tile(F, 512), _tile(D, 2048)
    nk = D // tk

    def body(a_ref, w_ref, gu_ref, dgu_ref, acc):
        part = _dot(a_ref[...], w_ref[...], NT)

        def finish(dact):
            g = gu_ref[0]
            u = gu_ref[1]
            sg = 1.0 / (1.0 + jnp.exp(-g))
            dgu_ref[0] = (dact * u * (sg * (1.0 + g * (1.0 - sg)))).astype(BF16)
            dgu_ref[1] = (dact * (g * sg)).astype(BF16)

        if nk == 1:
            finish(part)
        else:
            k = pl.program_id(2)

            @pl.when(k == 0)
            def _():
                acc[...] = part

            @pl.when(k > 0)
            def _():
                acc[...] += part

            @pl.when(k == nk - 1)
            def _():
                finish(acc[...])

    gu_spec = pl.BlockSpec((2, tm, tn), lambda i, j, k: (0, i, j))
    return pl.pallas_call(
        body, name=name, grid=(S // tm, F // tn, nk),
        in_specs=[pl.BlockSpec((tm, tk), lambda i, j, k: (i, k)),
                  pl.BlockSpec((tn, tk), lambda i, j, k: (j, k)), gu_spec],
        out_specs=gu_spec, out_shape=jax.ShapeDtypeStruct((2, S, F), BF16),
        scratch_shapes=[pltpu.VMEM((tm, tn) if nk > 1 else (8, LANES), F32)],
        compiler_params=_params("parallel", "parallel", "arbitrary"),
    )(dyb, w_down, gu)


LOG2E = 1.4426950408889634
Q_PRESCALE = HEAD_DIM ** -0.5 * LOG2E
AUX_A = 0
AUX_B = 3


def _aux_block(lane, minus=None, minus_at=None, ones_at=None):
    out = jnp.zeros(lane.shape, BF16)
    if minus is not None:
        x = -minus
        hi = x.astype(BF16)
        r1 = x - hi.astype(F32)
        mid = r1.astype(BF16)
        lo = (r1 - mid.astype(F32)).astype(BF16)
        for n, piece in enumerate((hi, mid, lo)):
            out = jnp.where(lane == minus_at + n, piece, out)
    if ones_at is not None:
        out = jnp.where((lane >= ones_at) & (lane < ones_at + 3), jnp.ones(lane.shape, BF16), out)
    return out


def _qkv_post(proj, gq, gk, cum, name):
    S, D3 = proj.shape
    D = D3 // 3
    tm = _tile(S, 256, 16)

    def body(q_ref, k_ref, v_ref, gq_ref, gk_ref, c_ref, qn_ref, kn_ref, vb_ref, ak_ref):
        lane = lax.broadcasted_iota(jnp.int32, (tm, HEAD_DIM), 1)
        for src, g_ref, dst, mult in ((q_ref, gq_ref, qn_ref, Q_PRESCALE), (k_ref, gk_ref, kn_ref, None)):
            for h in range(D // HEAD_DIM):
                cols = slice(h * HEAD_DIM, (h + 1) * HEAD_DIM)
                t = src[:, cols]
                r = lax.rsqrt(jnp.mean(t * t, axis=-1, keepdims=True) + RMS_EPS)
                n = (t * r) * g_ref[...]
                dst[:, cols] = (n if mult is None else n * mult).astype(BF16)
        for h in range(D // HEAD_DIM):
            ch = jnp.broadcast_to(c_ref[:, h:h + 1] * LOG2E, (tm, HEAD_DIM))
            ak_ref[:, h * HEAD_DIM:(h + 1) * HEAD_DIM] = _aux_block(lane, ch, AUX_A, AUX_B)
        vb_ref[...] = v_ref[...].astype(BF16)

    part = lambda n: pl.BlockSpec((tm, D), lambda i: (i, n))
    vec = pl.BlockSpec((1, HEAD_DIM), lambda i: (0, 0))
    row = pl.BlockSpec((tm, D), lambda i: (i, 0))
    return pl.pallas_call(
        body, name=name, grid=(S // tm,),
        in_specs=[part(0), part(1), part(2), vec, vec, pl.BlockSpec((tm, LANES), lambda i: (i, 0))],
        out_specs=[row, row, row, row], out_shape=[jax.ShapeDtypeStruct((S, D), BF16)] * 4,
        compiler_params=_params("parallel"),
    )(proj, proj, proj, gq, gk, cum)


def _qkv_post_bwd(proj, dqx, dkx, dvb, gq, gk, name):
    S, D = dvb.shape
    tm = _tile(S, 128, 16)

    def body(q_ref, k_ref, dqx_ref, dkx_ref, dvb_ref, gq_ref, gk_ref, dp_ref, dgq_ref, dgk_ref):
        @pl.when(pl.program_id(0) == 0)
        def _():
            dgq_ref[...] = jnp.zeros_like(dgq_ref)
            dgk_ref[...] = jnp.zeros_like(dgk_ref)

        for n, (src, dsrc, g_ref, dg_ref, mult) in enumerate(
                ((q_ref, dqx_ref, gq_ref, dgq_ref, HEAD_DIM ** -0.5), (k_ref, dkx_ref, gk_ref, dgk_ref, 1.0 / LOG2E))):
            dg = jnp.zeros((1, HEAD_DIM), F32)
            for h in range(D // HEAD_DIM):
                cols = slice(h * HEAD_DIM, (h + 1) * HEAD_DIM)
                t = src[:, cols]
                r = lax.rsqrt(jnp.mean(t * t, axis=-1, keepdims=True) + RMS_EPS)
                that = t * r
                dn = dsrc[:, 2 * h * HEAD_DIM:(2 * h + 1) * HEAD_DIM] * mult
                dhat = dn * g_ref[...]
                dt = r * (dhat - that * jnp.mean(dhat * that, axis=-1, keepdims=True))
                dp_ref[:, n * D + h * HEAD_DIM:n * D + (h + 1) * HEAD_DIM] = dt.astype(BF16)
                dg = dg + jnp.sum(dn * that, axis=0, keepdims=True)
            dg_ref[...] += dg
        dp_ref[:, 2 * D:3 * D] = dvb_ref[...]

    part = lambda n: pl.BlockSpec((tm, D), lambda i: (i, n))
    row = pl.BlockSpec((tm, D), lambda i: (i, 0))
    vec = pl.BlockSpec((1, HEAD_DIM), lambda i: (0, 0))
    wide = pl.BlockSpec((tm, 2 * D), lambda i: (i, 0))
    return pl.pallas_call(
        body, name=name, grid=(S // tm,), in_specs=[part(0), part(1), wide, wide, row, vec, vec],
        out_specs=[pl.BlockSpec((tm, 3 * D), lambda i: (i, 0)), vec, vec],
        out_shape=[jax.ShapeDtypeStruct((S, 3 * D), BF16), jax.ShapeDtypeStruct((1, HEAD_DIM), F32),
                   jax.ShapeDtypeStruct((1, HEAD_DIM), F32)],
        compiler_params=_params("arbitrary"),
    )(proj, proj, dqx, dkx, dvb, gq, gk)


def _log_sigmoid(z):
    return -(jnp.maximum(-z, 0.0) + jnp.log(1.0 + jnp.exp(-jnp.abs(z))))


def _fgate_fwd(fl, bf, name):
    S = fl.shape[0]
    T = _tile(S, 256, 16)

    def body(fl_ref, bf_ref, c_ref, carry):
        @pl.when(pl.program_id(0) == 0)
        def _():
            carry[...] = jnp.zeros_like(carry)

        logf = _log_sigmoid(fl_ref[...] + bf_ref[...])
        r = lax.broadcasted_iota(jnp.int32, (T, T), 0)
        c = lax.broadcasted_iota(jnp.int32, (T, T), 1)
        tri = (r >= c).astype(F32)
        cum = jnp.dot(tri, logf, precision=lax.Precision.HIGHEST, preferred_element_type=F32) + carry[...]
        c_ref[...] = cum
        carry[...] = cum[T - 1:T, :]

    row = pl.BlockSpec((T, LANES), lambda i: (i, 0))
    return pl.pallas_call(
        body, name=name, grid=(S // T,), in_specs=[row, pl.BlockSpec((1, LANES), lambda i: (0, 0))],
        out_specs=row, out_shape=jax.ShapeDtypeStruct((S, LANES), F32),
        scratch_shapes=[pltpu.VMEM((1, LANES), F32)], compiler_params=_params("arbitrary"),
    )(fl, bf)


def _fgate_bwd(dc, fl, bf, name):
    S = fl.shape[0]
    T = _tile(S, 256, 16)
    nb = S // T

    def body(dc_ref, fl_ref, bf_ref, dfl_ref, dbf_ref, carry):
        @pl.when(pl.program_id(0) == 0)
        def _():
            carry[...] = jnp.zeros_like(carry)
            dbf_ref[...] = jnp.zeros_like(dbf_ref)

        r = lax.broadcasted_iota(jnp.int32, (T, T), 0)
        c = lax.broadcasted_iota(jnp.int32, (T, T), 1)
        triu = (c >= r).astype(F32)
        dlogf = jnp.dot(triu, dc_ref[...], precision=lax.Precision.HIGHEST,
                        preferred_element_type=F32) + carry[...]
        carry[...] = dlogf[0:1, :]
        z = fl_ref[...] + bf_ref[...]
        dz = dlogf * (1.0 / (1.0 + jnp.exp(z)))
        dfl_ref[...] = dz.astype(BF16)
        dbf_ref[...] += jnp.sum(dz, axis=0, keepdims=True)

    row = pl.BlockSpec((T, LANES), lambda i: (nb - 1 - i, 0))
    vec = pl.BlockSpec((1, LANES), lambda i: (0, 0))
    return pl.pallas_call(
        body, name=name, grid=(nb,), in_specs=[row, row, vec], out_specs=[row, vec],
        out_shape=[jax.ShapeDtypeStruct((S, LANES), BF16), jax.ShapeDtypeStruct((1, LANES), F32)],
        scratch_shapes=[pltpu.VMEM((1, LANES), F32)], compiler_params=_params("arbitrary"),
    )(dc, fl, bf)


def _attn_logits(q_ref, aq, k_ref, ak, masked, T):
    qf = jnp.concatenate([q_ref[...], aq], axis=1)
    kf = jnp.concatenate([k_ref[...], ak], axis=1)
    s = _dot(qf, kf, NT)
    if masked:
        r = lax.broadcasted_iota(jnp.int32, (T, T), 0)
        c = lax.broadcasted_iota(jnp.int32, (T, T), 1)
        s = jnp.where(r >= c, s, NEG_INF)
    return s, qf, kf


def _causal_pairs(nb, q_major):
    pairs = ([(i, j) for i in range(nb) for j in range(i + 1)] if q_major
             else [(i, j) for j in range(nb) for i in range(j, nb)])
    return (jnp.asarray(np.array([p[0] for p in pairs], np.int32)),
            jnp.asarray(np.array([p[1] for p in pairs], np.int32)))


def _flash_fwd(q2, kn, vb, aux_k, name):
    S, D = q2.shape
    H = D // HEAD_DIM
    T = _tile(S, 512)
    nb = S // T
    q_idx, k_idx = _causal_pairs(nb, True)

    def body(qi_ref, kj_ref, q_ref, k_ref, v_ref, ak_ref, o_ref, aq_ref, m_s, acc_s):
        t = pl.program_id(1)
        i = qi_ref[t]
        j = kj_ref[t]
        lane = lax.broadcasted_iota(jnp.int32, (T, HEAD_DIM), 1)

        @pl.when(j == 0)
        def _():
            m_s[...] = jnp.full_like(m_s, NEG_INF)
            acc_s[...] = jnp.zeros_like(acc_s)

        def step(masked):
            s, _, _ = _attn_logits(q_ref, _aux_block(lane, ones_at=AUX_A), k_ref, ak_ref[...], masked, T)
            m_prev = m_s[...]
            m_new = jnp.maximum(m_prev, jnp.max(s, axis=1, keepdims=True))
            p = jnp.exp2(s - jnp.tile(m_new, (1, T // HEAD_DIM)))
            alpha = jnp.exp2(m_prev - m_new)
            vf = jnp.concatenate([v_ref[...], jnp.ones((T, HEAD_DIM), BF16)], axis=1)
            acc_s[...] = jnp.tile(alpha, (1, 2)) * acc_s[...] + _dot(p.astype(BF16), vf, NN)
            m_s[...] = m_new

        @pl.when(j < i)
        def _():
            step(False)

        @pl.when(j == i)
        def _():
            step(True)
            l = acc_s[:, HEAD_DIM:]
            o_ref[...] = (acc_s[:, :HEAD_DIM] / l).astype(BF16)
            aq_ref[...] = _aux_block(lane, m_s[...] + jnp.log2(l), AUX_B, AUX_A)

    qspec = pl.BlockSpec((T, HEAD_DIM), lambda h, t, qi, kj: (qi[t], h))
    kspec = pl.BlockSpec((T, HEAD_DIM), lambda h, t, qi, kj: (kj[t], h))
    grid_spec = pltpu.PrefetchScalarGridSpec(
        num_scalar_prefetch=2, grid=(H, q_idx.shape[0]), in_specs=[qspec, kspec, kspec, kspec],
        out_specs=[qspec, qspec],
        scratch_shapes=[pltpu.VMEM((T, HEAD_DIM), F32), pltpu.VMEM((T, 2 * HEAD_DIM), F32)])
    return pl.pallas_call(
        body, name=name, grid_spec=grid_spec,
        out_shape=[jax.ShapeDtypeStruct((S, D), BF16), jax.ShapeDtypeStruct((S, D), BF16)],
        compiler_params=_params("parallel", "arbitrary"),
    )(q_idx, k_idx, q2, kn, vb, aux_k)


def _attn_delta(do, o, name):
    S, D = do.shape
    H = D // HEAD_DIM
    tm = _tile(S, 256, 16)

    def body(do_ref, o_ref, ad_ref, dob_ref):
        lane = lax.broadcasted_iota(jnp.int32, (tm, HEAD_DIM), 1)
        for h in range(H):
            cols = slice(h * HEAD_DIM, (h + 1) * HEAD_DIM)
            delta = jnp.sum(do_ref[:, cols] * o_ref[:, cols].astype(F32), axis=1, keepdims=True)
            ad_ref[:, cols] = _aux_block(lane, jnp.broadcast_to(delta, (tm, HEAD_DIM)), AUX_A)
        dob_ref[...] = do_ref[...].astype(BF16)

    row = pl.BlockSpec((tm, D), lambda i: (i, 0))
    return pl.pallas_call(
        body, name=name, grid=(S // tm,), in_specs=[row, row], out_specs=[row, row],
        out_shape=[jax.ShapeDtypeStruct((S, D), BF16), jax.ShapeDtypeStruct((S, D), BF16)],
        compiler_params=_params("parallel"),
    )(do, o)


def _flash_bwd(q2, kn, vb, dob, aux_q, aux_k, aux_do, name):
    S, D = q2.shape
    H = D // HEAD_DIM
    T = _tile(S, 512)
    nb = S // T
    q_idx, k_idx = _causal_pairs(nb, False)

    def body(qi_ref, kj_ref, q_ref, k_ref, v_ref, do_ref, aq_ref, ak_ref, ad_ref,
             dqx_ref, dkx_ref, dv_ref, dv_acc):
        t = pl.program_id(1)
        i = qi_ref[t]
        j = kj_ref[t]
        lane = lax.broadcasted_iota(jnp.int32, (T, HEAD_DIM), 1)

        @pl.when(t == 0)
        def _():
            dqx_ref[...] = jnp.zeros_like(dqx_ref)

        @pl.when(i == j)
        def _():
            dkx_ref[...] = jnp.zeros_like(dkx_ref)
            dv_acc[...] = jnp.zeros_like(dv_acc)

        def step(masked):
            s, qf, kf = _attn_logits(q_ref, aq_ref[...], k_ref, ak_ref[...], masked, T)
            p = jnp.exp2(s)
            dof = jnp.concatenate([do_ref[...], ad_ref[...]], axis=1)
            vf = jnp.concatenate([v_ref[...], _aux_block(lane, ones_at=AUX_A)], axis=1)
            ds = p * _dot(dof, vf, NT)
            dsb = ds.astype(BF16)
            dv_acc[...] += _dot(p.astype(BF16), do_ref[...], TN)
            dkx_ref[...] += _dot(dsb, qf, TN)
            rows = pl.ds(pl.multiple_of(i * T, T), T)
            dqx_ref[rows, :] += _dot(dsb, kf, NN)

        @pl.when(i > j)
        def _():
            step(False)

        @pl.when(i == j)
        def _():
            step(True)

        @pl.when(i == nb - 1)
        def _():
            dv_ref[...] = dv_acc[...].astype(BF16)

    qspec = pl.BlockSpec((T, HEAD_DIM), lambda h, t, qi, kj: (qi[t], h))
    kspec = pl.BlockSpec((T, HEAD_DIM), lambda h, t, qi, kj: (kj[t], h))
    grid_spec = pltpu.PrefetchScalarGridSpec(
        num_scalar_prefetch=2, grid=(H, q_idx.shape[0]),
        in_specs=[qspec, kspec, kspec, qspec, qspec, kspec, qspec],
        out_specs=[pl.BlockSpec((S, 2 * HEAD_DIM), lambda h, t, qi, kj: (0, h)),
                   pl.BlockSpec((T, 2 * HEAD_DIM), lambda h, t, qi, kj: (kj[t], h)), kspec],
        scratch_shapes=[pltpu.VMEM((T, HEAD_DIM), F32)])
    return pl.pallas_call(
        body, name=name, grid_spec=grid_spec,
        out_shape=[jax.ShapeDtypeStruct((S, 2 * D), F32), jax.ShapeDtypeStruct((S, 2 * D), F32),
                   jax.ShapeDtypeStruct((S, D), BF16)],
        compiler_params=_params("parallel", "arbitrary"),
    )(q_idx, k_idx, q2, kn, vb, dob, aux_q, aux_k, aux_do)


def _pool_counts(first_row, tm, win):
    t = first_row + lax.broadcasted_iota(jnp.int32, (tm, 1), 0)
    return jnp.minimum(t + 1, win).astype(F32)


def _pool_fwd(h, x, wp, b, scale, name):
    S, D = h.shape
    G = D // len(POOL_WINDOWS)
    tm = _tile(S, 256, 16)

    def body(h_ref, halo_ref, x_ref, w_ref, b_ref, s_ref, y_ref, o_ref, ext):
        i = pl.program_id(0)
        ext[POOL_HALO:, :] = h_ref[...]

        @pl.when(i == 0)
        def _():
            ext[0:POOL_HALO, :] = jnp.zeros((POOL_HALO, D), F32)

        @pl.when(i > 0)
        def _():
            ext[0:POOL_HALO, :] = halo_ref[...]

        for g, win in enumerate(POOL_WINDOWS):
            cols = slice(g * G, (g + 1) * G)
            tot = ext[POOL_HALO:POOL_HALO + tm, cols]
            for k in range(1, win):
                tot = tot + ext[POOL_HALO - k:POOL_HALO - k + tm, cols]
            y = (tot / _pool_counts(i * tm, tm, win) - h_ref[:, cols]).astype(BF16)
            y_ref[:, cols] = y
            z = _dot(y, w_ref[g], NN)
            o_ref[:, cols] = x_ref[:, cols] + (z + b_ref[:, cols]) * s_ref[:, cols]

    row = pl.BlockSpec((tm, D), lambda i: (i, 0))
    vec = pl.BlockSpec((1, D), lambda i: (0, 0))
    halo = pl.BlockSpec((POOL_HALO, D), lambda i: (jnp.maximum(i * (tm // POOL_HALO) - 1, 0), 0))
    return pl.pallas_call(
        body, name=name, grid=(S // tm,),
        in_specs=[row, halo, row, pl.BlockSpec((len(POOL_WINDOWS), G, G), lambda i: (0, 0, 0)), vec, vec],
        out_specs=[row, row],
        out_shape=[jax.ShapeDtypeStruct((S, D), BF16), jax.ShapeDtypeStruct((S, D), F32)],
        scratch_shapes=[pltpu.VMEM((tm + POOL_HALO, D), F32)], compiler_params=_params("parallel"),
    )(h, h, x, wp, b, scale)


def _pool_bwd_mix(dout, yb, wp, b, scale, name):
    S, D = dout.shape
    NG = len(POOL_WINDOWS)
    G = D // NG
    tm = _tile(S, 256, 16)

    def body(do_ref, y_ref, w_ref, b_ref, s_ref, dyc_ref, dw_ref, db_ref, ds_ref):
        i = pl.program_id(0)

        @pl.when(i == 0)
        def _():
            dw_ref[...] = jnp.zeros_like(dw_ref)
            db_ref[...] = jnp.zeros_like(db_ref)
            ds_ref[...] = jnp.zeros_like(ds_ref)

        for g, win in enumerate(POOL_WINDOWS):
            cols = slice(g * G, (g + 1) * G)
            y = y_ref[:, cols]
            dz = do_ref[:, cols]
            zb = _dot(y, w_ref[g], NN) + b_ref[:, cols]
            ds_ref[:, cols] += jnp.sum(dz * zb, axis=0, keepdims=True)
            dzb = dz * s_ref[:, cols]
            db_ref[:, cols] += jnp.sum(dzb, axis=0, keepdims=True)
            dzb16 = dzb.astype(BF16)
            dw_ref[g] += _dot(y, dzb16, TN)
            dy = _dot(dzb16, w_ref[g], NT)
            dyc_ref[:, cols] = dy / _pool_counts(i * tm, tm, win)

    row = pl.BlockSpec((tm, D), lambda i: (i, 0))
    vec = pl.BlockSpec((1, D), lambda i: (0, 0))
    wspec = pl.BlockSpec((NG, G, G), lambda i: (0, 0, 0))
    return pl.pallas_call(
        body, name=name, grid=(S // tm,), in_specs=[row, row, wspec, vec, vec],
        out_specs=[row, wspec, vec, vec],
        out_shape=[jax.ShapeDtypeStruct((S, D), F32), jax.ShapeDtypeStruct((NG, G, G), F32),
                   jax.ShapeDtypeStruct((1, D), F32), jax.ShapeDtypeStruct((1, D), F32)],
        compiler_params=_params("arbitrary"),
    )(dout, yb, wp, b, scale)


def _pool_bwd_window(dyc, name):
    S, D = dyc.shape
    G = D // len(POOL_WINDOWS)
    tm = _tile(S, 256, 16)
    nb = S // tm

    def body(d_ref, halo_ref, dh_ref, ext):
        i = pl.program_id(0)
        ext[0:tm, :] = d_ref[...]

        @pl.when(i == nb - 1)
        def _():
            ext[tm:tm + POOL_HALO, :] = jnp.zeros((POOL_HALO, D), F32)

        @pl.when(i < nb - 1)
        def _():
            ext[tm:tm + POOL_HALO, :] = halo_ref[...]

        for g, win in enumerate(POOL_WINDOWS):
            cols = slice(g * G, (g + 1) * G)
            tot = ext[0:tm, cols] * (1.0 - _pool_counts(i * tm, tm, win))
            for k in range(1, win):
                tot = tot + ext[k:k + tm, cols]
            dh_ref[:, cols] = tot

    row = pl.BlockSpec((tm, D), lambda i: (i, 0))
    halo = pl.BlockSpec((POOL_HALO, D),
                        lambda i: (jnp.minimum((i + 1) * (tm // POOL_HALO), S // POOL_HALO - 1), 0))
    return pl.pallas_call(
        body, name=name, grid=(nb,), in_specs=[row, halo], out_specs=row,
        out_shape=jax.ShapeDtypeStruct((S, D), F32),
        scratch_shapes=[pltpu.VMEM((tm + POOL_HALO, D), F32)], compiler_params=_params("parallel"),
    )(dyc, dyc)


def _adamw(w, g, m, v, name):
    R, C = w.shape
    tr = _row_tile(R, C, 4, 1 << 20)

    def body(w_ref, g_ref, m_ref, v_ref, d_ref, nm_ref, nv_ref):
        gv = g_ref[...]
        m_new = ADAM_B1 * m_ref[...] + (1.0 - ADAM_B1) * gv
        v_new = ADAM_B2 * v_ref[...] + (1.0 - ADAM_B2) * (gv * gv)
        m_hat = m_new / (1.0 - ADAM_B1 ** ADAM_STEP)
        v_hat = v_new / (1.0 - ADAM_B2 ** ADAM_STEP)
        d_ref[...] = -ADAM_LR * (m_hat / (jnp.sqrt(v_hat) + ADAM_EPS) + ADAM_WD * w_ref[...])
        nm_ref[...] = m_new
        nv_ref[...] = v_new

    row = pl.BlockSpec((tr, C), lambda i: (i, 0))
    return pl.pallas_call(
        body, name=name, grid=(R // tr,), in_specs=[row] * 4, out_specs=[row] * 3,
        out_shape=[jax.ShapeDtypeStruct((R, C), F32)] * 3, compiler_params=_params("parallel"),
    )(w, g, m, v)


def _sum_core_halves(g, r1, c_idx, name):
    _, _, Rh, C = g.shape
    tr = _row_tile(Rh, C, 4, 2 << 20)

    def body(c_ref, g_ref, r_ref, o_ref):
        o_ref[...] = g_ref[...] + r_ref[...]

    grid_spec = pltpu.PrefetchScalarGridSpec(
        num_scalar_prefetch=1, grid=(N_CHIPS, Rh // tr),
        in_specs=[pl.BlockSpec((None, None, tr, C), lambda s, r, c_ref: (s, c_ref[0], r, 0)),
                  pl.BlockSpec((None, tr, C), lambda s, r, c_ref: (s, r, 0))],
        out_specs=pl.BlockSpec((None, tr, C), lambda s, r, c_ref: (s, r, 0)))
    return pl.pallas_call(
        body, name=name, grid_spec=grid_spec, out_shape=jax.ShapeDtypeStruct((N_CHIPS, Rh, C), F32),
        compiler_params=_params("parallel", "parallel"),
    )(c_idx, g, r1)


def _sum_chips(h, r2, place, name):
    _, Rh, C = h.shape
    tr = _row_tile(Rh, C, 4, 1 << 20)

    def body(place_ref, h_ref, r_ref, o_ref):
        o_ref[...] = ((h_ref[...] + r_ref[0]) + r_ref[1]) + r_ref[2]

    grid_spec = pltpu.PrefetchScalarGridSpec(
        num_scalar_prefetch=1, grid=(Rh // tr,),
        in_specs=[pl.BlockSpec((None, tr, C), lambda r, pr: (pr[0], r, 0)),
                  pl.BlockSpec((N_CHIPS - 1, tr, C), lambda r, pr: (0, r, 0))],
        out_specs=pl.BlockSpec((None, tr, C), lambda r, pr: (pr[1], r, 0)))
    return pl.pallas_call(
        body, name=name, grid_spec=grid_spec, out_shape=jax.ShapeDtypeStruct((2, Rh, C), F32),
        compiler_params=_params("parallel"),
    )(place, h, r2)


ANY = pl.BlockSpec(memory_space=pl.ANY)


def _place():
    x, y, c = lax.axis_index("x"), lax.axis_index("y"), lax.axis_index("c")
    chips = [(1 - x, y), (x, 1 - y), (1 - x, 1 - y)]
    return x, y, c, chips, [2 * cx + cy for cx, cy in chips]


def _allgather_chips(shards, name):
    n = len(shards)

    def body(*refs):
        ins, outs = refs[:n], refs[n:2 * n]
        send_sems, recv_sems = refs[2 * n:]
        x, y, c, chips, chip_idx = _place()
        me = 2 * x + y
        sibling = (x, y, 1 - c)

        def copy(t, k, src, dst, to):
            return pltpu.make_async_remote_copy(src_ref=src, dst_ref=dst, send_sem=send_sems.at[t, k],
                                                recv_sem=recv_sems.at[t, k], device_id=to, device_id_type=MESH)

        first, passed = [], []
        for t in range(n):
            for j, chip in enumerate(chips):
                cp = copy(t, j, ins[t].at[c], outs[t].at[me, c], (*chip, c))
                cp.start()
                first.append(cp)
        for t in range(n):
            for j, chip in enumerate(chips):
                landed = outs[t].at[chip_idx[j], c]
                copy(t, j, landed, landed, (*chip, c)).wait_recv()
                fw = copy(t, 3 + j, landed, landed, sibling)
                fw.start()
                passed.append(fw)
        for t in range(n):
            for j in range(3):
                other = outs[t].at[chip_idx[j], 1 - c]
                copy(t, 3 + j, other, other, sibling).wait_recv()
        for cp in first + passed:
            cp.wait_send()

    return pl.pallas_call(
        body, name=name, in_specs=[ANY] * n, out_specs=[ANY] * n,
        out_shape=[jax.ShapeDtypeStruct((N_CHIPS,) + s.shape, s.dtype) for s in shards],
        scratch_shapes=[pltpu.SemaphoreType.DMA((n, 6)), pltpu.SemaphoreType.DMA((n, 6))],
    )(*shards)


def _send_other_half_to_sibling(gs, name):
    n = len(gs)

    def body(*refs):
        ins, outs = refs[:n], refs[n:2 * n]
        send_sems, recv_sems = refs[2 * n:]
        x, y, c, _, _ = _place()
        copies = []
        for t in range(n):
            for s in range(N_CHIPS):
                cp = pltpu.make_async_remote_copy(
                    src_ref=ins[t].at[s, 1 - c], dst_ref=outs[t].at[s], send_sem=send_sems.at[t, s],
                    recv_sem=recv_sems.at[t, s], device_id=(x, y, 1 - c), device_id_type=MESH)
                cp.start()
                copies.append(cp)
        for cp in copies:
            cp.wait()

    return pl.pallas_call(
        body, name=name, in_specs=[ANY] * n, out_specs=[ANY] * n,
        out_shape=[jax.ShapeDtypeStruct((N_CHIPS,) + g.shape[2:], g.dtype) for g in gs],
        scratch_shapes=[pltpu.SemaphoreType.DMA((n, N_CHIPS)), pltpu.SemaphoreType.DMA((n, N_CHIPS))],
    )(*gs)


def _exchange_chips(hs, name):
    n = len(hs)

    def body(*refs):
        ins, outs = refs[:n], refs[n:2 * n]
        send_sems, recv_sems = refs[2 * n:]
        x, y, c, chips, chip_idx = _place()
        sends = []
        for t in range(n):
            for j, chip in enumerate(chips):
                cp = pltpu.make_async_remote_copy(
                    src_ref=ins[t].at[chip_idx[j]], dst_ref=outs[t].at[j], send_sem=send_sems.at[t, j],
                    recv_sem=recv_sems.at[t, j], device_id=(*chip, c), device_id_type=MESH)
                cp.start()
                sends.append(cp)
        for cp in sends:
            cp.wait()

    return pl.pallas_call(
        body, name=name, in_specs=[ANY] * n, out_specs=[ANY] * n,
        out_shape=[jax.ShapeDtypeStruct((N_CHIPS - 1,) + h.shape[1:], h.dtype) for h in hs],
        scratch_shapes=[pltpu.SemaphoreType.DMA((n, 3)), pltpu.SemaphoreType.DMA((n, 3))],
    )(*hs)


def _join_core_halves(bufs, name):
    n = len(bufs)

    def body(*refs):
        outs = refs[n:2 * n]
        send_sems, recv_sems = refs[2 * n:]
        x, y, c, _, _ = _place()
        sends = []
        for t in range(n):
            cp = pltpu.make_async_remote_copy(
                src_ref=outs[t].at[c], dst_ref=outs[t].at[c], send_sem=send_sems.at[t], recv_sem=recv_sems.at[t],
                device_id=(x, y, 1 - c), device_id_type=MESH)
            cp.start()
            sends.append(cp)
        for t in range(n):
            other = outs[t].at[1 - c]
            pltpu.make_async_remote_copy(
                src_ref=other, dst_ref=other, send_sem=send_sems.at[t], recv_sem=recv_sems.at[t],
                device_id=(x, y, 1 - c), device_id_type=MESH).wait_recv()
        for cp in sends:
            cp.wait_send()

    return pl.pallas_call(
        body, name=name, in_specs=[ANY] * n, out_specs=[ANY] * n,
        out_shape=[jax.ShapeDtypeStruct(b.shape, b.dtype) for b in bufs],
        input_output_aliases={t: t for t in range(n)},
        scratch_shapes=[pltpu.SemaphoreType.DMA((n,)), pltpu.SemaphoreType.DMA((n,))],
    )(*bufs)


def _allreduce_small(vec, name):
    R = vec.shape[0]

    def body(v_ref, o_ref, buf, send_sems, recv_sems):
        x, y, c = lax.axis_index("x"), lax.axis_index("y"), lax.axis_index("c")
        me = 4 * x + 2 * y + c
        buf[me] = v_ref[...]
        peers = []
        for k in range(1, N_DEV):
            px = 1 - x if k & 4 else x
            py = 1 - y if k & 2 else y
            pc = 1 - c if k & 1 else c
            peers.append(((px, py, pc), 4 * px + 2 * py + pc))
        sends = []
        for k, (peer, _) in enumerate(peers):
            cp = pltpu.make_async_remote_copy(
                src_ref=v_ref, dst_ref=buf.at[me], send_sem=send_sems.at[k], recv_sem=recv_sems.at[k],
                device_id=peer, device_id_type=MESH)
            cp.start()
            sends.append(cp)
        for k, (peer, idx) in enumerate(peers):
            pltpu.make_async_remote_copy(
                src_ref=v_ref, dst_ref=buf.at[idx], send_sem=send_sems.at[k], recv_sem=recv_sems.at[k],
                device_id=peer, device_id_type=MESH).wait_recv()
        for cp in sends:
            cp.wait_send()
        total = buf[0]
        for d in range(1, N_DEV):
            total = total + buf[d]
        o_ref[...] = total

    vm = pl.BlockSpec(memory_space=pltpu.VMEM)
    return pl.pallas_call(
        body, name=name, in_specs=[vm], out_specs=vm, out_shape=jax.ShapeDtypeStruct((R, LANES), F32),
        scratch_shapes=[pltpu.VMEM((N_DEV, R, LANES), F32), pltpu.SemaphoreType.DMA((N_DEV - 1,)),
                        pltpu.SemaphoreType.DMA((N_DEV - 1,))],
    )(vec)


def _halves(a):
    lead = 1
    for d in a.shape[:-1]:
        lead *= d
    return a.reshape(2, lead // 2, a.shape[-1])


def _cols_from_shards(g):
    return g.transpose(1, 0, 2).reshape(g.shape[1], N_CHIPS * g.shape[2])


def _shards_from_cols(w):
    return w.reshape(w.shape[0], N_CHIPS, w.shape[1] // N_CHIPS).transpose(1, 0, 2)


def _pack(parts, rows):
    flat = jnp.concatenate([p.reshape(-1).astype(F32) for p in parts])
    return jnp.pad(flat, (0, rows * LANES - flat.shape[0])).reshape(rows, LANES)


def _unpack(packed, shapes):
    flat = packed.reshape(-1)
    out, off = [], 0
    for s in shapes:
        n = 1
        for d in s:
            n *= d
        out.append(flat[off:off + n].reshape(s))
        off += n
    return out


def _packed_rows(shapes):
    n = 0
    for s in shapes:
        k = 1
        for d in s:
            k *= d
        n += k
    return -(-n // (8 * LANES)) * 8


def _pad_lanes(a):
    return jnp.pad(a, ((0, 0), (0, LANES - a.shape[1])))


def kernel(x, mix_norm_g, ffn_norm_g, fox_w_in, fox_b_f, fox_q_norm_g, fox_k_norm_g, fox_w_out, pool_w, pool_b, pool_scale, ffn_w_gate_up, ffn_w_down, loss_target, m_mix_norm_g, m_ffn_norm_g, m_fox_w_in, m_fox_b_f, m_fox_q_norm_g, m_fox_k_norm_g, m_fox_w_out, m_pool_w, m_pool_b, m_pool_scale, m_ffn_w_gate_up, m_ffn_w_down, v_mix_norm_g, v_ffn_norm_g, v_fox_w_in, v_fox_b_f, v_fox_q_norm_g, v_fox_k_norm_g, v_fox_w_out, v_pool_w, v_pool_b, v_pool_scale, v_ffn_w_gate_up, v_ffn_w_down):
    _, S, D = x.shape
    H = D // HEAD_DIM
    depth = mix_norm_g.shape[0]
    n_pool = pool_w.shape[0]
    NG = len(POOL_WINDOWS)
    G = D // NG
    F = ffn_w_down.shape[1] * N_CHIPS
    ax, ay, ac = lax.axis_index("x"), lax.axis_index("y"), lax.axis_index("c")
    chip = 2 * ax + ay
    c_idx = jnp.reshape(ac, (1,)).astype(jnp.int32)
    place = jnp.stack([chip, ac]).astype(jnp.int32)
    xs = x[0]
    target = loss_target[0]

    Dq = D // N_CHIPS
    small_fwd_shapes = [(n_pool, D), (n_pool, D)]
    placed = []
    for p in (pool_b, pool_scale):
        full = lax.dynamic_update_slice(jnp.zeros((n_pool, D), F32), p, (0, chip * Dq))
        placed.append(jnp.where(ac == 0, full, jnp.zeros_like(full)))
    pool_b_full, pool_scale_full = _unpack(
        _allreduce_small(_pack(placed, _packed_rows(small_fwd_shapes)), "gather_pool_vectors"), small_fwd_shapes)

    weights = []
    for i in range(depth):
        j = i // 2
        mixer = ([fox_w_in[j], fox_w_out[j]] if i % 2 == 0 else [pool_w[j]])
        shards = [_halves(w.astype(BF16)) for w in mixer + [ffn_w_gate_up[i], ffn_w_down[i]]]
        got = _allgather_chips(shards, f"allgather_weights_{'fox' if i % 2 == 0 else 'pool'}")
        got = [lax.dynamic_update_slice(g, sh[None], (chip, 0, 0, 0)) for g, sh in zip(got, shards)]
        lw = {}
        if i % 2 == 0:
            w_in = _cols_from_shards(got[0].reshape(N_CHIPS, D, -1))
            lw["w_qkv"] = w_in[:, :3 * D]
            lw["w_f"] = _pad_lanes(w_in[:, 3 * D:])
            lw["w_out"] = got[1].reshape(D, D)
        else:
            lw["w_pool"] = got[0].reshape(N_CHIPS, NG, G // N_CHIPS, G).transpose(1, 0, 2, 3).reshape(NG, G, G)
        lw["w_gu"] = _cols_from_shards(got[-2].reshape(N_CHIPS, D, -1))
        lw["w_down"] = got[-1].reshape(F, D)
        weights.append(lw)

    saved = []
    cur = xs
    for i in range(depth):
        j = i // 2
        lw = weights[i]
        sv = {"x": cur}
        g_mix = mix_norm_g[i][None]
        if i % 2 == 0:
            h1b = _rmsnorm_fwd(cur, g_mix, BF16, "rmsnorm_fwd_bf16")
            proj = _matmul(h1b, lw["w_qkv"], "nn", F32, "qkv_proj", M=S, N=3 * D, K=D)
            fl = _matmul(h1b, lw["w_f"], "nn", F32, "forget_proj", M=S, N=LANES, K=D)
            bf = _pad_lanes(fox_b_f[j][None])
            cum = _fgate_fwd(fl, bf, "forget_cumsum")
            q2, kn, vb, aux_k = _qkv_post(proj, fox_q_norm_g[j][None], fox_k_norm_g[j][None], cum, "qk_norm")
            o, aux_q = _flash_fwd(q2, kn, vb, aux_k, "fox_attention_fwd")
            x1 = _matmul(o, lw["w_out"], "nn", F32, "attn_out_proj", M=S, N=D, K=D, res=cur)
            sv.update(h1b=h1b, proj=proj, fl=fl, bf=bf, q2=q2, kn=kn, vb=vb, aux_k=aux_k, aux_q=aux_q, o=o)
        else:
            h1 = _rmsnorm_fwd(cur, g_mix, F32, "rmsnorm_fwd_f32")
            yb, x1 = _pool_fwd(h1, cur, lw["w_pool"], pool_b_full[j][None], pool_scale_full[j][None], "pool_fwd")
            sv.update(yb=yb)
        h2b = _rmsnorm_fwd(x1, ffn_norm_g[i][None], BF16, "rmsnorm_fwd_bf16")
        gu, act = _ffn_up(h2b, lw["w_gu"], "ffn_gate_up")
        x2 = _matmul(act, lw["w_down"], "nn", F32, "ffn_down", M=S, N=D, K=F, tk=2816, res=x1)
        sv.update(x1=x1, h2b=h2b, gu=gu, act=act)
        saved.append(sv)
        cur = x2

    dcur, dcur_b, sq = _loss_grad(cur, target, "loss_grad")
    loss = lax.psum(sq[0, 0] * (0.5 / D), ("x", "y", "c"))

    g_mix_rows, g_ffn_rows = [None] * depth, [None] * depth
    g_bf, g_gq, g_gk = [None] * (depth - n_pool), [None] * (depth - n_pool), [None] * (depth - n_pool)
    g_pb, g_ps = [None] * n_pool, [None] * n_pool
    reduced = [None] * depth
    nkh = None
    for i in reversed(range(depth)):
        j = i // 2
        lw, sv = weights[i], saved[i]
        dgu = _ffn_act_bwd(dcur_b, lw["w_down"], sv["gu"], "ffn_act_bwd")
        d_w_down = _matmul(sv["act"], dcur_b, "tn", F32, "ffn_down_dw", M=F, N=D, K=S, tm=1408)
        dh2 = _matmul(
            dgu, lw["w_gu"], "nt", F32, "ffn_up_dx", M=S, N=D, K=2 * F, tk=_tile(F, 2816),
            a_spec=lambda tm, tn, tk: pl.BlockSpec(
                (None, tm, tk), lambda i_, j_, k_: (k_ // (F // tk), i_, k_ % (F // tk))))
        d_w_gu = _matmul(
            sv["h2b"], dgu, "tn", F32, "ffn_up_dw", M=D, N=2 * F, K=S, tn=_tile(F, 1408),
            b_spec=lambda tm, tn, tk: pl.BlockSpec(
                (None, tk, tn), lambda i_, j_, k_: (j_ // (F // tn), k_, j_ % (F // tn))))
        dx1, dx1b, g_ffn_rows[i] = _rmsnorm_bwd(sv["x1"], ffn_norm_g[i][None], dh2, dcur, "rmsnorm_bwd")
        g_mix = mix_norm_g[i][None]
        if i % 2 == 0:
            do = _matmul(dx1b, lw["w_out"], "nt", F32, "attn_out_dx", M=S, N=D, K=D)
            d_w_out = _matmul(sv["o"], dx1b, "tn", F32, "attn_out_dw", M=D, N=D, K=S)
            aux_do, dob = _attn_delta(do, sv["o"], "attn_delta")
            dqx, dkx, dvb = _flash_bwd(sv["q2"], sv["kn"], sv["vb"], dob, sv["aux_q"], sv["aux_k"], aux_do,
                                       "fox_attention_bwd")
            dproj, dgq, dgk = _qkv_post_bwd(sv["proj"], dqx, dkx, dvb, fox_q_norm_g[j][None],
                                            fox_k_norm_g[j][None], "qk_norm_bwd")
            dc = _pad_lanes(dqx.reshape(S, H, 2 * HEAD_DIM)[:, :, HEAD_DIM + AUX_B]
                            - dkx.reshape(S, H, 2 * HEAD_DIM)[:, :, HEAD_DIM + AUX_A])
            dfl, dbf = _fgate_bwd(dc, sv["fl"], sv["bf"], "forget_cumsum_bwd")
            d_w_qkv = _matmul(sv["h1b"], dproj, "tn", F32, "qkv_dw", M=D, N=3 * D, K=S)
            d_w_f = _matmul(sv["h1b"], dfl, "tn", F32, "forget_dw", M=D, N=LANES, K=S)
            dh1f = _matmul(dfl, lw["w_f"], "nt", F32, "forget_dx", M=S, N=D, K=LANES)
            dh1 = _matmul(dproj, lw["w_qkv"], "nt", F32, "qkv_dx", M=S, N=D, K=3 * D, res=dh1f)
            d_w_in = jnp.concatenate([d_w_qkv, d_w_f[:, :H]], axis=1)
            big = [_shards_from_cols(d_w_in), d_w_out.reshape(N_CHIPS, D // N_CHIPS, D)]
            g_bf[j], g_gq[j], g_gk[j] = dbf[0, :H], dgq[0], dgk[0]
        else:
            dyc, d_wp, dpb, dps = _pool_bwd_mix(dx1, sv["yb"], lw["w_pool"], pool_b_full[j][None],
                                                pool_scale_full[j][None], "pool_bwd_mix")
            dh1 = _pool_bwd_window(dyc, "pool_bwd_window")
            big = [d_wp.reshape(NG, N_CHIPS, G // N_CHIPS, G).transpose(1, 0, 2, 3)]
            g_pb[j], g_ps[j] = dpb[0], dps[0]
        big += [_shards_from_cols(d_w_gu), d_w_down.reshape(N_CHIPS, F // N_CHIPS, D)]
        dcur, dcur_b, g_mix_rows[i] = _rmsnorm_bwd(sv["x"], g_mix, dh1, dx1, "rmsnorm_bwd")

        kind = "fox" if i % 2 == 0 else "pool"
        gs = [b.reshape(N_CHIPS, 2, -1, b.shape[-1]) for b in big]
        r1 = _send_other_half_to_sibling(gs, f"grads_to_sibling_{kind}")
        hs = [_sum_core_halves(g, r, c_idx, "sum_core_halves") for g, r in zip(gs, r1)]
        r2 = _exchange_chips(hs, f"grads_to_chips_{kind}")
        rs = [_sum_chips(h, r, place, "sum_chips") for h, r in zip(hs, r2)]
        reduced[i] = _join_core_halves(rs, f"grads_join_{kind}")

    small_shapes = [(depth, D), (depth, D), fox_b_f.shape, fox_q_norm_g.shape, fox_k_norm_g.shape,
                    (n_pool, D), (n_pool, D)]
    small = [jnp.stack(g_mix_rows)[:, 0], jnp.stack(g_ffn_rows)[:, 0], jnp.stack(g_bf), jnp.stack(g_gq),
             jnp.stack(g_gk), jnp.stack(g_pb), jnp.stack(g_ps)]
    (gr_mix, gr_ffn, gr_bf, gr_gq, gr_gk, gr_pb_full, gr_ps_full) = _unpack(
        _allreduce_small(_pack(small, _packed_rows(small_shapes)), "allreduce_small_grads"), small_shapes)
    gr_pb = lax.dynamic_slice(gr_pb_full, (0, chip * Dq), (n_pool, Dq))
    gr_ps = lax.dynamic_slice(gr_ps_full, (0, chip * Dq), (n_pool, Dq))

    fox_layers = [i for i in range(depth) if i % 2 == 0]
    pool_layers = [i for i in range(depth) if i % 2 == 1]
    gr_w_in = jnp.stack([reduced[i][0].reshape(fox_w_in.shape[1:]) for i in fox_layers])
    gr_w_out = jnp.stack([reduced[i][1].reshape(fox_w_out.shape[1:]) for i in fox_layers])
    gr_pool_w = jnp.stack([reduced[i][0].reshape(pool_w.shape[1:]) for i in pool_layers])
    gr_gu = jnp.stack([reduced[i][-2].reshape(ffn_w_gate_up.shape[1:]) for i in range(depth)])
    gr_down = jnp.stack([reduced[i][-1].reshape(ffn_w_down.shape[1:]) for i in range(depth)])

    def update_big(w, g, m, v, name):
        flat = lambda a: a.reshape(-1, a.shape[-1])
        return [o.reshape(w.shape) for o in _adamw(flat(w), flat(g), flat(m), flat(v), name)]

    names = ["mix_norm_g", "ffn_norm_g", "fox_w_in", "fox_b_f", "fox_q_norm_g", "fox_k_norm_g", "fox_w_out",
             "pool_w", "pool_b", "pool_scale", "ffn_w_gate_up", "ffn_w_down"]
    ws = dict(zip(names, [mix_norm_g, ffn_norm_g, fox_w_in, fox_b_f, fox_q_norm_g, fox_k_norm_g, fox_w_out,
                          pool_w, pool_b, pool_scale, ffn_w_gate_up, ffn_w_down]))
    ms = dict(zip(names, [m_mix_norm_g, m_ffn_norm_g, m_fox_w_in, m_fox_b_f, m_fox_q_norm_g, m_fox_k_norm_g,
                          m_fox_w_out, m_pool_w, m_pool_b, m_pool_scale, m_ffn_w_gate_up, m_ffn_w_down]))
    vs = dict(zip(names, [v_mix_norm_g, v_ffn_norm_g, v_fox_w_in, v_fox_b_f, v_fox_q_norm_g, v_fox_k_norm_g,
                          v_fox_w_out, v_pool_w, v_pool_b, v_pool_scale, v_ffn_w_gate_up, v_ffn_w_down]))
    grads = dict(mix_norm_g=gr_mix, ffn_norm_g=gr_ffn, fox_w_in=gr_w_in, fox_b_f=gr_bf, fox_q_norm_g=gr_gq,
                 fox_k_norm_g=gr_gk, fox_w_out=gr_w_out, pool_w=gr_pool_w, pool_b=gr_pb, pool_scale=gr_ps,
                 ffn_w_gate_up=gr_gu, ffn_w_down=gr_down)
    big_names = ["fox_w_in", "fox_w_out", "pool_w", "ffn_w_gate_up", "ffn_w_down"]
    small_names = [n for n in names if n not in big_names]
    delta, new_m, new_v = {}, {}, {}
    for n in big_names:
        delta[n], new_m[n], new_v[n] = update_big(ws[n], grads[n], ms[n], vs[n], "adamw_" + n)
    shapes = [ws[n].shape for n in small_names]
    rows = _packed_rows(shapes)
    packed = _adamw(_pack([ws[n] for n in small_names], rows), _pack([grads[n] for n in small_names], rows),
                    _pack([ms[n] for n in small_names], rows), _pack([vs[n] for n in small_names], rows),
                    "adamw_small")
    for dst, pk in zip((delta, new_m, new_v), packed):
        for n, a in zip(small_names, _unpack(pk, shapes)):
            dst[n] = a

    return (loss, dcur[None], *[grads[n] for n in names], *[delta[n] for n in names],
            *[new_m[n] for n in names], *[new_v[n] for n in names])
```

```python
import functools

import numpy as np
import jax
import jax.numpy as jnp
from jax import lax
from jax.experimental import pallas as pl
from jax.experimental.pallas import tpu as pltpu

F32 = jnp.float32
BF16 = jnp.bfloat16
MESH = pl.DeviceIdType.MESH

HEAD_DIM = 128
RMS_EPS = 1e-6
NEG_INF = -1e30
POOL_WINDOWS = (2, 4, 8, 16)
POOL_HALO = 16
LANES = 128
N_CHIPS = 4
N_DEV = 8
VMEM_LIMIT = 56 * 1024 * 1024

ADAM_LR = 0.001
ADAM_B1 = 0.9
ADAM_B2 = 0.999
ADAM_EPS = 1e-08
ADAM_WD = 0.01
ADAM_STEP = 10

NN = (((1,), (0,)), ((), ()))
NT = (((1,), (1,)), ((), ()))
TN = (((0,), (0,)), ((), ()))


def _params(*sem):
    return pltpu.CompilerParams(dimension_semantics=sem if sem else None, vmem_limit_bytes=VMEM_LIMIT)


def _tile(n, pref, unit=LANES):
    best = None
    t = unit
    while t <= min(n, pref):
        if n % t == 0:
            best = t
        t += unit
    return best if best is not None else n


def _row_tile(rows, cols, itemsize, budget):
    best = None
    for t in range(16, rows + 1, 16):
        if rows % t == 0 and t * cols * itemsize <= budget:
            best = t
    return best if best is not None else rows


def _dot(a, b, dims):
    return lax.dot_general(a, b, dims, preferred_element_type=F32)


class _Side:
    def __init__(self, operands, out_shapes, sem_shapes, start, finish):
        self.operands, self.out_shapes, self.sem_shapes = list(operands), list(out_shapes), list(sem_shapes)
        self.start, self.finish = start, finish

    def split(self, refs, n_in, n_out, n_scratch):
        si, so, ss = len(self.operands), len(self.out_shapes), len(self.sem_shapes)
        refs = list(refs)
        cuts = [n_in, si, n_out, so, n_scratch, ss]
        parts, at = [], 0
        for c in cuts:
            parts.append(refs[at:at + c])
            at += c
        ins, s_ins, outs, s_outs, scratch, sems = parts
        return ins + outs + scratch, (s_ins, s_outs, sems)

    def bracket(self, side_refs, first, last, compute):
        @pl.when(first)
        def _():
            self.start(*side_refs)

        compute()

        @pl.when(last)
        def _():
            self.finish(*side_refs)


def _grid_ends(grid):
    ids = [pl.program_id(a) for a in range(len(grid))]
    first, last = ids[0] == 0, ids[0] == grid[0] - 1
    for pid, n in zip(ids[1:], grid[1:]):
        first, last = first & (pid == 0), last & (pid == n - 1)
    return first, last


def _host(side, body, grid, n_in, n_out, n_scratch):
    if side is None:
        return body

    def hosted(*refs):
        own, side_refs = side.split(refs, n_in, n_out, n_scratch)
        first, last = _grid_ends(grid)
        side.bracket(side_refs, first, last, lambda: body(*own))

    return hosted


def _matmul(a, b, mode, out_dtype, name, *, M, N, K, tm=1024, tn=1024, tk=2048, res=None,
            a_spec=None, b_spec=None, out_shards=None, side=None):
    tm, tn, tk = _tile(M, tm), _tile(N, tn), _tile(K, tk)
    nk = K // tk
    dims = {"nn": NN, "nt": NT, "tn": TN}[mode]
    if a_spec is None:
        a_spec = (pl.BlockSpec((tk, tm), lambda i, j, k: (k, i)) if mode == "tn"
                  else pl.BlockSpec((tm, tk), lambda i, j, k: (i, k)))
    else:
        a_spec = a_spec(tm, tn, tk)
    if b_spec is None:
        b_spec = (pl.BlockSpec((tn, tk), lambda i, j, k: (j, k)) if mode == "nt"
                  else pl.BlockSpec((tk, tn), lambda i, j, k: (k, j)))
    else:
        b_spec = b_spec(tm, tn, tk)
    if out_shards is None:
        o_spec = pl.BlockSpec((tm, tn), lambda i, j, k: (i, j))
        out_shape = jax.ShapeDtypeStruct((M, N), out_dtype)
    else:
        per = N // out_shards // tn
        o_spec = pl.BlockSpec((None, tm, tn), lambda i, j, k: (j // per, i, j % per))
        out_shape = jax.ShapeDtypeStruct((out_shards, M, N // out_shards), out_dtype)
    has_res = res is not None

    def body(a_ref, b_ref, *rest):
        if has_res:
            r_ref, o_ref, acc = rest
        else:
            o_ref, acc = rest

        def finish(total):
            if has_res:
                total = total + r_ref[...]
            o_ref[...] = total.astype(o_ref.dtype)

        part = _dot(a_ref[...], b_ref[...], dims)
        if nk == 1:
            finish(part)
        else:
            k = pl.program_id(2)

            @pl.when(k == 0)
            def _():
                acc[...] = part

            @pl.when(k > 0)
            def _():
                acc[...] += part

            @pl.when(k == nk - 1)
            def _():
                finish(acc[...])

    grid = (M // tm, N // tn, nk)
    operands = [a, b] + ([res] if has_res else [])
    in_specs = [a_spec, b_spec] + ([o_spec] if has_res else [])
    out_specs, out_shapes = [o_spec], [out_shape]
    scratch = [pltpu.VMEM((tm, tn) if nk > 1 else (8, LANES), F32)]
    sem = ("parallel", "parallel", "arbitrary")
    if side is not None:
        body = _host(side, body, grid, len(operands), 1, 1)
        operands, in_specs = operands + side.operands, in_specs + [ANY] * len(side.operands)
        out_specs, out_shapes = out_specs + [ANY] * len(side.out_shapes), out_shapes + side.out_shapes
        scratch = scratch + side.sem_shapes
        sem = ("arbitrary",) * 3
    outs = pl.pallas_call(
        body, name=name, grid=grid, in_specs=in_specs, out_specs=out_specs, out_shape=out_shapes,
        scratch_shapes=scratch, compiler_params=_params(*sem),
    )(*operands)
    return outs[0] if side is None else (outs[0], list(outs[1:]))


def _rmsnorm_fwd(x, g, out_dtype, name):
    S, D = x.shape
    tm = _tile(S, 512, 16)

    def body(x_ref, g_ref, o_ref):
        xv = x_ref[...]
        r = lax.rsqrt(jnp.mean(xv * xv, axis=-1, keepdims=True) + RMS_EPS)
        o_ref[...] = ((xv * r) * g_ref[...]).astype(o_ref.dtype)

    row = pl.BlockSpec((tm, D), lambda i: (i, 0))
    return pl.pallas_call(
        body, name=name, grid=(S // tm,), in_specs=[row, pl.BlockSpec((1, D), lambda i: (0, 0))],
        out_specs=row, out_shape=jax.ShapeDtypeStruct((S, D), out_dtype),
        compiler_params=_params("parallel"),
    )(x, g)


def _rmsnorm_bwd(x, g, dh, dres, name):
    S, D = x.shape
    tm = _tile(S, 256, 16)

    def body(x_ref, g_ref, dh_ref, dres_ref, dx_ref, dxb_ref, dg_ref):
        xv = x_ref[...]
        r = lax.rsqrt(jnp.mean(xv * xv, axis=-1, keepdims=True) + RMS_EPS)
        xhat = xv * r
        dhv = dh_ref[...]
        dxhat = dhv * g_ref[...]
        dx = dres_ref[...] + r * (dxhat - xhat * jnp.mean(dxhat * xhat, axis=-1, keepdims=True))
        dx_ref[...] = dx
        dxb_ref[...] = dx.astype(BF16)

        @pl.when(pl.program_id(0) == 0)
        def _():
            dg_ref[...] = jnp.zeros_like(dg_ref)

        dg_ref[...] += jnp.sum(dhv * xhat, axis=0, keepdims=True)

    row = pl.BlockSpec((tm, D), lambda i: (i, 0))
    vec = pl.BlockSpec((1, D), lambda i: (0, 0))
    return pl.pallas_call(
        body, name=name, grid=(S // tm,), in_specs=[row, vec, row, row], out_specs=[row, row, vec],
        out_shape=[jax.ShapeDtypeStruct((S, D), F32), jax.ShapeDtypeStruct((S, D), BF16),
                   jax.ShapeDtypeStruct((1, D), F32)],
        compiler_params=_params("arbitrary"),
    )(x, g, dh, dres)


def _loss_grad(y, target, name):
    S, D = y.shape
    tm = _tile(S, 256, 16)

    def body(y_ref, t_ref, dy_ref, dyb_ref, sq_ref):
        e = y_ref[...] - t_ref[...]
        dy = e / D
        dy_ref[...] = dy
        dyb_ref[...] = dy.astype(BF16)

        @pl.when(pl.program_id(0) == 0)
        def _():
            sq_ref[...] = jnp.zeros_like(sq_ref)

        total = jnp.sum(jnp.sum(e * e, axis=1, keepdims=True), axis=0, keepdims=True)
        sq_ref[...] += jnp.broadcast_to(total, sq_ref.shape)

    row = pl.BlockSpec((tm, D), lambda i: (i, 0))
    vec = pl.BlockSpec((1, LANES), lambda i: (0, 0))
    return pl.pallas_call(
        body, name=name, grid=(S // tm,), in_specs=[row, row], out_specs=[row, row, vec],
        out_shape=[jax.ShapeDtypeStruct((S, D), F32), jax.ShapeDtypeStruct((S, D), BF16),
                   jax.ShapeDtypeStruct((1, LANES), F32)],
        compiler_params=_params("arbitrary"),
    )(y, target)


def _ffn_up(hb, w_gu, name, side=None):
    S, D = hb.shape
    F = w_gu.shape[1] // 2
    tm, tn, tk = _tile(S, 1024), _tile(F, 512), _tile(D, 2048)
    nk, nj = D // tk, F // tn

    def body(a_ref, wg_ref, wu_ref, gu_ref, act_ref, accg, accu):
        pg = _dot(a_ref[...], wg_ref[...], NN)
        pu = _dot(a_ref[...], wu_ref[...], NN)

        def finish(g, u):
            gu_ref[0] = g
            gu_ref[1] = u
            act_ref[...] = ((g / (1.0 + jnp.exp(-g))) * u).astype(BF16)

        if nk == 1:
            finish(pg, pu)
        else:
            k = pl.program_id(2)

            @pl.when(k == 0)
            def _():
                accg[...] = pg
                accu[...] = pu

            @pl.when(k > 0)
            def _():
                accg[...] += pg
                accu[...] += pu

            @pl.when(k == nk - 1)
            def _():
                finish(accg[...], accu[...])

    acc_shape = (tm, tn) if nk > 1 else (8, LANES)
    grid = (S // tm, nj, nk)
    operands = [hb, w_gu, w_gu]
    in_specs = [pl.BlockSpec((tm, tk), lambda i, j, k: (i, k)),
                pl.BlockSpec((tk, tn), lambda i, j, k: (k, j)),
                pl.BlockSpec((tk, tn), lambda i, j, k: (k, j + nj))]
    out_specs = [pl.BlockSpec((2, tm, tn), lambda i, j, k: (0, i, j)),
                 pl.BlockSpec((tm, tn), lambda i, j, k: (i, j))]
    out_shapes = [jax.ShapeDtypeStruct((2, S, F), F32), jax.ShapeDtypeStruct((S, F), BF16)]
    scratch = [pltpu.VMEM(acc_shape, F32), pltpu.VMEM(acc_shape, F32)]
    sem = ("parallel", "parallel", "arbitrary")
    if side is not None:
        body = _host(side, body, grid, 3, 2, 2)
        operands, in_specs = operands + side.operands, in_specs + [ANY] * len(side.operands)
        out_specs, out_shapes = out_specs + [ANY] * len(side.out_shapes), out_shapes + side.out_shapes
        scratch = scratch + side.sem_shapes
        sem = ("arbitrary",) * 3
    outs = pl.pallas_call(
        body, name=name, grid=grid, in_specs=in_specs, out_specs=out_specs, out_shape=out_shapes,
        scratch_shapes=scratch, compiler_params=_params(*sem),
    )(*operands)
    return outs[0], outs[1], list(outs[2:])


def _ffn_act_bwd(dyb, w_down, gu, name):
    S, D = dyb.shape
    F = w_down.shape[0]
    tm, tn, tk = _tile(S, 1024), _tile(F, 512), _tile(D, 2048)
    nk = D // tk

    def body(a_ref, w_ref, gu_ref, dgu_ref, acc):
        part = _dot(a_ref[...], w_ref[...], NT)

        def finish(dact):
            g = gu_ref[0]
            u = gu_ref[1]
            sg = 1.0 / (1.0 + jnp.exp(-g))
            dgu_ref[0] = (dact * u * (sg * (1.0 + g * (1.0 - sg)))).astype(BF16)
            dgu_ref[1] = (dact * (g * sg)).astype(BF16)

        if nk == 1:
            finish(part)
        else:
            k = pl.program_id(2)

            @pl.when(k == 0)
            def _():
                acc[...] = part

            @pl.when(k > 0)
            def _():
                acc[...] += part

            @pl.when(k == nk - 1)
            def _():
                finish(acc[...])

    gu_spec = pl.BlockSpec((2, tm, tn), lambda i, j, k: (0, i, j))
    return pl.pallas_call(
        body, name=name, grid=(S // tm, F // tn, nk),
        in_specs=[pl.BlockSpec((tm, tk), lambda i, j, k: (i, k)),
                  pl.BlockSpec((tn, tk), lambda i, j, k: (j, k)), gu_spec],
        out_specs=gu_spec, out_shape=jax.ShapeDtypeStruct((2, S, F), BF16),
        scratch_shapes=[pltpu.VMEM((tm, tn) if nk > 1 else (8, LANES), F32)],
        compiler_params=_params("parallel", "parallel", "arbitrary"),
    )(dyb, w_down, gu)


LOG2E = 1.4426950408889634
Q_PRESCALE = HEAD_DIM ** -0.5 * LOG2E
AUX_A = 0
AUX_B = 3


def _aux_block(lane, minus=None, minus_at=None, ones_at=None):
    out = jnp.zeros(lane.shape, BF16)
    if minus is not None:
        x = -minus
        hi = x.astype(BF16)
        r1 = x - hi.astype(F32)
        mid = r1.astype(BF16)
        lo = (r1 - mid.astype(F32)).astype(BF16)
        for n, piece in enumerate((hi, mid, lo)):
            out = jnp.where(lane == minus_at + n, piece, out)
    if ones_at is not None:
        out = jnp.where((lane >= ones_at) & (lane < ones_at + 3), jnp.ones(lane.shape, BF16), out)
    return out


def _qkv_post(proj, gq, gk, cum, name):
    S, D3 = proj.shape
    D = D3 // 3
    tm = _tile(S, 256, 16)

    def body(q_ref, k_ref, v_ref, gq_ref, gk_ref, c_ref, qn_ref, kn_ref, vb_ref, ak_ref):
        lane = lax.broadcasted_iota(jnp.int32, (tm, HEAD_DIM), 1)
        for src, g_ref, dst, mult in ((q_ref, gq_ref, qn_ref, Q_PRESCALE), (k_ref, gk_ref, kn_ref, None)):
            for h in range(D // HEAD_DIM):
                cols = slice(h * HEAD_DIM, (h + 1) * HEAD_DIM)
                t = src[:, cols]
                r = lax.rsqrt(jnp.mean(t * t, axis=-1, keepdims=True) + RMS_EPS)
                n = (t * r) * g_ref[...]
                dst[:, cols] = (n if mult is None else n * mult).astype(BF16)
        for h in range(D // HEAD_DIM):
            ch = jnp.broadcast_to(c_ref[:, h:h + 1] * LOG2E, (tm, HEAD_DIM))
            ak_ref[:, h * HEAD_DIM:(h + 1) * HEAD_DIM] = _aux_block(lane, ch, AUX_A, AUX_B)
        vb_ref[...] = v_ref[...].astype(BF16)

    part = lambda n: pl.BlockSpec((tm, D), lambda i: (i, n))
    vec = pl.BlockSpec((1, HEAD_DIM), lambda i: (0, 0))
    row = pl.BlockSpec((tm, D), lambda i: (i, 0))
    return pl.pallas_call(
        body, name=name, grid=(S // tm,),
        in_specs=[part(0), part(1), part(2), vec, vec, pl.BlockSpec((tm, LANES), lambda i: (i, 0))],
        out_specs=[row, row, row, row], out_shape=[jax.ShapeDtypeStruct((S, D), BF16)] * 4,
        compiler_params=_params("parallel"),
    )(proj, proj, proj, gq, gk, cum)


def _qkv_post_bwd(proj, dqx, dkx, dvb, gq, gk, name):
    S, D = dvb.shape
    tm = _tile(S, 128, 16)

    def body(q_ref, k_ref, dqx_ref, dkx_ref, dvb_ref, gq_ref, gk_ref, dp_ref, dgq_ref, dgk_ref):
        @pl.when(pl.program_id(0) == 0)
        def _():
            dgq_ref[...] = jnp.zeros_like(dgq_ref)
            dgk_ref[...] = jnp.zeros_like(dgk_ref)

        for n, (src, dsrc, g_ref, dg_ref, mult) in enumerate(
                ((q_ref, dqx_ref, gq_ref, dgq_ref, HEAD_DIM ** -0.5), (k_ref, dkx_ref, gk_ref, dgk_ref, 1.0 / LOG2E))):
            dg = jnp.zeros((1, HEAD_DIM), F32)
            for h in range(D // HEAD_DIM):
                cols = slice(h * HEAD_DIM, (h + 1) * HEAD_DIM)
                t = src[:, cols]
                r = lax.rsqrt(jnp.mean(t * t, axis=-1, keepdims=True) + RMS_EPS)
                that = t * r
                dn = dsrc[:, 2 * h * HEAD_DIM:(2 * h + 1) * HEAD_DIM] * mult
                dhat = dn * g_ref[...]
                dt = r * (dhat - that * jnp.mean(dhat * that, axis=-1, keepdims=True))
                dp_ref[:, n * D + h * HEAD_DIM:n * D + (h + 1) * HEAD_DIM] = dt.astype(BF16)
                dg = dg + jnp.sum(dn * that, axis=0, keepdims=True)
            dg_ref[...] += dg
        dp_ref[:, 2 * D:3 * D] = dvb_ref[...]

    part = lambda n: pl.BlockSpec((tm, D), lambda i: (i, n))
    row = pl.BlockSpec((tm, D), lambda i: (i, 0))
    vec = pl.BlockSpec((1, HEAD_DIM), lambda i: (0, 0))
    wide = pl.BlockSpec((tm, 2 * D), lambda i: (i, 0))
    return pl.pallas_call(
        body, name=name, grid=(S // tm,), in_specs=[part(0), part(1), wide, wide, row, vec, vec],
        out_specs=[pl.BlockSpec((tm, 3 * D), lambda i: (i, 0)), vec, vec],
        out_shape=[jax.ShapeDtypeStruct((S, 3 * D), BF16), jax.ShapeDtypeStruct((1, HEAD_DIM), F32),
                   jax.ShapeDtypeStruct((1, HEAD_DIM), F32)],
        compiler_params=_params("arbitrary"),
    )(proj, proj, dqx, dkx, dvb, gq, gk)


def _log_sigmoid(z):
    return -(jnp.maximum(-z, 0.0) + jnp.log(1.0 + jnp.exp(-jnp.abs(z))))


def _fgate_fwd(fl, bf, name):
    S = fl.shape[0]
    T = _tile(S, 256, 16)

    def body(fl_ref, bf_ref, c_ref, carry):
        @pl.when(pl.program_id(0) == 0)
        def _():
            carry[...] = jnp.zeros_like(carry)

        logf = _log_sigmoid(fl_ref[...] + bf_ref[...])
        r = lax.broadcasted_iota(jnp.int32, (T, T), 0)
        c = lax.broadcasted_iota(jnp.int32, (T, T), 1)
        tri = (r >= c).astype(F32)
        cum = jnp.dot(tri, logf, precision=lax.Precision.HIGHEST, preferred_element_type=F32) + carry[...]
        c_ref[...] = cum
        carry[...] = cum[T - 1:T, :]

    row = pl.BlockSpec((T, LANES), lambda i: (i, 0))
    return pl.pallas_call(
        body, name=name, grid=(S // T,), in_specs=[row, pl.BlockSpec((1, LANES), lambda i: (0, 0))],
        out_specs=row, out_shape=jax.ShapeDtypeStruct((S, LANES), F32),
        scratch_shapes=[pltpu.VMEM((1, LANES), F32)], compiler_params=_params("arbitrary"),
    )(fl, bf)


def _fgate_bwd(dc, fl, bf, name):
    S = fl.shape[0]
    T = _tile(S, 256, 16)
    nb = S // T

    def body(dc_ref, fl_ref, bf_ref, dfl_ref, dbf_ref, carry):
        @pl.when(pl.program_id(0) == 0)
        def _():
            carry[...] = jnp.zeros_like(carry)
            dbf_ref[...] = jnp.zeros_like(dbf_ref)

        r = lax.broadcasted_iota(jnp.int32, (T, T), 0)
        c = lax.broadcasted_iota(jnp.int32, (T, T), 1)
        triu = (c >= r).astype(F32)
        dlogf = jnp.dot(triu, dc_ref[...], precision=lax.Precision.HIGHEST,
                        preferred_element_type=F32) + carry[...]
        carry[...] = dlogf[0:1, :]
        z = fl_ref[...] + bf_ref[...]
        dz = dlogf * (1.0 / (1.0 + jnp.exp(z)))
        dfl_ref[...] = dz.astype(BF16)
        dbf_ref[...] += jnp.sum(dz, axis=0, keepdims=True)

    row = pl.BlockSpec((T, LANES), lambda i: (nb - 1 - i, 0))
    vec = pl.BlockSpec((1, LANES), lambda i: (0, 0))
    return pl.pallas_call(
        body, name=name, grid=(nb,), in_specs=[row, row, vec], out_specs=[row, vec],
        out_shape=[jax.ShapeDtypeStruct((S, LANES), BF16), jax.ShapeDtypeStruct((1, LANES), F32)],
        scratch_shapes=[pltpu.VMEM((1, LANES), F32)], compiler_params=_params("arbitrary"),
    )(dc, fl, bf)


def _attn_logits(q_ref, aq, k_ref, ak, masked, T):
    qf = jnp.concatenate([q_ref[...], aq], axis=1)
    kf = jnp.concatenate([k_ref[...], ak], axis=1)
    s = _dot(qf, kf, NT)
    if masked:
        r = lax.broadcasted_iota(jnp.int32, (T, T), 0)
        c = lax.broadcasted_iota(jnp.int32, (T, T), 1)
        s = jnp.where(r >= c, s, NEG_INF)
    return s, qf, kf


def _causal_pairs(nb, q_major):
    pairs = ([(i, j) for i in range(nb) for j in range(i + 1)] if q_major
             else [(i, j) for j in range(nb) for i in range(j, nb)])
    return (jnp.asarray(np.array([p[0] for p in pairs], np.int32)),
            jnp.asarray(np.array([p[1] for p in pairs], np.int32)))


def _flash_fwd(q2, kn, vb, aux_k, name, side=None):
    S, D = q2.shape
    H = D // HEAD_DIM
    T = _tile(S, 512)
    nb = S // T
    q_idx, k_idx = _causal_pairs(nb, True)

    def body(qi_ref, kj_ref, q_ref, k_ref, v_ref, ak_ref, o_ref, aq_ref, m_s, acc_s):
        t = pl.program_id(1)
        i = qi_ref[t]
        j = kj_ref[t]
        lane = lax.broadcasted_iota(jnp.int32, (T, HEAD_DIM), 1)

        @pl.when(j == 0)
        def _():
            m_s[...] = jnp.full_like(m_s, NEG_INF)
            acc_s[...] = jnp.zeros_like(acc_s)

        def step(masked):
            s, _, _ = _attn_logits(q_ref, _aux_block(lane, ones_at=AUX_A), k_ref, ak_ref[...], masked, T)
            m_prev = m_s[...]
            m_new = jnp.maximum(m_prev, jnp.max(s, axis=1, keepdims=True))
            p = jnp.exp2(s - jnp.tile(m_new, (1, T // HEAD_DIM)))
            alpha = jnp.exp2(m_prev - m_new)
            vf = jnp.concatenate([v_ref[...], jnp.ones((T, HEAD_DIM), BF16)], axis=1)
            acc_s[...] = jnp.tile(alpha, (1, 2)) * acc_s[...] + _dot(p.astype(BF16), vf, NN)
            m_s[...] = m_new

        @pl.when(j < i)
        def _():
            step(False)

        @pl.when(j == i)
        def _():
            step(True)
            l = acc_s[:, HEAD_DIM:]
            o_ref[...] = (acc_s[:, :HEAD_DIM] / l).astype(BF16)
            aq_ref[...] = _aux_block(lane, m_s[...] + jnp.log2(l), AUX_B, AUX_A)

    qspec = pl.BlockSpec((T, HEAD_DIM), lambda h, t, qi, kj: (qi[t], h))
    kspec = pl.BlockSpec((T, HEAD_DIM), lambda h, t, qi, kj: (kj[t], h))
    grid = (H, int(q_idx.shape[0]))
    operands = [q_idx, k_idx, q2, kn, vb, aux_k]
    in_specs, out_specs = [qspec, kspec, kspec, kspec], [qspec, qspec]
    out_shapes = [jax.ShapeDtypeStruct((S, D), BF16), jax.ShapeDtypeStruct((S, D), BF16)]
    scratch = [pltpu.VMEM((T, HEAD_DIM), F32), pltpu.VMEM((T, 2 * HEAD_DIM), F32)]
    sem = ("parallel", "arbitrary")
    if side is not None:
        body = _host(side, body, grid, 6, 2, 2)
        operands, in_specs = operands + side.operands, in_specs + [ANY] * len(side.operands)
        out_specs, out_shapes = out_specs + [ANY] * len(side.out_shapes), out_shapes + side.out_shapes
        scratch = scratch + side.sem_shapes
        sem = ("arbitrary", "arbitrary")
    grid_spec = pltpu.PrefetchScalarGridSpec(
        num_scalar_prefetch=2, grid=grid, in_specs=in_specs, out_specs=out_specs, scratch_shapes=scratch)
    outs = pl.pallas_call(
        body, name=name, grid_spec=grid_spec, out_shape=out_shapes, compiler_params=_params(*sem),
    )(*operands)
    return outs[0], outs[1], list(outs[2:])


def _attn_delta(do, o, name):
    S, D = do.shape
    H = D // HEAD_DIM
    tm = _tile(S, 256, 16)

    def body(do_ref, o_ref, ad_ref, dob_ref):
        lane = lax.broadcasted_iota(jnp.int32, (tm, HEAD_DIM), 1)
        for h in range(H):
            cols = slice(h * HEAD_DIM, (h + 1) * HEAD_DIM)
            delta = jnp.sum(do_ref[:, cols] * o_ref[:, cols].astype(F32), axis=1, keepdims=True)
            ad_ref[:, cols] = _aux_block(lane, jnp.broadcast_to(delta, (tm, HEAD_DIM)), AUX_A)
        dob_ref[...] = do_ref[...].astype(BF16)

    row = pl.BlockSpec((tm, D), lambda i: (i, 0))
    return pl.pallas_call(
        body, name=name, grid=(S // tm,), in_specs=[row, row], out_specs=[row, row],
        out_shape=[jax.ShapeDtypeStruct((S, D), BF16), jax.ShapeDtypeStruct((S, D), BF16)],
        compiler_params=_params("parallel"),
    )(do, o)


def _flash_bwd(q2, kn, vb, dob, aux_q, aux_k, aux_do, name):
    S, D = q2.shape
    H = D // HEAD_DIM
    T = _tile(S, 512)
    nb = S // T
    q_idx, k_idx = _causal_pairs(nb, False)

    def body(qi_ref, kj_ref, q_ref, k_ref, v_ref, do_ref, aq_ref, ak_ref, ad_ref,
             dqx_ref, dkx_ref, dv_ref, dv_acc):
        t = pl.program_id(1)
        i = qi_ref[t]
        j = kj_ref[t]
        lane = lax.broadcasted_iota(jnp.int32, (T, HEAD_DIM), 1)

        @pl.when(t == 0)
        def _():
            dqx_ref[...] = jnp.zeros_like(dqx_ref)

        @pl.when(i == j)
        def _():
            dkx_ref[...] = jnp.zeros_like(dkx_ref)
            dv_acc[...] = jnp.zeros_like(dv_acc)

        def step(masked):
            s, qf, kf = _attn_logits(q_ref, aq_ref[...], k_ref, ak_ref[...], masked, T)
            p = jnp.exp2(s)
            dof = jnp.concatenate([do_ref[...], ad_ref[...]], axis=1)
            vf = jnp.concatenate([v_ref[...], _aux_block(lane, ones_at=AUX_A)], axis=1)
            ds = p * _dot(dof, vf, NT)
            dsb = ds.astype(BF16)
            dv_acc[...] += _dot(p.astype(BF16), do_ref[...], TN)
            dkx_ref[...] += _dot(dsb, qf, TN)
            rows = pl.ds(pl.multiple_of(i * T, T), T)
            dqx_ref[rows, :] += _dot(dsb, kf, NN)

        @pl.when(i > j)
        def _():
            step(False)

        @pl.when(i == j)
        def _():
            step(True)

        @pl.when(i == nb - 1)
        def _():
            dv_ref[...] = dv_acc[...].astype(BF16)

    qspec = pl.BlockSpec((T, HEAD_DIM), lambda h, t, qi, kj: (qi[t], h))
    kspec = pl.BlockSpec((T, HEAD_DIM), lambda h, t, qi, kj: (kj[t], h))
    grid_spec = pltpu.PrefetchScalarGridSpec(
        num_scalar_prefetch=2, grid=(H, q_idx.shape[0]),
        in_specs=[qspec, kspec, kspec, qspec, qspec, kspec, qspec],
        out_specs=[pl.BlockSpec((S, 2 * HEAD_DIM), lambda h, t, qi, kj: (0, h)),
                   pl.BlockSpec((T, 2 * HEAD_DIM), lambda h, t, qi, kj: (kj[t], h)), kspec],
        scratch_shapes=[pltpu.VMEM((T, HEAD_DIM), F32)])
    return pl.pallas_call(
        body, name=name, grid_spec=grid_spec,
        out_shape=[jax.ShapeDtypeStruct((S, 2 * D), F32), jax.ShapeDtypeStruct((S, 2 * D), F32),
                   jax.ShapeDtypeStruct((S, D), BF16)],
        compiler_params=_params("parallel", "arbitrary"),
    )(q_idx, k_idx, q2, kn, vb, dob, aux_q, aux_k, aux_do)


def _pool_counts(first_row, tm, win):
    t = first_row + lax.broadcasted_iota(jnp.int32, (tm, 1), 0)
    return jnp.minimum(t + 1, win).astype(F32)


def _pool_fwd(h, x, wp, b, scale, name):
    S, D = h.shape
    G = D // len(POOL_WINDOWS)
    tm = _tile(S, 256, 16)

    def body(h_ref, halo_ref, x_ref, w_ref, b_ref, s_ref, y_ref, o_ref, ext):
        i = pl.program_id(0)
        ext[POOL_HALO:, :] = h_ref[...]

        @pl.when(i == 0)
        def _():
            ext[0:POOL_HALO, :] = jnp.zeros((POOL_HALO, D), F32)

        @pl.when(i > 0)
        def _():
            ext[0:POOL_HALO, :] = halo_ref[...]

        for g, win in enumerate(POOL_WINDOWS):
            cols = slice(g * G, (g + 1) * G)
            tot = ext[POOL_HALO:POOL_HALO + tm, cols]
            for k in range(1, win):
                tot = tot + ext[POOL_HALO - k:POOL_HALO - k + tm, cols]
            y = (tot / _pool_counts(i * tm, tm, win) - h_ref[:, cols]).astype(BF16)
            y_ref[:, cols] = y
            z = _dot(y, w_ref[g], NN)
            o_ref[:, cols] = x_ref[:, cols] + (z + b_ref[:, cols]) * s_ref[:, cols]

    row = pl.BlockSpec((tm, D), lambda i: (i, 0))
    vec = pl.BlockSpec((1, D), lambda i: (0, 0))
    halo = pl.BlockSpec((POOL_HALO, D), lambda i: (jnp.maximum(i * (tm // POOL_HALO) - 1, 0), 0))
    return pl.pallas_call(
        body, name=name, grid=(S // tm,),
        in_specs=[row, halo, row, pl.BlockSpec((len(POOL_WINDOWS), G, G), lambda i: (0, 0, 0)), vec, vec],
        out_specs=[row, row],
        out_shape=[jax.ShapeDtypeStruct((S, D), BF16), jax.ShapeDtypeStruct((S, D), F32)],
        scratch_shapes=[pltpu.VMEM((tm + POOL_HALO, D), F32)], compiler_params=_params("parallel"),
    )(h, h, x, wp, b, scale)


def _pool_bwd_mix(dout, yb, wp, b, scale, name):
    S, D = dout.shape
    NG = len(POOL_WINDOWS)
    G = D // NG
    tm = _tile(S, 256, 16)

    def body(do_ref, y_ref, w_ref, b_ref, s_ref, dyc_ref, dw_ref, db_ref, ds_ref):
        i = pl.program_id(0)

        @pl.when(i == 0)
        def _():
            dw_ref[...] = jnp.zeros_like(dw_ref)
            db_ref[...] = jnp.zeros_like(db_ref)
            ds_ref[...] = jnp.zeros_like(ds_ref)

        for g, win in enumerate(POOL_WINDOWS):
            cols = slice(g * G, (g + 1) * G)
            y = y_ref[:, cols]
            dz = do_ref[:, cols]
            zb = _dot(y, w_ref[g], NN) + b_ref[:, cols]
            ds_ref[:, cols] += jnp.sum(dz * zb, axis=0, keepdims=True)
            dzb = dz * s_ref[:, cols]
            db_ref[:, cols] += jnp.sum(dzb, axis=0, keepdims=True)
            dzb16 = dzb.astype(BF16)
            dw_ref[g] += _dot(y, dzb16, TN)
            dy = _dot(dzb16, w_ref[g], NT)
            dyc_ref[:, cols] = dy / _pool_counts(i * tm, tm, win)

    row = pl.BlockSpec((tm, D), lambda i: (i, 0))
    vec = pl.BlockSpec((1, D), lambda i: (0, 0))
    wspec = pl.BlockSpec((NG, G, G), lambda i: (0, 0, 0))
    return pl.pallas_call(
        body, name=name, grid=(S // tm,), in_specs=[row, row, wspec, vec, vec],
        out_specs=[row, wspec, vec, vec],
        out_shape=[jax.ShapeDtypeStruct((S, D), F32), jax.ShapeDtypeStruct((NG, G, G), F32),
                   jax.ShapeDtypeStruct((1, D), F32), jax.ShapeDtypeStruct((1, D), F32)],
        compiler_params=_params("arbitrary"),
    )(dout, yb, wp, b, scale)


def _pool_bwd_window(dyc, name):
    S, D = dyc.shape
    G = D // len(POOL_WINDOWS)
    tm = _tile(S, 256, 16)
    nb = S // tm

    def body(d_ref, halo_ref, dh_ref, ext):
        i = pl.program_id(0)
        ext[0:tm, :] = d_ref[...]

        @pl.when(i == nb - 1)
        def _():
            ext[tm:tm + POOL_HALO, :] = jnp.zeros((POOL_HALO, D), F32)

        @pl.when(i < nb - 1)
        def _():
            ext[tm:tm + POOL_HALO, :] = halo_ref[...]

        for g, win in enumerate(POOL_WINDOWS):
            cols = slice(g * G, (g + 1) * G)
            tot = ext[0:tm, cols] * (1.0 - _pool_counts(i * tm, tm, win))
            for k in range(1, win):
                tot = tot + ext[k:k + tm, cols]
            dh_ref[:, cols] = tot

    row = pl.BlockSpec((tm, D), lambda i: (i, 0))
    halo = pl.BlockSpec((POOL_HALO, D),
                        lambda i: (jnp.minimum((i + 1) * (tm // POOL_HALO), S // POOL_HALO - 1), 0))
    return pl.pallas_call(
        body, name=name, grid=(nb,), in_specs=[row, halo], out_specs=row,
        out_shape=jax.ShapeDtypeStruct((S, D), F32),
        scratch_shapes=[pltpu.VMEM((tm + POOL_HALO, D), F32)], compiler_params=_params("parallel"),
    )(dyc, dyc)


def _adamw(w, g, m, v, name):
    R, C = w.shape
    tr = _row_tile(R, C, 4, 1 << 20)

    def body(w_ref, g_ref, m_ref, v_ref, d_ref, nm_ref, nv_ref):
        gv = g_ref[...]
        m_new = ADAM_B1 * m_ref[...] + (1.0 - ADAM_B1) * gv
        v_new = ADAM_B2 * v_ref[...] + (1.0 - ADAM_B2) * (gv * gv)
        m_hat = m_new / (1.0 - ADAM_B1 ** ADAM_STEP)
        v_hat = v_new / (1.0 - ADAM_B2 ** ADAM_STEP)
        d_ref[...] = -ADAM_LR * (m_hat / (jnp.sqrt(v_hat) + ADAM_EPS) + ADAM_WD * w_ref[...])
        nm_ref[...] = m_new
        nv_ref[...] = v_new

    row = pl.BlockSpec((tr, C), lambda i: (i, 0))
    return pl.pallas_call(
        body, name=name, grid=(R // tr,), in_specs=[row] * 4, out_specs=[row] * 3,
        out_shape=[jax.ShapeDtypeStruct((R, C), F32)] * 3, compiler_params=_params("parallel"),
    )(w, g, m, v)


def _sum_core_halves(g, r1, c_idx, name):
    _, _, Rh, C = g.shape
    tr = _row_tile(Rh, C, 4, 2 << 20)

    def body(c_ref, g_ref, r_ref, o_ref, ob_ref):
        total = g_ref[...] + r_ref[...]
        o_ref[...] = total
        ob_ref[...] = total.astype(BF16)

    piece = pl.BlockSpec((None, tr, C), lambda s, r, c_ref: (s, r, 0))
    grid_spec = pltpu.PrefetchScalarGridSpec(
        num_scalar_prefetch=1, grid=(N_CHIPS, Rh // tr),
        in_specs=[pl.BlockSpec((None, None, tr, C), lambda s, r, c_ref: (s, c_ref[0], r, 0)), piece],
        out_specs=[piece, piece])
    return pl.pallas_call(
        body, name=name, grid_spec=grid_spec,
        out_shape=[jax.ShapeDtypeStruct((N_CHIPS, Rh, C), F32), jax.ShapeDtypeStruct((N_CHIPS, Rh, C), BF16)],
        compiler_params=_params("parallel", "parallel"),
    )(c_idx, g, r1)


def _sum_chips(h, r2, place, name):
    _, Rh, C = h.shape
    tr = _row_tile(Rh, C, 4, 1 << 20)

    def body(place_ref, h_ref, r_ref, o_ref):
        o_ref[...] = ((h_ref[...] + r_ref[0].astype(F32)) + r_ref[1].astype(F32)) + r_ref[2].astype(F32)

    grid_spec = pltpu.PrefetchScalarGridSpec(
        num_scalar_prefetch=1, grid=(Rh // tr,),
        in_specs=[pl.BlockSpec((None, tr, C), lambda r, pr: (pr[0], r, 0)),
                  pl.BlockSpec((N_CHIPS - 1, tr, C), lambda r, pr: (0, r, 0))],
        out_specs=pl.BlockSpec((None, tr, C), lambda r, pr: (pr[1], r, 0)))
    return pl.pallas_call(
        body, name=name, grid_spec=grid_spec, out_shape=jax.ShapeDtypeStruct((2, Rh, C), F32),
        compiler_params=_params("parallel"),
    )(place, h, r2)


ANY = pl.BlockSpec(memory_space=pl.ANY)


def _place():
    x, y, c = lax.axis_index("x"), lax.axis_index("y"), lax.axis_index("c")
    chips = [(1 - x, y), (x, 1 - y), (1 - x, 1 - y)]
    return x, y, c, chips, [2 * cx + cy for cx, cy in chips]


def _run_side(side, name):
    def body(*refs):
        _, side_refs = side.split(refs, 0, 0, 0)
        side.start(*side_refs)
        side.finish(*side_refs)

    return pl.pallas_call(
        body, name=name, in_specs=[ANY] * len(side.operands), out_specs=[ANY] * len(side.out_shapes),
        out_shape=side.out_shapes, scratch_shapes=side.sem_shapes,
    )(*side.operands)


def _allgather_side(shards):
    n = len(shards)

    def copy(sems, t, k, src, dst, to):
        return pltpu.make_async_remote_copy(src_ref=src, dst_ref=dst, send_sem=sems[0].at[t, k],
                                            recv_sem=sems[1].at[t, k], device_id=to, device_id_type=MESH)

    def first_copies(ins, outs, sems):
        x, y, c, chips, _ = _place()
        me = 2 * x + y
        return [copy(sems, t, j, ins[t].at[c], outs[t].at[me, c], (*chip, c))
                for t in range(n) for j, chip in enumerate(chips)]

    def start(ins, outs, sems):
        for cp in first_copies(ins, outs, sems):
            cp.start()

    def finish(ins, outs, sems):
        x, y, c, chips, chip_idx = _place()
        sibling = (x, y, 1 - c)
        passed = []
        for t in range(n):
            for j, chip in enumerate(chips):
                landed = outs[t].at[chip_idx[j], c]
                copy(sems, t, j, landed, landed, (*chip, c)).wait_recv()
                fw = copy(sems, t, 3 + j, landed, landed, sibling)
                fw.start()
                passed.append(fw)
        for t in range(n):
            for j in range(3):
                other = outs[t].at[chip_idx[j], 1 - c]
                copy(sems, t, 3 + j, other, other, sibling).wait_recv()
        for cp in first_copies(ins, outs, sems) + passed:
            cp.wait_send()

    return _Side(shards, [jax.ShapeDtypeStruct((N_CHIPS,) + s.shape, s.dtype) for s in shards],
                 [pltpu.SemaphoreType.DMA((n, 6)), pltpu.SemaphoreType.DMA((n, 6))], start, finish)


def _send_other_half_to_sibling(gs, name):
    n = len(gs)

    def body(*refs):
        ins, outs = refs[:n], refs[n:2 * n]
        send_sems, recv_sems = refs[2 * n:]
        x, y, c, _, _ = _place()
        copies = []
        for t in range(n):
            for s in range(N_CHIPS):
                cp = pltpu.make_async_remote_copy(
                    src_ref=ins[t].at[s, 1 - c], dst_ref=outs[t].at[s], send_sem=send_sems.at[t, s],
                    recv_sem=recv_sems.at[t, s], device_id=(x, y, 1 - c), device_id_type=MESH)
                cp.start()
                copies.append(cp)
        for cp in copies:
            cp.wait()

    return pl.pallas_call(
        body, name=name, in_specs=[ANY] * n, out_specs=[ANY] * n,
        out_shape=[jax.ShapeDtypeStruct((N_CHIPS,) + g.shape[2:], g.dtype) for g in gs],
        scratch_shapes=[pltpu.SemaphoreType.DMA((n, N_CHIPS)), pltpu.SemaphoreType.DMA((n, N_CHIPS))],
    )(*gs)


def _exchange_side(hs):
    n = len(hs)

    def copies(ins, outs, sems):
        x, y, c, chips, chip_idx = _place()
        return [pltpu.make_async_remote_copy(
            src_ref=ins[t].at[chip_idx[j]], dst_ref=outs[t].at[j], send_sem=sems[0].at[t, j],
            recv_sem=sems[1].at[t, j], device_id=(*chip, c), device_id_type=MESH)
            for t in range(n) for j, chip in enumerate(chips)]

    def start(ins, outs, sems):
        for cp in copies(ins, outs, sems):
            cp.start()

    def finish(ins, outs, sems):
        for cp in copies(ins, outs, sems):
            cp.wait()

    return _Side(hs, [jax.ShapeDtypeStruct((N_CHIPS - 1,) + h.shape[1:], h.dtype) for h in hs],
                 [pltpu.SemaphoreType.DMA((n, 3)), pltpu.SemaphoreType.DMA((n, 3))], start, finish)


def _join_core_halves(bufs, name):
    n = len(bufs)

    def body(*refs):
        outs = refs[n:2 * n]
        send_sems, recv_sems = refs[2 * n:]
        x, y, c, _, _ = _place()
        sends = []
        for t in range(n):
            cp = pltpu.make_async_remote_copy(
                src_ref=outs[t].at[c], dst_ref=outs[t].at[c], send_sem=send_sems.at[t], recv_sem=recv_sems.at[t],
                device_id=(x, y, 1 - c), device_id_type=MESH)
            cp.start()
            sends.append(cp)
        for t in range(n):
            other = outs[t].at[1 - c]
            pltpu.make_async_remote_copy(
                src_ref=other, dst_ref=other, send_sem=send_sems.at[t], recv_sem=recv_sems.at[t],
                device_id=(x, y, 1 - c), device_id_type=MESH).wait_recv()
        for cp in sends:
            cp.wait_send()

    return pl.pallas_call(
        body, name=name, in_specs=[ANY] * n, out_specs=[ANY] * n,
        out_shape=[jax.ShapeDtypeStruct(b.shape, b.dtype) for b in bufs],
        input_output_aliases={t: t for t in range(n)},
        scratch_shapes=[pltpu.SemaphoreType.DMA((n,)), pltpu.SemaphoreType.DMA((n,))],
    )(*bufs)


def _allreduce_small(vec, name):
    R = vec.shape[0]

    def body(v_ref, o_ref, buf, send_sems, recv_sems):
        x, y, c = lax.axis_index("x"), lax.axis_index("y"), lax.axis_index("c")
        me = 4 * x + 2 * y + c
        buf[me] = v_ref[...]
        peers = []
        for k in range(1, N_DEV):
            px = 1 - x if k & 4 else x
            py = 1 - y if k & 2 else y
            pc = 1 - c if k & 1 else c
            peers.append(((px, py, pc), 4 * px + 2 * py + pc))
        sends = []
        for k, (peer, _) in enumerate(peers):
            cp = pltpu.make_async_remote_copy(
                src_ref=v_ref, dst_ref=buf.at[me], send_sem=send_sems.at[k], recv_sem=recv_sems.at[k],
                device_id=peer, device_id_type=MESH)
            cp.start()
            sends.append(cp)
        for k, (peer, idx) in enumerate(peers):
            pltpu.make_async_remote_copy(
                src_ref=v_ref, dst_ref=buf.at[idx], send_sem=send_sems.at[k], recv_sem=recv_sems.at[k],
                device_id=peer, device_id_type=MESH).wait_recv()
        for cp in sends:
            cp.wait_send()
        total = buf[0]
        for d in range(1, N_DEV):
            total = total + buf[d]
        o_ref[...] = total

    vm = pl.BlockSpec(memory_space=pltpu.VMEM)
    return pl.pallas_call(
        body, name=name, in_specs=[vm], out_specs=vm, out_shape=jax.ShapeDtypeStruct((R, LANES), F32),
        scratch_shapes=[pltpu.VMEM((N_DEV, R, LANES), F32), pltpu.SemaphoreType.DMA((N_DEV - 1,)),
                        pltpu.SemaphoreType.DMA((N_DEV - 1,))],
    )(vec)


def _halves(a):
    lead = 1
    for d in a.shape[:-1]:
        lead *= d
    return a.reshape(2, lead // 2, a.shape[-1])


def _cols_from_shards(g):
    return g.transpose(1, 0, 2).reshape(g.shape[1], N_CHIPS * g.shape[2])


def _shards_from_cols(w):
    return w.reshape(w.shape[0], N_CHIPS, w.shape[1] // N_CHIPS).transpose(1, 0, 2)


def _pack(parts, rows):
    flat = jnp.concatenate([p.reshape(-1).astype(F32) for p in parts])
    return jnp.pad(flat, (0, rows * LANES - flat.shape[0])).reshape(rows, LANES)


def _unpack(packed, shapes):
    flat = packed.reshape(-1)
    out, off = [], 0
    for s in shapes:
        n = 1
        for d in s:
            n *= d
        out.append(flat[off:off + n].reshape(s))
        off += n
    return out


def _packed_rows(shapes):
    n = 0
    for s in shapes:
        k = 1
        for d in s:
            k *= d
        n += k
    return -(-n // (8 * LANES)) * 8


def _pad_lanes(a):
    return jnp.pad(a, ((0, 0), (0, LANES - a.shape[1])))


def kernel(x, mix_norm_g, ffn_norm_g, fox_w_in, fox_b_f, fox_q_norm_g, fox_k_norm_g, fox_w_out, pool_w, pool_b, pool_scale, ffn_w_gate_up, ffn_w_down, loss_target, m_mix_norm_g, m_ffn_norm_g, m_fox_w_in, m_fox_b_f, m_fox_q_norm_g, m_fox_k_norm_g, m_fox_w_out, m_pool_w, m_pool_b, m_pool_scale, m_ffn_w_gate_up, m_ffn_w_down, v_mix_norm_g, v_ffn_norm_g, v_fox_w_in, v_fox_b_f, v_fox_q_norm_g, v_fox_k_norm_g, v_fox_w_out, v_pool_w, v_pool_b, v_pool_scale, v_ffn_w_gate_up, v_ffn_w_down):
    _, S, D = x.shape
    H = D // HEAD_DIM
    depth = mix_norm_g.shape[0]
    n_pool = pool_w.shape[0]
    NG = len(POOL_WINDOWS)
    G = D // NG
    F = ffn_w_down.shape[1] * N_CHIPS
    ax, ay, ac = lax.axis_index("x"), lax.axis_index("y"), lax.axis_index("c")
    chip = 2 * ax + ay
    c_idx = jnp.reshape(ac, (1,)).astype(jnp.int32)
    place = jnp.stack([chip, ac]).astype(jnp.int32)
    xs = x[0]
    target = loss_target[0]

    Dq = D // N_CHIPS
    small_fwd_shapes = [(n_pool, D), (n_pool, D)]
    placed = []
    for p in (pool_b, pool_scale):
        full = lax.dynamic_update_slice(jnp.zeros((n_pool, D), F32), p, (0, chip * Dq))
        placed.append(jnp.where(ac == 0, full, jnp.zeros_like(full)))
    pool_b_full, pool_scale_full = _unpack(
        _allreduce_small(_pack(placed, _packed_rows(small_fwd_shapes)), "gather_pool_vectors"), small_fwd_shapes)

    def weight_shards(i):
        mixer = ([fox_w_in[i // 2], fox_w_out[i // 2]] if i % 2 == 0 else [pool_w[i // 2]])
        return [_halves(w.astype(BF16)) for w in mixer + [ffn_w_gate_up[i], ffn_w_down[i]]]

    def layer_weights(i, shards, got):
        got = [lax.dynamic_update_slice(g, sh[None], (chip, 0, 0, 0)) for g, sh in zip(got, shards)]
        lw = {}
        if i % 2 == 0:
            w_in = _cols_from_shards(got[0].reshape(N_CHIPS, D, -1))
            lw["w_qkv"] = w_in[:, :3 * D]
            lw["w_f"] = _pad_lanes(w_in[:, 3 * D:])
            lw["w_out"] = got[1].reshape(D, D)
        else:
            lw["w_pool"] = got[0].reshape(N_CHIPS, NG, G // N_CHIPS, G).transpose(1, 0, 2, 3).reshape(NG, G, G)
        lw["w_gu"] = _cols_from_shards(got[-2].reshape(N_CHIPS, D, -1))
        lw["w_down"] = got[-1].reshape(F, D)
        return lw

    shards0 = weight_shards(0)
    weights = [layer_weights(0, shards0, _run_side(_allgather_side(shards0), "allgather_weights_first"))]

    saved = []
    cur = xs
    for i in range(depth):
        j = i // 2
        lw = weights[i]
        sv = {"x": cur}
        next_shards = weight_shards(i + 1) if i + 1 < depth else None
        gather_next = _allgather_side(next_shards) if next_shards is not None else None
        g_mix = mix_norm_g[i][None]
        if i % 2 == 0:
            h1b = _rmsnorm_fwd(cur, g_mix, BF16, "rmsnorm_fwd_bf16")
            proj = _matmul(h1b, lw["w_qkv"], "nn", F32, "qkv_proj", M=S, N=3 * D, K=D)
            fl = _matmul(h1b, lw["w_f"], "nn", F32, "forget_proj", M=S, N=LANES, K=D)
            bf = _pad_lanes(fox_b_f[j][None])
            cum = _fgate_fwd(fl, bf, "forget_cumsum")
            q2, kn, vb, aux_k = _qkv_post(proj, fox_q_norm_g[j][None], fox_k_norm_g[j][None], cum, "qk_norm")
            o, aux_q, got = _flash_fwd(q2, kn, vb, aux_k, "fox_attention_fwd", side=gather_next)
            gather_next = None
            x1 = _matmul(o, lw["w_out"], "nn", F32, "attn_out_proj", M=S, N=D, K=D, res=cur)
            sv.update(h1b=h1b, proj=proj, fl=fl, bf=bf, q2=q2, kn=kn, vb=vb, aux_k=aux_k, aux_q=aux_q, o=o)
        else:
            h1 = _rmsnorm_fwd(cur, g_mix, F32, "rmsnorm_fwd_f32")
            yb, x1 = _pool_fwd(h1, cur, lw["w_pool"], pool_b_full[j][None], pool_scale_full[j][None], "pool_fwd")
            sv.update(yb=yb)
        h2b = _rmsnorm_fwd(x1, ffn_norm_g[i][None], BF16, "rmsnorm_fwd_bf16")
        if gather_next is not None:
            gu, act, got = _ffn_up(h2b, lw["w_gu"], "ffn_gate_up_gather", side=gather_next)
        else:
            gu, act, _ = _ffn_up(h2b, lw["w_gu"], "ffn_gate_up")
        if next_shards is not None:
            weights.append(layer_weights(i + 1, next_shards, got))
        x2 = _matmul(act, lw["w_down"], "nn", F32, "ffn_down", M=S, N=D, K=F, tk=2816, res=x1)
        sv.update(x1=x1, h2b=h2b, gu=gu, act=act)
        saved.append(sv)
        cur = x2

    dcur, dcur_b, sq = _loss_grad(cur, target, "loss_grad")
    loss = lax.psum(sq[0, 0] * (0.5 / D), ("x", "y", "c"))

    g_mix_rows, g_ffn_rows = [None] * depth, [None] * depth
    g_bf, g_gq, g_gk = [None] * (depth - n_pool), [None] * (depth - n_pool), [None] * (depth - n_pool)
    g_pb, g_ps = [None] * n_pool, [None] * n_pool
    reduced = [None] * depth

    def finish_reduce(layer, hs, r2):
        rs = [_sum_chips(h, r, place, "sum_chips") for h, r in zip(hs, r2)]
        reduced[layer] = _join_core_halves(rs, f"grads_join_{'fox' if layer % 2 == 0 else 'pool'}")

    pending = None
    for i in reversed(range(depth)):
        j = i // 2
        lw, sv = weights[i], saved[i]
        dgu = _ffn_act_bwd(dcur_b, lw["w_down"], sv["gu"], "ffn_act_bwd")
        d_w_down = _matmul(sv["act"], dcur_b, "tn", F32, "ffn_down_dw", M=F, N=D, K=S, tm=1408)
        dh2 = _matmul(
            dgu, lw["w_gu"], "nt", F32, "ffn_up_dx", M=S, N=D, K=2 * F, tk=_tile(F, 2816),
            a_spec=lambda tm, tn, tk: pl.BlockSpec(
                (None, tm, tk), lambda i_, j_, k_: (k_ // (F // tk), i_, k_ % (F // tk))))
        d_w_gu = _matmul(
            sv["h2b"], dgu, "tn", F32, "ffn_up_dw" if pending is None else "ffn_up_dw_exchange",
            M=D, N=2 * F, K=S, tn=_tile(F, 1408), out_shards=N_CHIPS,
            b_spec=lambda tm, tn, tk: pl.BlockSpec(
                (None, tk, tn), lambda i_, j_, k_: (j_ // (F // tn), k_, j_ % (F // tn))),
            side=None if pending is None else _exchange_side(pending[2]))
        if pending is not None:
            d_w_gu, r2 = d_w_gu
            finish_reduce(pending[0], pending[1], r2)
        dx1, dx1b, g_ffn_rows[i] = _rmsnorm_bwd(sv["x1"], ffn_norm_g[i][None], dh2, dcur, "rmsnorm_bwd")
        g_mix = mix_norm_g[i][None]
        if i % 2 == 0:
            do = _matmul(dx1b, lw["w_out"], "nt", F32, "attn_out_dx", M=S, N=D, K=D)
            d_w_out = _matmul(sv["o"], dx1b, "tn", F32, "attn_out_dw", M=D, N=D, K=S)
            aux_do, dob = _attn_delta(do, sv["o"], "attn_delta")
            dqx, dkx, dvb = _flash_bwd(sv["q2"], sv["kn"], sv["vb"], dob, sv["aux_q"], sv["aux_k"], aux_do,
                                       "fox_attention_bwd")
            dproj, dgq, dgk = _qkv_post_bwd(sv["proj"], dqx, dkx, dvb, fox_q_norm_g[j][None],
                                            fox_k_norm_g[j][None], "qk_norm_bwd")
            dc = _pad_lanes(dqx.reshape(S, H, 2 * HEAD_DIM)[:, :, HEAD_DIM + AUX_B]
                            - dkx.reshape(S, H, 2 * HEAD_DIM)[:, :, HEAD_DIM + AUX_A])
            dfl, dbf = _fgate_bwd(dc, sv["fl"], sv["bf"], "forget_cumsum_bwd")
            d_w_qkv = _matmul(sv["h1b"], dproj, "tn", F32, "qkv_dw", M=D, N=3 * D, K=S)
            d_w_f = _matmul(sv["h1b"], dfl, "tn", F32, "forget_dw", M=D, N=LANES, K=S)
            dh1f = _matmul(dfl, lw["w_f"], "nt", F32, "forget_dx", M=S, N=D, K=LANES)
            dh1 = _matmul(dproj, lw["w_qkv"], "nt", F32, "qkv_dx", M=S, N=D, K=3 * D, res=dh1f)
            d_w_in = jnp.concatenate([d_w_qkv, d_w_f[:, :H]], axis=1)
            big = [_shards_from_cols(d_w_in), d_w_out.reshape(N_CHIPS, D // N_CHIPS, D)]
            g_bf[j], g_gq[j], g_gk[j] = dbf[0, :H], dgq[0], dgk[0]
        else:
            dyc, d_wp, dpb, dps = _pool_bwd_mix(dx1, sv["yb"], lw["w_pool"], pool_b_full[j][None],
                                                pool_scale_full[j][None], "pool_bwd_mix")
            dh1 = _pool_bwd_window(dyc, "pool_bwd_window")
            big = [d_wp.reshape(NG, N_CHIPS, G // N_CHIPS, G).transpose(1, 0, 2, 3)]
            g_pb[j], g_ps[j] = dpb[0], dps[0]
        big += [d_w_gu, d_w_down.reshape(N_CHIPS, F // N_CHIPS, D)]
        dcur, dcur_b, g_mix_rows[i] = _rmsnorm_bwd(sv["x"], g_mix, dh1, dx1, "rmsnorm_bwd")

        kind = "fox" if i % 2 == 0 else "pool"
        gs = [b.reshape(N_CHIPS, 2, -1, b.shape[-1]) for b in big]
        r1 = _send_other_half_to_sibling(gs, f"grads_to_sibling_{kind}")
        sums = [_sum_core_halves(g, r, c_idx, "sum_core_halves") for g, r in zip(gs, r1)]
        pending = (i, [s[0] for s in sums], [s[1] for s in sums])
    finish_reduce(pending[0], pending[1], _run_side(_exchange_side(pending[2]), "grads_to_chips_last"))

    small_shapes = [(depth, D), (depth, D), fox_b_f.shape, fox_q_norm_g.shape, fox_k_norm_g.shape,
                    (n_pool, D), (n_pool, D)]
    small = [jnp.stack(g_mix_rows)[:, 0], jnp.stack(g_ffn_rows)[:, 0], jnp.stack(g_bf), jnp.stack(g_gq),
             jnp.stack(g_gk), jnp.stack(g_pb), jnp.stack(g_ps)]
    (gr_mix, gr_ffn, gr_bf, gr_gq, gr_gk, gr_pb_full, gr_ps_full) = _unpack(
        _allreduce_small(_pack(small, _packed_rows(small_shapes)), "allreduce_small_grads"), small_shapes)
    gr_pb = lax.dynamic_slice(gr_pb_full, (0, chip * Dq), (n_pool, Dq))
    gr_ps = lax.dynamic_slice(gr_ps_full, (0, chip * Dq), (n_pool, Dq))

    fox_layers = [i for i in range(depth) if i % 2 == 0]
    pool_layers = [i for i in range(depth) if i % 2 == 1]
    gr_w_in = jnp.stack([reduced[i][0].reshape(fox_w_in.shape[1:]) for i in fox_layers])
    gr_w_out = jnp.stack([reduced[i][1].reshape(fox_w_out.shape[1:]) for i in fox_layers])
    gr_pool_w = jnp.stack([reduced[i][0].reshape(pool_w.shape[1:]) for i in pool_layers])
    gr_gu = jnp.stack([reduced[i][-2].reshape(ffn_w_gate_up.shape[1:]) for i in range(depth)])
    gr_down = jnp.stack([reduced[i][-1].reshape(ffn_w_down.shape[1:]) for i in range(depth)])

    def update_big(w, g, m, v, name):
        flat = lambda a: a.reshape(-1, a.shape[-1])
        return [o.reshape(w.shape) for o in _adamw(flat(w), flat(g), flat(m), flat(v), name)]

    names = ["mix_norm_g", "ffn_norm_g", "fox_w_in", "fox_b_f", "fox_q_norm_g", "fox_k_norm_g", "fox_w_out",
             "pool_w", "pool_b", "pool_scale", "ffn_w_gate_up", "ffn_w_down"]
    ws = dict(zip(names, [mix_norm_g, ffn_norm_g, fox_w_in, fox_b_f, fox_q_norm_g, fox_k_norm_g, fox_w_out,
                          pool_w, pool_b, pool_scale, ffn_w_gate_up, ffn_w_down]))
    ms = dict(zip(names, [m_mix_norm_g, m_ffn_norm_g, m_fox_w_in, m_fox_b_f, m_fox_q_norm_g, m_fox_k_norm_g,
                          m_fox_w_out, m_pool_w, m_pool_b, m_pool_scale, m_ffn_w_gate_up, m_ffn_w_down]))
    vs = dict(zip(names, [v_mix_norm_g, v_ffn_norm_g, v_fox_w_in, v_fox_b_f, v_fox_q_norm_g, v_fox_k_norm_g,
                          v_fox_w_out, v_pool_w, v_pool_b, v_pool_scale, v_ffn_w_gate_up, v_ffn_w_down]))
    grads = dict(mix_norm_g=gr_mix, ffn_norm_g=gr_ffn, fox_w_in=gr_w_in, fox_b_f=gr_bf, fox_q_norm_g=gr_gq,
                 fox_k_norm_g=gr_gk, fox_w_out=gr_w_out, pool_w=gr_pool_w, pool_b=gr_pb, pool_scale=gr_ps,
                 ffn_w_gate_up=gr_gu, ffn_w_down=gr_down)
    big_names = ["fox_w_in", "fox_w_out", "pool_w", "ffn_w_gate_up", "ffn_w_down"]
    small_names = [n for n in names if n not in big_names]
    delta, new_m, new_v = {}, {}, {}
    for n in big_names:
        delta[n], new_m[n], new_v[n] = update_big(ws[n], grads[n], ms[n], vs[n], "adamw_" + n)
    shapes = [ws[n].shape for n in small_names]
    rows = _packed_rows(shapes)
    packed = _adamw(_pack([ws[n] for n in small_names], rows), _pack([grads[n] for n in small_names], rows),
                    _pack([ms[n] for n in small_names], rows), _pack([vs[n] for n in small_names], rows),
                    "adamw_small")
    for dst, pk in zip((delta, new_m, new_v), packed):
        for n, a in zip(small_names, _unpack(pk, shapes)):
            dst[n] = a

    return (loss, dcur[None], *[grads[n] for n in names], *[delta[n] for n in names],
            *[new_m[n] for n in names], *[new_v[n] for n in names])
```

```python
import functools

import numpy as np
import jax
import jax.numpy as jnp
from jax import lax
from jax.experimental import pallas as pl
from jax.experimental.pallas import tpu as pltpu

F32 = jnp.float32
BF16 = jnp.bfloat16
MESH = pl.DeviceIdType.MESH

HEAD_DIM = 128
RMS_EPS = 1e-6
NEG_INF = -1e30
POOL_WINDOWS = (2, 4, 8, 16)
POOL_HALO = 16
LANES = 128
N_CHIPS = 4
N_DEV = 8
VMEM_LIMIT = 56 * 1024 * 1024

ADAM_LR = 0.001
ADAM_B1 = 0.9
ADAM_B2 = 0.999
ADAM_EPS = 1e-08
ADAM_WD = 0.01
ADAM_STEP = 10

NN = (((1,), (0,)), ((), ()))
NT = (((1,), (1,)), ((), ()))
TN = (((0,), (0,)), ((), ()))


def _params(*sem):
    return pltpu.CompilerParams(dimension_semantics=sem if sem else None, vmem_limit_bytes=VMEM_LIMIT)


def _tile(n, pref, unit=LANES):
    best = None
    t = unit
    while t <= min(n, pref):
        if n % t == 0:
            best = t
        t += unit
    return best if best is not None else n


def _row_tile(rows, cols, itemsize, budget):
    best = None
    for t in range(16, rows + 1, 16):
        if rows % t == 0 and t * cols * itemsize <= budget:
            best = t
    return best if best is not None else rows


def _dot(a, b, dims):
    return lax.dot_general(a, b, dims, preferred_element_type=F32)


class _Side:
    def __init__(self, operands, out_shapes, sem_shapes, start, finish):
        self.operands, self.out_shapes, self.sem_shapes = list(operands), list(out_shapes), list(sem_shapes)
        self.start, self.finish = start, finish

    def split(self, refs, n_in, n_out, n_scratch):
        si, so, ss = len(self.operands), len(self.out_shapes), len(self.sem_shapes)
        refs = list(refs)
        cuts = [n_in, si, n_out, so, n_scratch, ss]
        parts, at = [], 0
        for c in cuts:
            parts.append(refs[at:at + c])
            at += c
        ins, s_ins, outs, s_outs, scratch, sems = parts
        return ins + outs + scratch, (s_ins, s_outs, sems)

    def bracket(self, side_refs, first, last, compute):
        @pl.when(first)
        def _():
            self.start(*side_refs)

        compute()

        @pl.when(last)
        def _():
            self.finish(*side_refs)


def _merge_sides(sides):
    sides = [s for s in sides if s is not None]
    if len(sides) < 2:
        return sides[0] if sides else None

    def each(ins, outs, sems, what):
        i = o = m = 0
        for s in sides:
            ni, no, nm = len(s.operands), len(s.out_shapes), len(s.sem_shapes)
            getattr(s, what)(ins[i:i + ni], outs[o:o + no], sems[m:m + nm])
            i, o, m = i + ni, o + no, m + nm

    return _Side([a for s in sides for a in s.operands], [a for s in sides for a in s.out_shapes],
                 [a for s in sides for a in s.sem_shapes],
                 lambda ins, outs, sems: each(ins, outs, sems, "start"),
                 lambda ins, outs, sems: each(ins, outs, sems, "finish"))


def _grid_ends(grid):
    ids = [pl.program_id(a) for a in range(len(grid))]
    first, last = ids[0] == 0, ids[0] == grid[0] - 1
    for pid, n in zip(ids[1:], grid[1:]):
        first, last = first & (pid == 0), last & (pid == n - 1)
    return first, last


def _host(side, body, grid, n_in, n_out, n_scratch):
    if side is None:
        return body

    def hosted(*refs):
        own, side_refs = side.split(refs, n_in, n_out, n_scratch)
        first, last = _grid_ends(grid)
        side.bracket(side_refs, first, last, lambda: body(*own))

    return hosted


def _matmul(a, b, mode, out_dtype, name, *, M, N, K, tm=1024, tn=1024, tk=2048, res=None,
            a_spec=None, b_spec=None, out_shards=None, side=None):
    tm, tn, tk = _tile(M, tm), _tile(N, tn), _tile(K, tk)
    nk = K // tk
    dims = {"nn": NN, "nt": NT, "tn": TN}[mode]
    if a_spec is None:
        a_spec = (pl.BlockSpec((tk, tm), lambda i, j, k: (k, i)) if mode == "tn"
                  else pl.BlockSpec((tm, tk), lambda i, j, k: (i, k)))
    else:
        a_spec = a_spec(tm, tn, tk)
    if b_spec is None:
        b_spec = (pl.BlockSpec((tn, tk), lambda i, j, k: (j, k)) if mode == "nt"
                  else pl.BlockSpec((tk, tn), lambda i, j, k: (k, j)))
    else:
        b_spec = b_spec(tm, tn, tk)
    if out_shards is None:
        o_spec = pl.BlockSpec((tm, tn), lambda i, j, k: (i, j))
        out_shape = jax.ShapeDtypeStruct((M, N), out_dtype)
    else:
        per = N // out_shards // tn
        o_spec = pl.BlockSpec((None, tm, tn), lambda i, j, k: (j // per, i, j % per))
        out_shape = jax.ShapeDtypeStruct((out_shards, M, N // out_shards), out_dtype)
    has_res = res is not None

    def body(a_ref, b_ref, *rest):
        if has_res:
            r_ref, o_ref, acc = rest
        else:
            o_ref, acc = rest

        def finish(total):
            if has_res:
                total = total + r_ref[...]
            o_ref[...] = total.astype(o_ref.dtype)

        part = _dot(a_ref[...], b_ref[...], dims)
        if nk == 1:
            finish(part)
        else:
            k = pl.program_id(2)

            @pl.when(k == 0)
            def _():
                acc[...] = part

            @pl.when(k > 0)
            def _():
                acc[...] += part

            @pl.when(k == nk - 1)
            def _():
                finish(acc[...])

    grid = (M // tm, N // tn, nk)
    operands = [a, b] + ([res] if has_res else [])
    in_specs = [a_spec, b_spec] + ([o_spec] if has_res else [])
    out_specs, out_shapes = [o_spec], [out_shape]
    scratch = [pltpu.VMEM((tm, tn) if nk > 1 else (8, LANES), F32)]
    sem = ("parallel", "parallel", "arbitrary")
    if side is not None:
        body = _host(side, body, grid, len(operands), 1, 1)
        operands, in_specs = operands + side.operands, in_specs + [ANY] * len(side.operands)
        out_specs, out_shapes = out_specs + [ANY] * len(side.out_shapes), out_shapes + side.out_shapes
        scratch = scratch + side.sem_shapes
        sem = ("arbitrary",) * 3
    outs = pl.pallas_call(
        body, name=name, grid=grid, in_specs=in_specs, out_specs=out_specs, out_shape=out_shapes,
        scratch_shapes=scratch, compiler_params=_params(*sem),
    )(*operands)
    return outs[0] if side is None else (outs[0], list(outs[1:]))


def _rmsnorm_fwd(x, g, out_dtype, name):
    S, D = x.shape
    tm = _tile(S, 512, 16)

    def body(x_ref, g_ref, o_ref):
        xv = x_ref[...]
        r = lax.rsqrt(jnp.mean(xv * xv, axis=-1, keepdims=True) + RMS_EPS)
        o_ref[...] = ((xv * r) * g_ref[...]).astype(o_ref.dtype)

    row = pl.BlockSpec((tm, D), lambda i: (i, 0))
    return pl.pallas_call(
        body, name=name, grid=(S // tm,), in_specs=[row, pl.BlockSpec((1, D), lambda i: (0, 0))],
        out_specs=row, out_shape=jax.ShapeDtypeStruct((S, D), out_dtype),
        compiler_params=_params("parallel"),
    )(x, g)


def _rmsnorm_bwd(x, g, dh, dres, name):
    S, D = x.shape
    tm = _tile(S, 256, 16)

    def body(x_ref, g_ref, dh_ref, dres_ref, dx_ref, dxb_ref, dg_ref):
        xv = x_ref[...]
        r = lax.rsqrt(jnp.mean(xv * xv, axis=-1, keepdims=True) + RMS_EPS)
        xhat = xv * r
        dhv = dh_ref[...]
        dxhat = dhv * g_ref[...]
        dx = dres_ref[...] + r * (dxhat - xhat * jnp.mean(dxhat * xhat, axis=-1, keepdims=True))
        dx_ref[...] = dx
        dxb_ref[...] = dx.astype(BF16)

        @pl.when(pl.program_id(0) == 0)
        def _():
            dg_ref[...] = jnp.zeros_like(dg_ref)

        dg_ref[...] += jnp.sum(dhv * xhat, axis=0, keepdims=True)

    row = pl.BlockSpec((tm, D), lambda i: (i, 0))
    vec = pl.BlockSpec((1, D), lambda i: (0, 0))
    return pl.pallas_call(
        body, name=name, grid=(S // tm,), in_specs=[row, vec, row, row], out_specs=[row, row, vec],
        out_shape=[jax.ShapeDtypeStruct((S, D), F32), jax.ShapeDtypeStruct((S, D), BF16),
                   jax.ShapeDtypeStruct((1, D), F32)],
        compiler_params=_params("arbitrary"),
    )(x, g, dh, dres)


def _loss_grad(y, target, name):
    S, D = y.shape
    tm = _tile(S, 256, 16)

    def body(y_ref, t_ref, dy_ref, dyb_ref, sq_ref):
        e = y_ref[...] - t_ref[...]
        dy = e / D
        dy_ref[...] = dy
        dyb_ref[...] = dy.astype(BF16)

        @pl.when(pl.program_id(0) == 0)
        def _():
            sq_ref[...] = jnp.zeros_like(sq_ref)

        total = jnp.sum(jnp.sum(e * e, axis=1, keepdims=True), axis=0, keepdims=True)
        sq_ref[...] += jnp.broadcast_to(total, sq_ref.shape)

    row = pl.BlockSpec((tm, D), lambda i: (i, 0))
    vec = pl.BlockSpec((1, LANES), lambda i: (0, 0))
    return pl.pallas_call(
        body, name=name, grid=(S // tm,), in_specs=[row, row], out_specs=[row, row, vec],
        out_shape=[jax.ShapeDtypeStruct((S, D), F32), jax.ShapeDtypeStruct((S, D), BF16),
                   jax.ShapeDtypeStruct((1, LANES), F32)],
        compiler_params=_params("arbitrary"),
    )(y, target)


def _ffn_up(hb, w_gu, name, side=None):
    S, D = hb.shape
    F = w_gu.shape[1] // 2
    tm, tn, tk = _tile(S, 1024), _tile(F, 512), _tile(D, 2048)
    nk, nj = D // tk, F // tn

    def body(a_ref, wg_ref, wu_ref, gu_ref, act_ref, accg, accu):
        pg = _dot(a_ref[...], wg_ref[...], NN)
        pu = _dot(a_ref[...], wu_ref[...], NN)

        def finish(g, u):
            gu_ref[0] = g
            gu_ref[1] = u
            act_ref[...] = ((g / (1.0 + jnp.exp(-g))) * u).astype(BF16)

        if nk == 1:
            finish(pg, pu)
        else:
            k = pl.program_id(2)

            @pl.when(k == 0)
            def _():
                accg[...] = pg
                accu[...] = pu

            @pl.when(k > 0)
            def _():
                accg[...] += pg
                accu[...] += pu

            @pl.when(k == nk - 1)
            def _():
                finish(accg[...], accu[...])

    acc_shape = (tm, tn) if nk > 1 else (8, LANES)
    grid = (S // tm, nj, nk)
    operands = [hb, w_gu, w_gu]
    in_specs = [pl.BlockSpec((tm, tk), lambda i, j, k: (i, k)),
                pl.BlockSpec((tk, tn), lambda i, j, k: (k, j)),
                pl.BlockSpec((tk, tn), lambda i, j, k: (k, j + nj))]
    out_specs = [pl.BlockSpec((2, tm, tn), lambda i, j, k: (0, i, j)),
                 pl.BlockSpec((tm, tn), lambda i, j, k: (i, j))]
    out_shapes = [jax.ShapeDtypeStruct((2, S, F), F32), jax.ShapeDtypeStruct((S, F), BF16)]
    scratch = [pltpu.VMEM(acc_shape, F32), pltpu.VMEM(acc_shape, F32)]
    sem = ("parallel", "parallel", "arbitrary")
    if side is not None:
        body = _host(side, body, grid, 3, 2, 2)
        operands, in_specs = operands + side.operands, in_specs + [ANY] * len(side.operands)
        out_specs, out_shapes = out_specs + [ANY] * len(side.out_shapes), out_shapes + side.out_shapes
        scratch = scratch + side.sem_shapes
        sem = ("arbitrary",) * 3
    outs = pl.pallas_call(
        body, name=name, grid=grid, in_specs=in_specs, out_specs=out_specs, out_shape=out_shapes,
        scratch_shapes=scratch, compiler_params=_params(*sem),
    )(*operands)
    return outs[0], outs[1], list(outs[2:])


def _ffn_act_bwd(dyb, w_down, gu, name):
    S, D = dyb.shape
    F = w_down.shape[0]
    tm, tn, tk = _tile(S, 1024), _tile(F, 512), _tile(D, 2048)
    nk = D // tk

    def body(a_ref, w_ref, gu_ref, dgu_ref, acc):
        part = _dot(a_ref[...], w_ref[...], NT)

        def finish(dact):
            g = gu_ref[0]
            u = gu_ref[1]
            sg = 1.0 / (1.0 + jnp.exp(-g))
            dgu_ref[0] = (dact * u * (sg * (1.0 + g * (1.0 - sg)))).astype(BF16)
            dgu_ref[1] = (dact * (g * sg)).astype(BF16)

        if nk == 1:
            finish(part)
        else:
            k = pl.program_id(2)

            @pl.when(k == 0)
            def _():
                acc[...] = part

            @pl.when(k > 0)
            def _():
                acc[...] += part

            @pl.when(k == nk - 1)
            def _():
                finish(acc[...])

    gu_spec = pl.BlockSpec((2, tm, tn), lambda i, j, k: (0, i, j))
    return pl.pallas_call(
        body, name=name, grid=(S // tm, F // tn, nk),
        in_specs=[pl.BlockSpec((tm, tk), lambda i, j, k: (i, k)),
                  pl.BlockSpec((tn, tk), lambda i, j, k: (j, k)), gu_spec],
        out_specs=gu_spec, out_shape=jax.ShapeDtypeStruct((2, S, F), BF16),
        scratch_shapes=[pltpu.VMEM((tm, tn) if nk > 1 else (8, LANES), F32)],
        compiler_params=_params("parallel", "parallel", "arbitrary"),
    )(dyb, w_down, gu)


LOG2E = 1.4426950408889634
Q_PRESCALE = HEAD_DIM ** -0.5 * LOG2E
AUX_A = 0
AUX_B = 3


def _aux_block(lane, minus=None, minus_at=None, ones_at=None):
    out = jnp.zeros(lane.shape, BF16)
    if minus is not None:
        x = -minus
        hi = x.astype(BF16)
        r1 = x - hi.astype(F32)
        mid = r1.astype(BF16)
        lo = (r1 - mid.astype(F32)).astype(BF16)
        for n, piece in enumerate((hi, mid, lo)):
            out = jnp.where(lane == minus_at + n, piece, out)
    if ones_at is not None:
        out = jnp.where((lane >= ones_at) & (lane < ones_at + 3), jnp.ones(lane.shape, BF16), out)
    return out


def _qkv_post(proj, gq, gk, cum, name):
    S, D3 = proj.shape
    D = D3 // 3
    tm = _tile(S, 256, 16)

    def body(q_ref, k_ref, v_ref, gq_ref, gk_ref, c_ref, qn_ref, kn_ref, vb_ref, ak_ref):
        lane = lax.broadcasted_iota(jnp.int32, (tm, HEAD_DIM), 1)
        for src, g_ref, dst, mult in ((q_ref, gq_ref, qn_ref, Q_PRESCALE), (k_ref, gk_ref, kn_ref, None)):
            for h in range(D // HEAD_DIM):
                cols = slice(h * HEAD_DIM, (h + 1) * HEAD_DIM)
                t = src[:, cols]
                r = lax.rsqrt(jnp.mean(t * t, axis=-1, keepdims=True) + RMS_EPS)
                n = (t * r) * g_ref[...]
                dst[:, cols] = (n if mult is None else n * mult).astype(BF16)
        for h in range(D // HEAD_DIM):
            ch = jnp.broadcast_to(c_ref[:, h:h + 1] * LOG2E, (tm, HEAD_DIM))
            ak_ref[:, h * HEAD_DIM:(h + 1) * HEAD_DIM] = _aux_block(lane, ch, AUX_A, AUX_B)
        vb_ref[...] = v_ref[...].astype(BF16)

    part = lambda n: pl.BlockSpec((tm, D), lambda i: (i, n))
    vec = pl.BlockSpec((1, HEAD_DIM), lambda i: (0, 0))
    row = pl.BlockSpec((tm, D), lambda i: (i, 0))
    return pl.pallas_call(
        body, name=name, grid=(S // tm,),
        in_specs=[part(0), part(1), part(2), vec, vec, pl.BlockSpec((tm, LANES), lambda i: (i, 0))],
        out_specs=[row, row, row, row], out_shape=[jax.ShapeDtypeStruct((S, D), BF16)] * 4,
        compiler_params=_params("parallel"),
    )(proj, proj, proj, gq, gk, cum)


def _qkv_post_bwd(proj, dqx, dkx, dvb, gq, gk, name):
    S, D = dvb.shape
    tm = _tile(S, 128, 16)

    def body(q_ref, k_ref, dqx_ref, dkx_ref, dvb_ref, gq_ref, gk_ref, dp_ref, dgq_ref, dgk_ref, dc_ref):
        @pl.when(pl.program_id(0) == 0)
        def _():
            dgq_ref[...] = jnp.zeros_like(dgq_ref)
            dgk_ref[...] = jnp.zeros_like(dgk_ref)

        lane = lax.broadcasted_iota(jnp.int32, (tm, LANES), 1)
        dc = jnp.zeros((tm, LANES), F32)
        for h in range(D // HEAD_DIM):
            at = (2 * h + 1) * HEAD_DIM
            diff = dqx_ref[:, at + AUX_B:at + AUX_B + 1] - dkx_ref[:, at + AUX_A:at + AUX_A + 1]
            dc = jnp.where(lane == h, jnp.broadcast_to(diff, (tm, LANES)), dc)
        dc_ref[...] = dc

        for n, (src, dsrc, g_ref, dg_ref, mult) in enumerate(
                ((q_ref, dqx_ref, gq_ref, dgq_ref, HEAD_DIM ** -0.5), (k_ref, dkx_ref, gk_ref, dgk_ref, 1.0 / LOG2E))):
            dg = jnp.zeros((1, HEAD_DIM), F32)
            for h in range(D // HEAD_DIM):
                cols = slice(h * HEAD_DIM, (h + 1) * HEAD_DIM)
                t = src[:, cols]
                r = lax.rsqrt(jnp.mean(t * t, axis=-1, keepdims=True) + RMS_EPS)
                that = t * r
                dn = dsrc[:, 2 * h * HEAD_DIM:(2 * h + 1) * HEAD_DIM] * mult
                dhat = dn * g_ref[...]
                dt = r * (dhat - that * jnp.mean(dhat * that, axis=-1, keepdims=True))
                dp_ref[:, n * D + h * HEAD_DIM:n * D + (h + 1) * HEAD_DIM] = dt.astype(BF16)
                dg = dg + jnp.sum(dn * that, axis=0, keepdims=True)
            dg_ref[...] += dg
        dp_ref[:, 2 * D:3 * D] = dvb_ref[...]

    part = lambda n: pl.BlockSpec((tm, D), lambda i: (i, n))
    row = pl.BlockSpec((tm, D), lambda i: (i, 0))
    vec = pl.BlockSpec((1, HEAD_DIM), lambda i: (0, 0))
    wide = pl.BlockSpec((tm, 2 * D), lambda i: (i, 0))
    return pl.pallas_call(
        body, name=name, grid=(S // tm,), in_specs=[part(0), part(1), wide, wide, row, vec, vec],
        out_specs=[pl.BlockSpec((tm, 3 * D), lambda i: (i, 0)), vec, vec,
                   pl.BlockSpec((tm, LANES), lambda i: (i, 0))],
        out_shape=[jax.ShapeDtypeStruct((S, 3 * D), BF16), jax.ShapeDtypeStruct((1, HEAD_DIM), F32),
                   jax.ShapeDtypeStruct((1, HEAD_DIM), F32), jax.ShapeDtypeStruct((S, LANES), F32)],
        compiler_params=_params("arbitrary"),
    )(proj, proj, dqx, dkx, dvb, gq, gk)


def _log_sigmoid(z):
    return -(jnp.maximum(-z, 0.0) + jnp.log(1.0 + jnp.exp(-jnp.abs(z))))


def _fgate_fwd(fl, bf, name):
    S = fl.shape[0]
    T = _tile(S, 256, 16)

    def body(fl_ref, bf_ref, c_ref, carry):
        @pl.when(pl.program_id(0) == 0)
        def _():
            carry[...] = jnp.zeros_like(carry)

        logf = _log_sigmoid(fl_ref[...] + bf_ref[...])
        r = lax.broadcasted_iota(jnp.int32, (T, T), 0)
        c = lax.broadcasted_iota(jnp.int32, (T, T), 1)
        tri = (r >= c).astype(F32)
        cum = jnp.dot(tri, logf, precision=lax.Precision.HIGHEST, preferred_element_type=F32) + carry[...]
        c_ref[...] = cum
        carry[...] = cum[T - 1:T, :]

    row = pl.BlockSpec((T, LANES), lambda i: (i, 0))
    return pl.pallas_call(
        body, name=name, grid=(S // T,), in_specs=[row, pl.BlockSpec((1, LANES), lambda i: (0, 0))],
        out_specs=row, out_shape=jax.ShapeDtypeStruct((S, LANES), F32),
        scratch_shapes=[pltpu.VMEM((1, LANES), F32)], compiler_params=_params("arbitrary"),
    )(fl, bf)


def _fgate_bwd(dc, fl, bf, name):
    S = fl.shape[0]
    T = _tile(S, 256, 16)
    nb = S // T

    def body(dc_ref, fl_ref, bf_ref, dfl_ref, dbf_ref, carry):
        @pl.when(pl.program_id(0) == 0)
        def _():
            carry[...] = jnp.zeros_like(carry)
            dbf_ref[...] = jnp.zeros_like(dbf_ref)

        r = lax.broadcasted_iota(jnp.int32, (T, T), 0)
        c = lax.broadcasted_iota(jnp.int32, (T, T), 1)
        triu = (c >= r).astype(F32)
        dlogf = jnp.dot(triu, dc_ref[...], precision=lax.Precision.HIGHEST,
                        preferred_element_type=F32) + carry[...]
        carry[...] = dlogf[0:1, :]
        z = fl_ref[...] + bf_ref[...]
        dz = dlogf * (1.0 / (1.0 + jnp.exp(z)))
        dfl_ref[...] = dz.astype(BF16)
        dbf_ref[...] += jnp.sum(dz, axis=0, keepdims=True)

    row = pl.BlockSpec((T, LANES), lambda i: (nb - 1 - i, 0))
    vec = pl.BlockSpec((1, LANES), lambda i: (0, 0))
    return pl.pallas_call(
        body, name=name, grid=(nb,), in_specs=[row, row, vec], out_specs=[row, vec],
        out_shape=[jax.ShapeDtypeStruct((S, LANES), BF16), jax.ShapeDtypeStruct((1, LANES), F32)],
        scratch_shapes=[pltpu.VMEM((1, LANES), F32)], compiler_params=_params("arbitrary"),
    )(dc, fl, bf)


def _attn_logits(q_ref, aq, k_ref, ak, masked, T):
    qf = jnp.concatenate([q_ref[...], aq], axis=1)
    kf = jnp.concatenate([k_ref[...], ak], axis=1)
    s = _dot(qf, kf, NT)
    if masked:
        r = lax.broadcasted_iota(jnp.int32, (T, T), 0)
        c = lax.broadcasted_iota(jnp.int32, (T, T), 1)
        s = jnp.where(r >= c, s, NEG_INF)
    return s, qf, kf


def _causal_pairs(nb, q_major):
    pairs = ([(i, j) for i in range(nb) for j in range(i + 1)] if q_major
             else [(i, j) for j in range(nb) for i in range(j, nb)])
    return (jnp.asarray(np.array([p[0] for p in pairs], np.int32)),
            jnp.asarray(np.array([p[1] for p in pairs], np.int32)))


def _flash_fwd(q2, kn, vb, aux_k, name, side=None):
    S, D = q2.shape
    H = D // HEAD_DIM
    T = _tile(S, 512)
    nb = S // T
    q_idx, k_idx = _causal_pairs(nb, True)

    def body(qi_ref, kj_ref, q_ref, k_ref, v_ref, ak_ref, o_ref, aq_ref, m_s, acc_s):
        t = pl.program_id(1)
        i = qi_ref[t]
        j = kj_ref[t]
        lane = lax.broadcasted_iota(jnp.int32, (T, HEAD_DIM), 1)

        @pl.when(j == 0)
        def _():
            m_s[...] = jnp.full_like(m_s, NEG_INF)
            acc_s[...] = jnp.zeros_like(acc_s)

        def step(masked):
            s, _, _ = _attn_logits(q_ref, _aux_block(lane, ones_at=AUX_A), k_ref, ak_ref[...], masked, T)
            m_prev = m_s[...]
            m_new = jnp.maximum(m_prev, jnp.max(s, axis=1, keepdims=True))
            p = jnp.exp2(s - jnp.tile(m_new, (1, T // HEAD_DIM)))
            alpha = jnp.exp2(m_prev - m_new)
            vf = jnp.concatenate([v_ref[...], jnp.ones((T, HEAD_DIM), BF16)], axis=1)
            acc_s[...] = jnp.tile(alpha, (1, 2)) * acc_s[...] + _dot(p.astype(BF16), vf, NN)
            m_s[...] = m_new

        @pl.when(j < i)
        def _():
            step(False)

        @pl.when(j == i)
        def _():
            step(True)
            l = acc_s[:, HEAD_DIM:]
            o_ref[...] = (acc_s[:, :HEAD_DIM] / l).astype(BF16)
            aq_ref[...] = _aux_block(lane, m_s[...] + jnp.log2(l), AUX_B, AUX_A)

    qspec = pl.BlockSpec((T, HEAD_DIM), lambda h, t, qi, kj: (qi[t], h))
    kspec = pl.BlockSpec((T, HEAD_DIM), lambda h, t, qi, kj: (kj[t], h))
    grid = (H, int(q_idx.shape[0]))
    operands = [q_idx, k_idx, q2, kn, vb, aux_k]
    in_specs, out_specs = [qspec, kspec, kspec, kspec], [qspec, qspec]
    out_shapes = [jax.ShapeDtypeStruct((S, D), BF16), jax.ShapeDtypeStruct((S, D), BF16)]
    scratch = [pltpu.VMEM((T, HEAD_DIM), F32), pltpu.VMEM((T, 2 * HEAD_DIM), F32)]
    sem = ("parallel", "arbitrary")
    if side is not None:
        body = _host(side, body, grid, 6, 2, 2)
        operands, in_specs = operands + side.operands, in_specs + [ANY] * len(side.operands)
        out_specs, out_shapes = out_specs + [ANY] * len(side.out_shapes), out_shapes + side.out_shapes
        scratch = scratch + side.sem_shapes
        sem = ("arbitrary", "arbitrary")
    grid_spec = pltpu.PrefetchScalarGridSpec(
        num_scalar_prefetch=2, grid=grid, in_specs=in_specs, out_specs=out_specs, scratch_shapes=scratch)
    outs = pl.pallas_call(
        body, name=name, grid_spec=grid_spec, out_shape=out_shapes, compiler_params=_params(*sem),
    )(*operands)
    return outs[0], outs[1], list(outs[2:])


def _attn_delta(do, o, name):
    S, D = do.shape
    H = D // HEAD_DIM
    tm = _tile(S, 256, 16)

    def body(do_ref, o_ref, ad_ref, dob_ref):
        lane = lax.broadcasted_iota(jnp.int32, (tm, HEAD_DIM), 1)
        for h in range(H):
            cols = slice(h * HEAD_DIM, (h + 1) * HEAD_DIM)
            delta = jnp.sum(do_ref[:, cols] * o_ref[:, cols].astype(F32), axis=1, keepdims=True)
            ad_ref[:, cols] = _aux_block(lane, jnp.broadcast_to(delta, (tm, HEAD_DIM)), AUX_A)
        dob_ref[...] = do_ref[...].astype(BF16)

    row = pl.BlockSpec((tm, D), lambda i: (i, 0))
    return pl.pallas_call(
        body, name=name, grid=(S // tm,), in_specs=[row, row], out_specs=[row, row],
        out_shape=[jax.ShapeDtypeStruct((S, D), BF16), jax.ShapeDtypeStruct((S, D), BF16)],
        compiler_params=_params("parallel"),
    )(do, o)


def _flash_bwd(q2, kn, vb, dob, aux_q, aux_k, aux_do, name, side=None):
    S, D = q2.shape
    H = D // HEAD_DIM
    T = _tile(S, 512)
    nb = S // T
    q_idx, k_idx = _causal_pairs(nb, False)

    def body(qi_ref, kj_ref, q_ref, k_ref, v_ref, do_ref, aq_ref, ak_ref, ad_ref,
             dqx_ref, dkx_ref, dv_ref, dv_acc):
        t = pl.program_id(1)
        i = qi_ref[t]
        j = kj_ref[t]
        lane = lax.broadcasted_iota(jnp.int32, (T, HEAD_DIM), 1)

        @pl.when(t == 0)
        def _():
            dqx_ref[...] = jnp.zeros_like(dqx_ref)

        @pl.when(i == j)
        def _():
            dkx_ref[...] = jnp.zeros_like(dkx_ref)
            dv_acc[...] = jnp.zeros_like(dv_acc)

        def step(masked):
            s, qf, kf = _attn_logits(q_ref, aq_ref[...], k_ref, ak_ref[...], masked, T)
            p = jnp.exp2(s)
            dof = jnp.concatenate([do_ref[...], ad_ref[...]], axis=1)
            vf = jnp.concatenate([v_ref[...], _aux_block(lane, ones_at=AUX_A)], axis=1)
            ds = p * _dot(dof, vf, NT)
            dsb = ds.astype(BF16)
            dv_acc[...] += _dot(p.astype(BF16), do_ref[...], TN)
            dkx_ref[...] += _dot(dsb, qf, TN)
            rows = pl.ds(pl.multiple_of(i * T, T), T)
            dqx_ref[rows, :] += _dot(dsb, kf, NN)

        @pl.when(i > j)
        def _():
            step(False)

        @pl.when(i == j)
        def _():
            step(True)

        @pl.when(i == nb - 1)
        def _():
            dv_ref[...] = dv_acc[...].astype(BF16)

    qspec = pl.BlockSpec((T, HEAD_DIM), lambda h, t, qi, kj: (qi[t], h))
    kspec = pl.BlockSpec((T, HEAD_DIM), lambda h, t, qi, kj: (kj[t], h))
    grid = (H, int(q_idx.shape[0]))
    operands = [q_idx, k_idx, q2, kn, vb, dob, aux_q, aux_k, aux_do]
    in_specs = [qspec, kspec, kspec, qspec, qspec, kspec, qspec]
    out_specs = [pl.BlockSpec((S, 2 * HEAD_DIM), lambda h, t, qi, kj: (0, h)),
                 pl.BlockSpec((T, 2 * HEAD_DIM), lambda h, t, qi, kj: (kj[t], h)), kspec]
    out_shapes = [jax.ShapeDtypeStruct((S, 2 * D), F32), jax.ShapeDtypeStruct((S, 2 * D), F32),
                  jax.ShapeDtypeStruct((S, D), BF16)]
    scratch = [pltpu.VMEM((T, HEAD_DIM), F32)]
    sem = ("parallel", "arbitrary")
    if side is not None:
        body = _host(side, body, grid, 9, 3, 1)
        operands, in_specs = operands + side.operands, in_specs + [ANY] * len(side.operands)
        out_specs, out_shapes = out_specs + [ANY] * len(side.out_shapes), out_shapes + side.out_shapes
        scratch = scratch + side.sem_shapes
        sem = ("arbitrary", "arbitrary")
    grid_spec = pltpu.PrefetchScalarGridSpec(
        num_scalar_prefetch=2, grid=grid, in_specs=in_specs, out_specs=out_specs, scratch_shapes=scratch)
    outs = pl.pallas_call(
        body, name=name, grid_spec=grid_spec, out_shape=out_shapes, compiler_params=_params(*sem),
    )(*operands)
    return outs[0], outs[1], outs[2], list(outs[3:])


def _pool_counts(first_row, tm, win):
    t = first_row + lax.broadcasted_iota(jnp.int32, (tm, 1), 0)
    return jnp.minimum(t + 1, win).astype(F32)


def _pool_fwd(h, x, wp, b, scale, name):
    S, D = h.shape
    G = D // len(POOL_WINDOWS)
    tm = _tile(S, 256, 16)

    def body(h_ref, halo_ref, x_ref, w_ref, b_ref, s_ref, y_ref, o_ref, ext):
        i = pl.program_id(0)
        ext[POOL_HALO:, :] = h_ref[...]

        @pl.when(i == 0)
        def _():
            ext[0:POOL_HALO, :] = jnp.zeros((POOL_HALO, D), F32)

        @pl.when(i > 0)
        def _():
            ext[0:POOL_HALO, :] = halo_ref[...]

        for g, win in enumerate(POOL_WINDOWS):
            cols = slice(g * G, (g + 1) * G)
            tot = ext[POOL_HALO:POOL_HALO + tm, cols]
            for k in range(1, win):
                tot = tot + ext[POOL_HALO - k:POOL_HALO - k + tm, cols]
            y = (tot / _pool_counts(i * tm, tm, win) - h_ref[:, cols]).astype(BF16)
            y_ref[:, cols] = y
            z = _dot(y, w_ref[g], NN)
            o_ref[:, cols] = x_ref[:, cols] + (z + b_ref[:, cols]) * s_ref[:, cols]

    row = pl.BlockSpec((tm, D), lambda i: (i, 0))
    vec = pl.BlockSpec((1, D), lambda i: (0, 0))
    halo = pl.BlockSpec((POOL_HALO, D), lambda i: (jnp.maximum(i * (tm // POOL_HALO) - 1, 0), 0))
    return pl.pallas_call(
        body, name=name, grid=(S // tm,),
        in_specs=[row, halo, row, pl.BlockSpec((len(POOL_WINDOWS), G, G), lambda i: (0, 0, 0)), vec, vec],
        out_specs=[row, row],
        out_shape=[jax.ShapeDtypeStruct((S, D), BF16), jax.ShapeDtypeStruct((S, D), F32)],
        scratch_shapes=[pltpu.VMEM((tm + POOL_HALO, D), F32)], compiler_params=_params("parallel"),
    )(h, h, x, wp, b, scale)


def _pool_bwd_mix(dout, yb, wp, b, scale, name):
    S, D = dout.shape
    NG = len(POOL_WINDOWS)
    G = D // NG
    tm = _tile(S, 256, 16)

    def body(do_ref, y_ref, w_ref, b_ref, s_ref, dyc_ref, dw_ref, db_ref, ds_ref):
        i = pl.program_id(0)

        @pl.when(i == 0)
        def _():
            dw_ref[...] = jnp.zeros_like(dw_ref)
            db_ref[...] = jnp.zeros_like(db_ref)
            ds_ref[...] = jnp.zeros_like(ds_ref)

        for g, win in enumerate(POOL_WINDOWS):
            cols = slice(g * G, (g + 1) * G)
            y = y_ref[:, cols]
            dz = do_ref[:, cols]
            zb = _dot(y, w_ref[g], NN) + b_ref[:, cols]
            ds_ref[:, cols] += jnp.sum(dz * zb, axis=0, keepdims=True)
            dzb = dz * s_ref[:, cols]
            db_ref[:, cols] += jnp.sum(dzb, axis=0, keepdims=True)
            dzb16 = dzb.astype(BF16)
            dw_ref[g] += _dot(y, dzb16, TN)
            dy = _dot(dzb16, w_ref[g], NT)
            dyc_ref[:, cols] = dy / _pool_counts(i * tm, tm, win)

    row = pl.BlockSpec((tm, D), lambda i: (i, 0))
    vec = pl.BlockSpec((1, D), lambda i: (0, 0))
    wspec = pl.BlockSpec((NG, G, G), lambda i: (0, 0, 0))
    return pl.pallas_call(
        body, name=name, grid=(S // tm,), in_specs=[row, row, wspec, vec, vec],
        out_specs=[row, wspec, vec, vec],
        out_shape=[jax.ShapeDtypeStruct((S, D), F32), jax.ShapeDtypeStruct((NG, G, G), F32),
                   jax.ShapeDtypeStruct((1, D), F32), jax.ShapeDtypeStruct((1, D), F32)],
        compiler_params=_params("arbitrary"),
    )(dout, yb, wp, b, scale)


def _pool_bwd_window(dyc, name):
    S, D = dyc.shape
    G = D // len(POOL_WINDOWS)
    tm = _tile(S, 256, 16)
    nb = S // tm

    def body(d_ref, halo_ref, dh_ref, ext):
        i = pl.program_id(0)
        ext[0:tm, :] = d_ref[...]

        @pl.when(i == nb - 1)
        def _():
            ext[tm:tm + POOL_HALO, :] = jnp.zeros((POOL_HALO, D), F32)

        @pl.when(i < nb - 1)
        def _():
            ext[tm:tm + POOL_HALO, :] = halo_ref[...]

        for g, win in enumerate(POOL_WINDOWS):
            cols = slice(g * G, (g + 1) * G)
            tot = ext[0:tm, cols] * (1.0 - _pool_counts(i * tm, tm, win))
            for k in range(1, win):
                tot = tot + ext[k:k + tm, cols]
            dh_ref[:, cols] = tot

    row = pl.BlockSpec((tm, D), lambda i: (i, 0))
    halo = pl.BlockSpec((POOL_HALO, D),
                        lambda i: (jnp.minimum((i + 1) * (tm // POOL_HALO), S // POOL_HALO - 1), 0))
    return pl.pallas_call(
        body, name=name, grid=(nb,), in_specs=[row, halo], out_specs=row,
        out_shape=jax.ShapeDtypeStruct((S, D), F32),
        scratch_shapes=[pltpu.VMEM((tm + POOL_HALO, D), F32)], compiler_params=_params("parallel"),
    )(dyc, dyc)


def _adamw(w, g, m, v, name):
    R, C = w.shape
    tr = _row_tile(R, C, 4, 1 << 20)

    def body(w_ref, g_ref, m_ref, v_ref, d_ref, nm_ref, nv_ref):
        gv = g_ref[...]
        m_new = ADAM_B1 * m_ref[...] + (1.0 - ADAM_B1) * gv
        v_new = ADAM_B2 * v_ref[...] + (1.0 - ADAM_B2) * (gv * gv)
        m_hat = m_new / (1.0 - ADAM_B1 ** ADAM_STEP)
        v_hat = v_new / (1.0 - ADAM_B2 ** ADAM_STEP)
        d_ref[...] = -ADAM_LR * (m_hat / (jnp.sqrt(v_hat) + ADAM_EPS) + ADAM_WD * w_ref[...])
        nm_ref[...] = m_new
        nv_ref[...] = v_new

    row = pl.BlockSpec((tr, C), lambda i: (i, 0))
    return pl.pallas_call(
        body, name=name, grid=(R // tr,), in_specs=[row] * 4, out_specs=[row] * 3,
        out_shape=[jax.ShapeDtypeStruct((R, C), F32)] * 3, compiler_params=_params("parallel"),
    )(w, g, m, v)


def _sum_core_halves(g, r1, c_idx, name):
    _, _, Rh, C = g.shape
    tr = _row_tile(Rh, C, 4, 2 << 20)

    def body(c_ref, g_ref, r_ref, o_ref, ob_ref):
        total = g_ref[...] + r_ref[...]
        o_ref[...] = total
        ob_ref[...] = total.astype(BF16)

    piece = pl.BlockSpec((None, tr, C), lambda s, r, c_ref: (s, r, 0))
    grid_spec = pltpu.PrefetchScalarGridSpec(
        num_scalar_prefetch=1, grid=(N_CHIPS, Rh // tr),
        in_specs=[pl.BlockSpec((None, None, tr, C), lambda s, r, c_ref: (s, c_ref[0], r, 0)), piece],
        out_specs=[piece, piece])
    return pl.pallas_call(
        body, name=name, grid_spec=grid_spec,
        out_shape=[jax.ShapeDtypeStruct((N_CHIPS, Rh, C), F32), jax.ShapeDtypeStruct((N_CHIPS, Rh, C), BF16)],
        compiler_params=_params("parallel", "parallel"),
    )(c_idx, g, r1)


def _sum_chips(h, r2, place, name):
    _, Rh, C = h.shape
    tr = _row_tile(Rh, C, 4, 1 << 20)

    def body(place_ref, h_ref, r_ref, o_ref):
        o_ref[...] = ((h_ref[...] + r_ref[0].astype(F32)) + r_ref[1].astype(F32)) + r_ref[2].astype(F32)

    grid_spec = pltpu.PrefetchScalarGridSpec(
        num_scalar_prefetch=1, grid=(Rh // tr,),
        in_specs=[pl.BlockSpec((None, tr, C), lambda r, pr: (pr[0], r, 0)),
                  pl.BlockSpec((N_CHIPS - 1, tr, C), lambda r, pr: (0, r, 0))],
        out_specs=pl.BlockSpec((None, tr, C), lambda r, pr: (pr[1], r, 0)))
    return pl.pallas_call(
        body, name=name, grid_spec=grid_spec, out_shape=jax.ShapeDtypeStruct((2, Rh, C), F32),
        compiler_params=_params("parallel"),
    )(place, h, r2)


ANY = pl.BlockSpec(memory_space=pl.ANY)


def _place():
    x, y, c = lax.axis_index("x"), lax.axis_index("y"), lax.axis_index("c")
    chips = [(1 - x, y), (x, 1 - y), (1 - x, 1 - y)]
    return x, y, c, chips, [2 * cx + cy for cx, cy in chips]


def _run_side(side, name):
    def body(*refs):
        _, side_refs = side.split(refs, 0, 0, 0)
        side.start(*side_refs)
        side.finish(*side_refs)

    return pl.pallas_call(
        body, name=name, in_specs=[ANY] * len(side.operands), out_specs=[ANY] * len(side.out_shapes),
        out_shape=side.out_shapes, scratch_shapes=side.sem_shapes,
    )(*side.operands)


def _allgather_side(shards):
    n = len(shards)

    def copy(sems, t, k, src, dst, to):
        return pltpu.make_async_remote_copy(src_ref=src, dst_ref=dst, send_sem=sems[0].at[t, k],
                                            recv_sem=sems[1].at[t, k], device_id=to, device_id_type=MESH)

    def first_copies(ins, outs, sems):
        x, y, c, chips, _ = _place()
        me = 2 * x + y
        return [copy(sems, t, j, ins[t].at[c], outs[t].at[me, c], (*chip, c))
                for t in range(n) for j, chip in enumerate(chips)]

    def start(ins, outs, sems):
        for cp in first_copies(ins, outs, sems):
            cp.start()

    def finish(ins, outs, sems):
        x, y, c, chips, chip_idx = _place()
        sibling = (x, y, 1 - c)
        passed = []
        for t in range(n):
            for j, chip in enumerate(chips):
                landed = outs[t].at[chip_idx[j], c]
                copy(sems, t, j, landed, landed, (*chip, c)).wait_recv()
                fw = copy(sems, t, 3 + j, landed, landed, sibling)
                fw.start()
                passed.append(fw)
        for t in range(n):
            for j in range(3):
                other = outs[t].at[chip_idx[j], 1 - c]
                copy(sems, t, 3 + j, other, other, sibling).wait_recv()
        for cp in first_copies(ins, outs, sems) + passed:
            cp.wait_send()

    return _Side(shards, [jax.ShapeDtypeStruct((N_CHIPS,) + s.shape, s.dtype) for s in shards],
                 [pltpu.SemaphoreType.DMA((n, 6)), pltpu.SemaphoreType.DMA((n, 6))], start, finish)


def _send_other_half_to_sibling(gs, name):
    n = len(gs)

    def body(*refs):
        ins, outs = refs[:n], refs[n:2 * n]
        send_sems, recv_sems = refs[2 * n:]
        x, y, c, _, _ = _place()
        copies = []
        for t in range(n):
            for s in range(N_CHIPS):
                cp = pltpu.make_async_remote_copy(
                    src_ref=ins[t].at[s, 1 - c], dst_ref=outs[t].at[s], send_sem=send_sems.at[t, s],
                    recv_sem=recv_sems.at[t, s], device_id=(x, y, 1 - c), device_id_type=MESH)
                cp.start()
                copies.append(cp)
        for cp in copies:
            cp.wait()

    return pl.pallas_call(
        body, name=name, in_specs=[ANY] * n, out_specs=[ANY] * n,
        out_shape=[jax.ShapeDtypeStruct((N_CHIPS,) + g.shape[2:], g.dtype) for g in gs],
        scratch_shapes=[pltpu.SemaphoreType.DMA((n, N_CHIPS)), pltpu.SemaphoreType.DMA((n, N_CHIPS))],
    )(*gs)


def _exchange_side(hs):
    n = len(hs)

    def copies(ins, outs, sems):
        x, y, c, chips, chip_idx = _place()
        return [pltpu.make_async_remote_copy(
            src_ref=ins[t].at[chip_idx[j]], dst_ref=outs[t].at[j], send_sem=sems[0].at[t, j],
            recv_sem=sems[1].at[t, j], device_id=(*chip, c), device_id_type=MESH)
            for t in range(n) for j, chip in enumerate(chips)]

    def start(ins, outs, sems):
        for cp in copies(ins, outs, sems):
            cp.start()

    def finish(ins, outs, sems):
        for cp in copies(ins, outs, sems):
            cp.wait()

    return _Side(hs, [jax.ShapeDtypeStruct((N_CHIPS - 1,) + h.shape[1:], h.dtype) for h in hs],
                 [pltpu.SemaphoreType.DMA((n, 3)), pltpu.SemaphoreType.DMA((n, 3))], start, finish)


def _join_core_halves(bufs, name):
    n = len(bufs)

    def body(*refs):
        outs = refs[n:2 * n]
        send_sems, recv_sems = refs[2 * n:]
        x, y, c, _, _ = _place()
        sends = []
        for t in range(n):
            cp = pltpu.make_async_remote_copy(
                src_ref=outs[t].at[c], dst_ref=outs[t].at[c], send_sem=send_sems.at[t], recv_sem=recv_sems.at[t],
                device_id=(x, y, 1 - c), device_id_type=MESH)
            cp.start()
            sends.append(cp)
        for t in range(n):
            other = outs[t].at[1 - c]
            pltpu.make_async_remote_copy(
                src_ref=other, dst_ref=other, send_sem=send_sems.at[t], recv_sem=recv_sems.at[t],
                device_id=(x, y, 1 - c), device_id_type=MESH).wait_recv()
        for cp in sends:
            cp.wait_send()

    return pl.pallas_call(
        body, name=name, in_specs=[ANY] * n, out_specs=[ANY] * n,
        out_shape=[jax.ShapeDtypeStruct(b.shape, b.dtype) for b in bufs],
        input_output_aliases={t: t for t in range(n)},
        scratch_shapes=[pltpu.SemaphoreType.DMA((n,)), pltpu.SemaphoreType.DMA((n,))],
    )(*bufs)


def _allreduce_small(vec, name):
    R = vec.shape[0]

    def body(v_ref, o_ref, buf, send_sems, recv_sems):
        x, y, c = lax.axis_index("x"), lax.axis_index("y"), lax.axis_index("c")
        me = 4 * x + 2 * y + c
        buf[me] = v_ref[...]
        peers = []
        for k in range(1, N_DEV):
            px = 1 - x if k & 4 else x
            py = 1 - y if k & 2 else y
            pc = 1 - c if k & 1 else c
            peers.append(((px, py, pc), 4 * px + 2 * py + pc))
        sends = []
        for k, (peer, _) in enumerate(peers):
            cp = pltpu.make_async_remote_copy(
                src_ref=v_ref, dst_ref=buf.at[me], send_sem=send_sems.at[k], recv_sem=recv_sems.at[k],
                device_id=peer, device_id_type=MESH)
            cp.start()
            sends.append(cp)
        for k, (peer, idx) in enumerate(peers):
            pltpu.make_async_remote_copy(
                src_ref=v_ref, dst_ref=buf.at[idx], send_sem=send_sems.at[k], recv_sem=recv_sems.at[k],
                device_id=peer, device_id_type=MESH).wait_recv()
        for cp in sends:
            cp.wait_send()
        total = buf[0]
        for d in range(1, N_DEV):
            total = total + buf[d]
        o_ref[...] = total

    vm = pl.BlockSpec(memory_space=pltpu.VMEM)
    return pl.pallas_call(
        body, name=name, in_specs=[vm], out_specs=vm, out_shape=jax.ShapeDtypeStruct((R, LANES), F32),
        scratch_shapes=[pltpu.VMEM((N_DEV, R, LANES), F32), pltpu.SemaphoreType.DMA((N_DEV - 1,)),
                        pltpu.SemaphoreType.DMA((N_DEV - 1,))],
    )(vec)


def _halves(a):
    lead = 1
    for d in a.shape[:-1]:
        lead *= d
    return a.reshape(2, lead // 2, a.shape[-1])


def _cols_from_shards(g):
    return g.transpose(1, 0, 2).reshape(g.shape[1], N_CHIPS * g.shape[2])


def _shards_from_cols(w):
    return w.reshape(w.shape[0], N_CHIPS, w.shape[1] // N_CHIPS).transpose(1, 0, 2)


def _pack(parts, rows):
    flat = jnp.concatenate([p.reshape(-1).astype(F32) for p in parts])
    return jnp.pad(flat, (0, rows * LANES - flat.shape[0])).reshape(rows, LANES)


def _unpack(packed, shapes):
    flat = packed.reshape(-1)
    out, off = [], 0
    for s in shapes:
        n = 1
        for d in s:
            n *= d
        out.append(flat[off:off + n].reshape(s))
        off += n
    return out


def _packed_rows(shapes):
    n = 0
    for s in shapes:
        k = 1
        for d in s:
            k *= d
        n += k
    return -(-n // (8 * LANES)) * 8


def _pad_lanes(a):
    return jnp.pad(a, ((0, 0), (0, LANES - a.shape[1])))


def kernel(x, mix_norm_g, ffn_norm_g, fox_w_in, fox_b_f, fox_q_norm_g, fox_k_norm_g, fox_w_out, pool_w, pool_b, pool_scale, ffn_w_gate_up, ffn_w_down, loss_target, m_mix_norm_g, m_ffn_norm_g, m_fox_w_in, m_fox_b_f, m_fox_q_norm_g, m_fox_k_norm_g, m_fox_w_out, m_pool_w, m_pool_b, m_pool_scale, m_ffn_w_gate_up, m_ffn_w_down, v_mix_norm_g, v_ffn_norm_g, v_fox_w_in, v_fox_b_f, v_fox_q_norm_g, v_fox_k_norm_g, v_fox_w_out, v_pool_w, v_pool_b, v_pool_scale, v_ffn_w_gate_up, v_ffn_w_down):
    _, S, D = x.shape
    H = D // HEAD_DIM
    depth = mix_norm_g.shape[0]
    n_pool = pool_w.shape[0]
    NG = len(POOL_WINDOWS)
    G = D // NG
    F = ffn_w_down.shape[1] * N_CHIPS
    ax, ay, ac = lax.axis_index("x"), lax.axis_index("y"), lax.axis_index("c")
    chip = 2 * ax + ay
    c_idx = jnp.reshape(ac, (1,)).astype(jnp.int32)
    place = jnp.stack([chip, ac]).astype(jnp.int32)
    xs = x[0]
    target = loss_target[0]

    Dq = D // N_CHIPS
    small_fwd_shapes = [(n_pool, D), (n_pool, D)]
    placed = []
    for p in (pool_b, pool_scale):
        full = lax.dynamic_update_slice(jnp.zeros((n_pool, D), F32), p, (0, chip * Dq))
        placed.append(jnp.where(ac == 0, full, jnp.zeros_like(full)))
    pool_b_full, pool_scale_full = _unpack(
        _allreduce_small(_pack(placed, _packed_rows(small_fwd_shapes)), "gather_pool_vectors"), small_fwd_shapes)

    def weight_shards(i):
        mixer = ([fox_w_in[i // 2], fox_w_out[i // 2]] if i % 2 == 0 else [pool_w[i // 2]])
        return [_halves(w.astype(BF16)) for w in mixer + [ffn_w_gate_up[i], ffn_w_down[i]]]

    def with_own(shards, got):
        return [lax.dynamic_update_slice(g, sh[None], (chip, 0, 0, 0)) for g, sh in zip(got, shards)]

    def mixer_weights(i, shards, got):
        got = with_own(shards, got)
        if i % 2 == 1:
            return {"w_pool": got[0].reshape(N_CHIPS, NG, G // N_CHIPS, G).transpose(1, 0, 2, 3).reshape(NG, G, G)}
        w_in = _cols_from_shards(got[0].reshape(N_CHIPS, D, -1))
        return {"w_qkv": w_in[:, :3 * D], "w_f": _pad_lanes(w_in[:, 3 * D:]), "w_out": got[1].reshape(D, D)}

    def ffn_weights(shards, got):
        got = with_own(shards, got)
        return {"w_gu": _cols_from_shards(got[0].reshape(N_CHIPS, D, -1)), "w_down": got[1].reshape(F, D)}

    def layer_weights(i, shards, got):
        return {**mixer_weights(i, shards[:-2], got[:-2]), **ffn_weights(shards[-2:], got[-2:])}

    shards0 = weight_shards(0)
    weights = [mixer_weights(0, shards0[:-2], _run_side(_allgather_side(shards0[:-2]), "allgather_weights_first"))]
    own_ffn = shards0[-2:]

    saved = []
    cur = xs
    for i in range(depth):
        j = i // 2
        lw = weights[i]
        sv = {"x": cur}
        next_shards = weight_shards(i + 1) if i + 1 < depth else None
        gather_next = _allgather_side(next_shards) if next_shards is not None else None
        g_mix = mix_norm_g[i][None]
        if i % 2 == 0:
            h1b = _rmsnorm_fwd(cur, g_mix, BF16, "rmsnorm_fwd_bf16")
            proj = _matmul(h1b, lw["w_qkv"], "nn", F32, "qkv_proj", M=S, N=3 * D, K=D)
            fl = _matmul(h1b, lw["w_f"], "nn", F32, "forget_proj", M=S, N=LANES, K=D)
            bf = _pad_lanes(fox_b_f[j][None])
            cum = _fgate_fwd(fl, bf, "forget_cumsum")
            q2, kn, vb, aux_k = _qkv_post(proj, fox_q_norm_g[j][None], fox_k_norm_g[j][None], cum, "qk_norm")
            gather_own = _allgather_side(own_ffn) if own_ffn is not None else None
            o, aux_q, got = _flash_fwd(q2, kn, vb, aux_k, "fox_attention_fwd",
                                       side=_merge_sides([gather_own, gather_next]))
            if own_ffn is not None:
                lw.update(ffn_weights(own_ffn, got[:2]))
                got, own_ffn = got[2:], None
            gather_next = None
            x1 = _matmul(o, lw["w_out"], "nn", F32, "attn_out_proj", M=S, N=D, K=D, res=cur)
            sv.update(h1b=h1b, proj=proj, fl=fl, bf=bf, q2=q2, kn=kn, vb=vb, aux_k=aux_k, aux_q=aux_q, o=o)
        else:
            h1 = _rmsnorm_fwd(cur, g_mix, F32, "rmsnorm_fwd_f32")
            yb, x1 = _pool_fwd(h1, cur, lw["w_pool"], pool_b_full[j][None], pool_scale_full[j][None], "pool_fwd")
            sv.update(yb=yb)
        h2b = _rmsnorm_fwd(x1, ffn_norm_g[i][None], BF16, "rmsnorm_fwd_bf16")
        if gather_next is not None:
            gu, act, got = _ffn_up(h2b, lw["w_gu"], "ffn_gate_up_gather", side=gather_next)
        else:
            gu, act, _ = _ffn_up(h2b, lw["w_gu"], "ffn_gate_up")
        if next_shards is not None:
            weights.append(layer_weights(i + 1, next_shards, got))
        x2 = _matmul(act, lw["w_down"], "nn", F32, "ffn_down", M=S, N=D, K=F, tk=2816, res=x1)
        sv.update(x1=x1, h2b=h2b, gu=gu, act=act)
        saved.append(sv)
        cur = x2

    dcur, dcur_b, sq = _loss_grad(cur, target, "loss_grad")
    loss = lax.psum(sq[0, 0] * (0.5 / D), ("x", "y", "c"))

    g_mix_rows, g_ffn_rows = [None] * depth, [None] * depth
    g_bf, g_gq, g_gk = [None] * (depth - n_pool), [None] * (depth - n_pool), [None] * (depth - n_pool)
    g_pb, g_ps = [None] * n_pool, [None] * n_pool
    red_mix, red_ffn = [None] * depth, [None] * depth

    def half_sums(big, group):
        gs = [b.reshape(N_CHIPS, 2, -1, b.shape[-1]) for b in big]
        r1 = _send_other_half_to_sibling(gs, f"grads_to_sibling_{group}")
        sums = [_sum_core_halves(g, r, c_idx, "sum_core_halves") for g, r in zip(gs, r1)]
        return [s[0] for s in sums], [s[1] for s in sums]

    def finish_reduce(layer, group, hs, r2):
        rs = [_sum_chips(h, r, place, "sum_chips") for h, r in zip(hs, r2)]
        joined = _join_core_halves(rs, f"grads_join_{group}")
        if group != "ffn":
            red_mix[layer] = joined if group == "mix" else joined[:-2]
        if group != "mix":
            red_ffn[layer] = joined[-2:]

    pending = None
    for i in reversed(range(depth)):
        j = i // 2
        lw, sv = weights[i], saved[i]
        dgu = _ffn_act_bwd(dcur_b, lw["w_down"], sv["gu"], "ffn_act_bwd")
        d_w_down = _matmul(sv["act"], dcur_b, "tn", F32, "ffn_down_dw", M=F, N=D, K=S, tm=1408)
        dh2 = _matmul(
            dgu, lw["w_gu"], "nt", F32, "ffn_up_dx", M=S, N=D, K=2 * F, tk=_tile(F, 2816),
            a_spec=lambda tm, tn, tk: pl.BlockSpec(
                (None, tm, tk), lambda i_, j_, k_: (k_ // (F // tk), i_, k_ % (F // tk))))
        d_w_gu = _matmul(
            sv["h2b"], dgu, "tn", F32, "ffn_up_dw" if pending is None else "ffn_up_dw_exchange",
            M=D, N=2 * F, K=S, tn=_tile(F, 1408), out_shards=N_CHIPS,
            b_spec=lambda tm, tn, tk: pl.BlockSpec(
                (None, tk, tn), lambda i_, j_, k_: (j_ // (F // tn), k_, j_ % (F // tn))),
            side=None if pending is None else _exchange_side(pending[3]))
        if pending is not None:
            d_w_gu, r2 = d_w_gu
            finish_reduce(pending[0], pending[1], pending[2], r2)
        ffn_big = [d_w_gu, d_w_down.reshape(N_CHIPS, F // N_CHIPS, D)]
        dx1, dx1b, g_ffn_rows[i] = _rmsnorm_bwd(sv["x1"], ffn_norm_g[i][None], dh2, dcur, "rmsnorm_bwd")
        g_mix = mix_norm_g[i][None]
        if i % 2 == 0:
            do = _matmul(dx1b, lw["w_out"], "nt", F32, "attn_out_dx", M=S, N=D, K=D)
            d_w_out = _matmul(sv["o"], dx1b, "tn", F32, "attn_out_dw", M=D, N=D, K=S)
            aux_do, dob = _attn_delta(do, sv["o"], "attn_delta")
            ffn_hs, ffn_hb = half_sums(ffn_big, "ffn")
            dqx, dkx, dvb, r2 = _flash_bwd(sv["q2"], sv["kn"], sv["vb"], dob, sv["aux_q"], sv["aux_k"], aux_do,
                                           "fox_attention_bwd", side=_exchange_side(ffn_hb))
            finish_reduce(i, "ffn", ffn_hs, r2)
            dproj, dgq, dgk, dc = _qkv_post_bwd(sv["proj"], dqx, dkx, dvb, fox_q_norm_g[j][None],
                                                fox_k_norm_g[j][None], "qk_norm_bwd")
            dfl, dbf = _fgate_bwd(dc, sv["fl"], sv["bf"], "forget_cumsum_bwd")
            d_w_qkv = _matmul(sv["h1b"], dproj, "tn", F32, "qkv_dw", M=D, N=3 * D, K=S)
            d_w_f = _matmul(sv["h1b"], dfl, "tn", F32, "forget_dw", M=D, N=LANES, K=S)
            dh1f = _matmul(dfl, lw["w_f"], "nt", F32, "forget_dx", M=S, N=D, K=LANES)
            dh1 = _matmul(dproj, lw["w_qkv"], "nt", F32, "qkv_dx", M=S, N=D, K=3 * D, res=dh1f)
            d_w_in = jnp.concatenate([d_w_qkv, d_w_f[:, :H]], axis=1)
            big, group = [_shards_from_cols(d_w_in), d_w_out.reshape(N_CHIPS, D // N_CHIPS, D)], "mix"
            g_bf[j], g_gq[j], g_gk[j] = dbf[0, :H], dgq[0], dgk[0]
        else:
            dyc, d_wp, dpb, dps = _pool_bwd_mix(dx1, sv["yb"], lw["w_pool"], pool_b_full[j][None],
                                                pool_scale_full[j][None], "pool_bwd_mix")
            dh1 = _pool_bwd_window(dyc, "pool_bwd_window")
            big, group = [d_wp.reshape(NG, N_CHIPS, G // N_CHIPS, G).transpose(1, 0, 2, 3)] + ffn_big, "all"
            g_pb[j], g_ps[j] = dpb[0], dps[0]
        dcur, dcur_b, g_mix_rows[i] = _rmsnorm_bwd(sv["x"], g_mix, dh1, dx1, "rmsnorm_bwd")
        pending = (i, group) + half_sums(big, group)
    finish_reduce(pending[0], pending[1], pending[2],
                  _run_side(_exchange_side(pending[3]), "grads_to_chips_last"))

    small_shapes = [(depth, D), (depth, D), fox_b_f.shape, fox_q_norm_g.shape, fox_k_norm_g.shape,
                    (n_pool, D), (n_pool, D)]
    small = [jnp.stack(g_mix_rows)[:, 0], jnp.stack(g_ffn_rows)[:, 0], jnp.stack(g_bf), jnp.stack(g_gq),
             jnp.stack(g_gk), jnp.stack(g_pb), jnp.stack(g_ps)]
    (gr_mix, gr_ffn, gr_bf, gr_gq, gr_gk, gr_pb_full, gr_ps_full) = _unpack(
        _allreduce_small(_pack(small, _packed_rows(small_shapes)), "allreduce_small_grads"), small_shapes)
    gr_pb = lax.dynamic_slice(gr_pb_full, (0, chip * Dq), (n_pool, Dq))
    gr_ps = lax.dynamic_slice(gr_ps_full, (0, chip * Dq), (n_pool, Dq))

    fox_layers = [i for i in range(depth) if i % 2 == 0]
    pool_layers = [i for i in range(depth) if i % 2 == 1]
    gr_w_in = jnp.stack([red_mix[i][0].reshape(fox_w_in.shape[1:]) for i in fox_layers])
    gr_w_out = jnp.stack([red_mix[i][1].reshape(fox_w_out.shape[1:]) for i in fox_layers])
    gr_pool_w = jnp.stack([red_mix[i][0].reshape(pool_w.shape[1:]) for i in pool_layers])
    gr_gu = jnp.stack([red_ffn[i][0].reshape(ffn_w_gate_up.shape[1:]) for i in range(depth)])
    gr_down = jnp.stack([red_ffn[i][1].reshape(ffn_w_down.shape[1:]) for i in range(depth)])

    def update_big(w, g, m, v, name):
        flat = lambda a: a.reshape(-1, a.shape[-1])
        return [o.reshape(w.shape) for o in _adamw(flat(w), flat(g), flat(m), flat(v), name)]

    names = ["mix_norm_g", "ffn_norm_g", "fox_w_in", "fox_b_f", "fox_q_norm_g", "fox_k_norm_g", "fox_w_out",
             "pool_w", "pool_b", "pool_scale", "ffn_w_gate_up", "ffn_w_down"]
    ws = dict(zip(names, [mix_norm_g, ffn_norm_g, fox_w_in, fox_b_f, fox_q_norm_g, fox_k_norm_g, fox_w_out,
                          pool_w, pool_b, pool_scale, ffn_w_gate_up, ffn_w_down]))
    ms = dict(zip(names, [m_mix_norm_g, m_ffn_norm_g, m_fox_w_in, m_fox_b_f, m_fox_q_norm_g, m_fox_k_norm_g,
                          m_fox_w_out, m_pool_w, m_pool_b, m_pool_scale, m_ffn_w_gate_up, m_ffn_w_down]))
    vs = dict(zip(names, [v_mix_norm_g, v_ffn_norm_g, v_fox_w_in, v_fox_b_f, v_fox_q_norm_g, v_fox_k_norm_g,
                          v_fox_w_out, v_pool_w, v_pool_b, v_pool_scale, v_ffn_w_gate_up, v_ffn_w_down]))
    grads = dict(mix_norm_g=gr_mix, ffn_norm_g=gr_ffn, fox_w_in=gr_w_in, fox_b_f=gr_bf, fox_q_norm_g=gr_gq,
                 fox_k_norm_g=gr_gk, fox_w_out=gr_w_out, pool_w=gr_pool_w, pool_b=gr_pb, pool_scale=gr_ps,
                 ffn_w_gate_up=gr_gu, ffn_w_down=gr_down)
    big_names = ["fox_w_in", "fox_w_out", "pool_w", "ffn_w_gate_up", "ffn_w_down"]
    small_names = [n for n in names if n not in big_names]
    delta, new_m, new_v = {}, {}, {}
    for n in big_names:
        delta[n], new_m[n], new_v[n] = update_big(ws[n], grads[n], ms[n], vs[n], "adamw_" + n)
    shapes = [ws[n].shape for n in small_names]
    rows = _packed_rows(shapes)
    packed = _adamw(_pack([ws[n] for n in small_names], rows), _pack([grads[n] for n in small_names], rows),
                    _pack([ms[n] for n in small_names], rows), _pack([vs[n] for n in small_names], rows),
                    "adamw_small")
    for dst, pk in zip((delta, new_m, new_v), packed):
        for n, a in zip(small_names, _unpack(pk, shapes)):
            dst[n] = a

    return (loss, dcur[None], *[grads[n] for n in names], *[delta[n] for n in names],
            *[new_m[n] for n in names], *[new_v[n] for n in names])
```

```python
import functools

import numpy as np
import jax
import jax.numpy as jnp
from jax import lax
from jax.experimental import pallas as pl
from jax.experimental.pallas import tpu as pltpu

F32 = jnp.float32
BF16 = jnp.bfloat16
MESH = pl.DeviceIdType.MESH

HEAD_DIM = 128
RMS_EPS = 1e-6
NEG_INF = -1e30
POOL_WINDOWS = (2, 4, 8, 16)
POOL_HALO = 16
LANES = 128
N_CHIPS = 4
N_DEV = 8
VMEM_LIMIT = 56 * 1024 * 1024

ADAM_LR = 0.001
ADAM_B1 = 0.9
ADAM_B2 = 0.999
ADAM_EPS = 1e-08
ADAM_WD = 0.01
ADAM_STEP = 10

NN = (((1,), (0,)), ((), ()))
NT = (((1,), (1,)), ((), ()))
TN = (((0,), (0,)), ((), ()))


def _params(*sem):
    return pltpu.CompilerParams(dimension_semantics=sem if sem else None, vmem_limit_bytes=VMEM_LIMIT)


def _tile(n, pref, unit=LANES):
    best = None
    t = unit
    while t <= min(n, pref):
        if n % t == 0:
            best = t
        t += unit
    return best if best is not None else n


def _row_tile(rows, cols, itemsize, budget):
    best = None
    for t in range(16, rows + 1, 16):
        if rows % t == 0 and t * cols * itemsize <= budget:
            best = t
    return best if best is not None else rows


def _dot(a, b, dims):
    return lax.dot_general(a, b, dims, preferred_element_type=F32)


class _Side:
    def __init__(self, operands, out_shapes, sem_shapes, start, finish):
        self.operands, self.out_shapes, self.sem_shapes = list(operands), list(out_shapes), list(sem_shapes)
        self.start, self.finish = start, finish

    def split(self, refs, n_in, n_out, n_scratch):
        si, so, ss = len(self.operands), len(self.out_shapes), len(self.sem_shapes)
        refs = list(refs)
        cuts = [n_in, si, n_out, so, n_scratch, ss]
        parts, at = [], 0
        for c in cuts:
            parts.append(refs[at:at + c])
            at += c
        ins, s_ins, outs, s_outs, scratch, sems = parts
        return ins + outs + scratch, (s_ins, s_outs, sems)

    def bracket(self, side_refs, first, last, compute):
        @pl.when(first)
        def _():
            self.start(*side_refs)

        compute()

        @pl.when(last)
        def _():
            self.finish(*side_refs)


def _merge_sides(sides):
    sides = [s for s in sides if s is not None]
    if len(sides) < 2:
        return sides[0] if sides else None

    def each(ins, outs, sems, what):
        i = o = m = 0
        for s in sides:
            ni, no, nm = len(s.operands), len(s.out_shapes), len(s.sem_shapes)
            getattr(s, what)(ins[i:i + ni], outs[o:o + no], sems[m:m + nm])
            i, o, m = i + ni, o + no, m + nm

    return _Side([a for s in sides for a in s.operands], [a for s in sides for a in s.out_shapes],
                 [a for s in sides for a in s.sem_shapes],
                 lambda ins, outs, sems: each(ins, outs, sems, "start"),
                 lambda ins, outs, sems: each(ins, outs, sems, "finish"))


def _grid_ends(grid):
    ids = [pl.program_id(a) for a in range(len(grid))]
    first, last = ids[0] == 0, ids[0] == grid[0] - 1
    for pid, n in zip(ids[1:], grid[1:]):
        first, last = first & (pid == 0), last & (pid == n - 1)
    return first, last


def _host(side, body, grid, n_in, n_out, n_scratch):
    if side is None:
        return body

    def hosted(*refs):
        own, side_refs = side.split(refs, n_in, n_out, n_scratch)
        first, last = _grid_ends(grid)
        side.bracket(side_refs, first, last, lambda: body(*own))

    return hosted


def _matmul(a, b, mode, out_dtype, name, *, M, N, K, tm=1024, tn=1024, tk=2048, res=None,
            a_spec=None, b_spec=None, out_shards=None, side=None):
    tm, tn, tk = _tile(M, tm), _tile(N, tn), _tile(K, tk)
    nk = K // tk
    dims = {"nn": NN, "nt": NT, "tn": TN}[mode]
    if a_spec is None:
        a_spec = (pl.BlockSpec((tk, tm), lambda i, j, k: (k, i)) if mode == "tn"
                  else pl.BlockSpec((tm, tk), lambda i, j, k: (i, k)))
    else:
        a_spec = a_spec(tm, tn, tk)
    if b_spec is None:
        b_spec = (pl.BlockSpec((tn, tk), lambda i, j, k: (j, k)) if mode == "nt"
                  else pl.BlockSpec((tk, tn), lambda i, j, k: (k, j)))
    else:
        b_spec = b_spec(tm, tn, tk)
    if out_shards is None:
        o_spec = pl.BlockSpec((tm, tn), lambda i, j, k: (i, j))
        out_shape = jax.ShapeDtypeStruct((M, N), out_dtype)
    else:
        per = N // out_shards // tn
        o_spec = pl.BlockSpec((None, tm, tn), lambda i, j, k: (j // per, i, j % per))
        out_shape = jax.ShapeDtypeStruct((out_shards, M, N // out_shards), out_dtype)
    has_res = res is not None

    def body(a_ref, b_ref, *rest):
        if has_res:
            r_ref, o_ref, acc = rest
        else:
            o_ref, acc = rest

        def finish(total):
            if has_res:
                total = total + r_ref[...]
            o_ref[...] = total.astype(o_ref.dtype)

        part = _dot(a_ref[...], b_ref[...], dims)
        if nk == 1:
            finish(part)
        else:
            k = pl.program_id(2)

            @pl.when(k == 0)
            def _():
                acc[...] = part

            @pl.when(k > 0)
            def _():
                acc[...] += part

            @pl.when(k == nk - 1)
            def _():
                finish(acc[...])

    grid = (M // tm, N // tn, nk)
    operands = [a, b] + ([res] if has_res else [])
    in_specs = [a_spec, b_spec] + ([o_spec] if has_res else [])
    out_specs, out_shapes = [o_spec], [out_shape]
    scratch = [pltpu.VMEM((tm, tn) if nk > 1 else (8, LANES), F32)]
    sem = ("parallel", "parallel", "arbitrary")
    if side is not None:
        body = _host(side, body, grid, len(operands), 1, 1)
        operands, in_specs = operands + side.operands, in_specs + [ANY] * len(side.operands)
        out_specs, out_shapes = out_specs + [ANY] * len(side.out_shapes), out_shapes + side.out_shapes
        scratch = scratch + side.sem_shapes
        sem = ("arbitrary",) * 3
    outs = pl.pallas_call(
        body, name=name, grid=grid, in_specs=in_specs, out_specs=out_specs, out_shape=out_shapes,
        scratch_shapes=scratch, compiler_params=_params(*sem),
    )(*operands)
    return outs[0] if side is None else (outs[0], list(outs[1:]))


def _rmsnorm_fwd(x, g, out_dtype, name):
    S, D = x.shape
    tm = _tile(S, 512, 16)

    def body(x_ref, g_ref, o_ref):
        xv = x_ref[...]
        r = lax.rsqrt(jnp.mean(xv * xv, axis=-1, keepdims=True) + RMS_EPS)
        o_ref[...] = ((xv * r) * g_ref[...]).astype(o_ref.dtype)

    row = pl.BlockSpec((tm, D), lambda i: (i, 0))
    return pl.pallas_call(
        body, name=name, grid=(S // tm,), in_specs=[row, pl.BlockSpec((1, D), lambda i: (0, 0))],
        out_specs=row, out_shape=jax.ShapeDtypeStruct((S, D), out_dtype),
        compiler_params=_params("parallel"),
    )(x, g)


def _rmsnorm_bwd(x, g, dh, dres, name):
    S, D = x.shape
    tm = _tile(S, 256, 16)

    def body(x_ref, g_ref, dh_ref, dres_ref, dx_ref, dxb_ref, dg_ref):
        xv = x_ref[...]
        r = lax.rsqrt(jnp.mean(xv * xv, axis=-1, keepdims=True) + RMS_EPS)
        xhat = xv * r
        dhv = dh_ref[...]
        dxhat = dhv * g_ref[...]
        dx = dres_ref[...] + r * (dxhat - xhat * jnp.mean(dxhat * xhat, axis=-1, keepdims=True))
        dx_ref[...] = dx
        dxb_ref[...] = dx.astype(BF16)

        @pl.when(pl.program_id(0) == 0)
        def _():
            dg_ref[...] = jnp.zeros_like(dg_ref)

        dg_ref[...] += jnp.sum(dhv * xhat, axis=0, keepdims=True)

    row = pl.BlockSpec((tm, D), lambda i: (i, 0))
    vec = pl.BlockSpec((1, D), lambda i: (0, 0))
    return pl.pallas_call(
        body, name=name, grid=(S // tm,), in_specs=[row, vec, row, row], out_specs=[row, row, vec],
        out_shape=[jax.ShapeDtypeStruct((S, D), F32), jax.ShapeDtypeStruct((S, D), BF16),
                   jax.ShapeDtypeStruct((1, D), F32)],
        compiler_params=_params("arbitrary"),
    )(x, g, dh, dres)


def _loss_grad(y, target, name):
    S, D = y.shape
    tm = _tile(S, 256, 16)

    def body(y_ref, t_ref, dy_ref, dyb_ref, sq_ref):
        e = y_ref[...] - t_ref[...]
        dy = e / D
        dy_ref[...] = dy
        dyb_ref[...] = dy.astype(BF16)

        @pl.when(pl.program_id(0) == 0)
        def _():
            sq_ref[...] = jnp.zeros_like(sq_ref)

        total = jnp.sum(jnp.sum(e * e, axis=1, keepdims=True), axis=0, keepdims=True)
        sq_ref[...] += jnp.broadcast_to(total, sq_ref.shape)

    row = pl.BlockSpec((tm, D), lambda i: (i, 0))
    vec = pl.BlockSpec((1, LANES), lambda i: (0, 0))
    return pl.pallas_call(
        body, name=name, grid=(S // tm,), in_specs=[row, row], out_specs=[row, row, vec],
        out_shape=[jax.ShapeDtypeStruct((S, D), F32), jax.ShapeDtypeStruct((S, D), BF16),
                   jax.ShapeDtypeStruct((1, LANES), F32)],
        compiler_params=_params("arbitrary"),
    )(y, target)


def _ffn_up(hb, w_gu, name, side=None):
    S, D = hb.shape
    F = w_gu.shape[1] // 2
    tm, tn, tk = _tile(S, 1024), _tile(F, 512), _tile(D, 2048)
    nk, nj = D // tk, F // tn

    def body(a_ref, wg_ref, wu_ref, gu_ref, act_ref, accg, accu):
        pg = _dot(a_ref[...], wg_ref[...], NN)
        pu = _dot(a_ref[...], wu_ref[...], NN)

        def finish(g, u):
            gu_ref[0] = g
            gu_ref[1] = u
            act_ref[...] = ((g / (1.0 + jnp.exp(-g))) * u).astype(BF16)

        if nk == 1:
            finish(pg, pu)
        else:
            k = pl.program_id(2)

            @pl.when(k == 0)
            def _():
                accg[...] = pg
                accu[...] = pu

            @pl.when(k > 0)
            def _():
                accg[...] += pg
                accu[...] += pu

            @pl.when(k == nk - 1)
            def _():
                finish(accg[...], accu[...])

    acc_shape = (tm, tn) if nk > 1 else (8, LANES)
    grid = (S // tm, nj, nk)
    operands = [hb, w_gu, w_gu]
    in_specs = [pl.BlockSpec((tm, tk), lambda i, j, k: (i, k)),
                pl.BlockSpec((tk, tn), lambda i, j, k: (k, j)),
                pl.BlockSpec((tk, tn), lambda i, j, k: (k, j + nj))]
    out_specs = [pl.BlockSpec((2, tm, tn), lambda i, j, k: (0, i, j)),
                 pl.BlockSpec((tm, tn), lambda i, j, k: (i, j))]
    out_shapes = [jax.ShapeDtypeStruct((2, S, F), F32), jax.ShapeDtypeStruct((S, F), BF16)]
    scratch = [pltpu.VMEM(acc_shape, F32), pltpu.VMEM(acc_shape, F32)]
    sem = ("parallel", "parallel", "arbitrary")
    if side is not None:
        body = _host(side, body, grid, 3, 2, 2)
        operands, in_specs = operands + side.operands, in_specs + [ANY] * len(side.operands)
        out_specs, out_shapes = out_specs + [ANY] * len(side.out_shapes), out_shapes + side.out_shapes
        scratch = scratch + side.sem_shapes
        sem = ("arbitrary",) * 3
    outs = pl.pallas_call(
        body, name=name, grid=grid, in_specs=in_specs, out_specs=out_specs, out_shape=out_shapes,
        scratch_shapes=scratch, compiler_params=_params(*sem),
    )(*operands)
    return outs[0], outs[1], list(outs[2:])


def _ffn_act_bwd(dyb, w_down, gu, name):
    S, D = dyb.shape
    F = w_down.shape[0]
    tm, tn, tk = _tile(S, 1024), _tile(F, 512), _tile(D, 2048)
    nk = D // tk

    def body(a_ref, w_ref, gu_ref, dgu_ref, acc):
        part = _dot(a_ref[...], w_ref[...], NT)

        def finish(dact):
            g = gu_ref[0]
            u = gu_ref[1]
            sg = 1.0 / (1.0 + jnp.exp(-g))
            dgu_ref[0] = (dact * u * (sg * (1.0 + g * (1.0 - sg)))).astype(BF16)
            dgu_ref[1] = (dact * (g * sg)).astype(BF16)

        if nk == 1:
            finish(part)
        else:
            k = pl.program_id(2)

            @pl.when(k == 0)
            def _():
                acc[...] = part

            @pl.when(k > 0)
            def _():
                acc[...] += part

            @pl.when(k == nk - 1)
            def _():
                finish(acc[...])

    gu_spec = pl.BlockSpec((2, tm, tn), lambda i, j, k: (0, i, j))
    return pl.pallas_call(
        body, name=name, grid=(S // tm, F // tn, nk),
        in_specs=[pl.BlockSpec((tm, tk), lambda i, j, k: (i, k)),
                  pl.BlockSpec((tn, tk), lambda i, j, k: (j, k)), gu_spec],
        out_specs=gu_spec, out_shape=jax.ShapeDtypeStruct((2, S, F), BF16),
        scratch_shapes=[pltpu.VMEM((tm, tn) if nk > 1 else (8, LANES), F32)],
        compiler_params=_params("parallel", "parallel", "arbitrary"),
    )(dyb, w_down, gu)


LOG2E = 1.4426950408889634
Q_PRESCALE = HEAD_DIM ** -0.5 * LOG2E
AUX_A = 0
AUX_B = 3


def _aux_block(lane, minus=None, minus_at=None, ones_at=None):
    out = jnp.zeros(lane.shape, BF16)
    if minus is not None:
        x = -minus
        hi = x.astype(BF16)
        r1 = x - hi.astype(F32)
        mid = r1.astype(BF16)
        lo = (r1 - mid.astype(F32)).astype(BF16)
        for n, piece in enumerate((hi, mid, lo)):
            out = jnp.where(lane == minus_at + n, piece, out)
    if ones_at is not None:
        out = jnp.where((lane >= ones_at) & (lane < ones_at + 3), jnp.ones(lane.shape, BF16), out)
    return out


def _qkv_post(proj, gq, gk, cum, name):
    S, D3 = proj.shape
    D = D3 // 3
    tm = _tile(S, 256, 16)

    def body(q_ref, k_ref, v_ref, gq_ref, gk_ref, c_ref, qn_ref, kn_ref, vb_ref, ak_ref):
        lane = lax.broadcasted_iota(jnp.int32, (tm, HEAD_DIM), 1)
        for src, g_ref, dst, mult in ((q_ref, gq_ref, qn_ref, Q_PRESCALE), (k_ref, gk_ref, kn_ref, None)):
            for h in range(D // HEAD_DIM):
                cols = slice(h * HEAD_DIM, (h + 1) * HEAD_DIM)
                t = src[:, cols]
                r = lax.rsqrt(jnp.mean(t * t, axis=-1, keepdims=True) + RMS_EPS)
                n = (t * r) * g_ref[...]
                dst[:, cols] = (n if mult is None else n * mult).astype(BF16)
        for h in range(D // HEAD_DIM):
            ch = jnp.broadcast_to(c_ref[:, h:h + 1] * LOG2E, (tm, HEAD_DIM))
            ak_ref[:, h * HEAD_DIM:(h + 1) * HEAD_DIM] = _aux_block(lane, ch, AUX_A, AUX_B)
        vb_ref[...] = v_ref[...].astype(BF16)

    part = lambda n: pl.BlockSpec((tm, D), lambda i: (i, n))
    vec = pl.BlockSpec((1, HEAD_DIM), lambda i: (0, 0))
    row = pl.BlockSpec((tm, D), lambda i: (i, 0))
    return pl.pallas_call(
        body, name=name, grid=(S // tm,),
        in_specs=[part(0), part(1), part(2), vec, vec, pl.BlockSpec((tm, LANES), lambda i: (i, 0))],
        out_specs=[row, row, row, row], out_shape=[jax.ShapeDtypeStruct((S, D), BF16)] * 4,
        compiler_params=_params("parallel"),
    )(proj, proj, proj, gq, gk, cum)


def _qkv_post_bwd(proj, dqx, dkx, dvb, gq, gk, name):
    S, D = dvb.shape
    tm = _tile(S, 128, 16)

    def body(q_ref, k_ref, dqx_ref, dkx_ref, dvb_ref, gq_ref, gk_ref, dp_ref, dgq_ref, dgk_ref, dc_ref):
        @pl.when(pl.program_id(0) == 0)
        def _():
            dgq_ref[...] = jnp.zeros_like(dgq_ref)
            dgk_ref[...] = jnp.zeros_like(dgk_ref)

        lane = lax.broadcasted_iota(jnp.int32, (tm, LANES), 1)
        dc = jnp.zeros((tm, LANES), F32)
        for h in range(D // HEAD_DIM):
            at = (2 * h + 1) * HEAD_DIM
            diff = dqx_ref[:, at + AUX_B:at + AUX_B + 1] - dkx_ref[:, at + AUX_A:at + AUX_A + 1]
            dc = jnp.where(lane == h, jnp.broadcast_to(diff, (tm, LANES)), dc)
        dc_ref[...] = dc

        for n, (src, dsrc, g_ref, dg_ref, mult) in enumerate(
                ((q_ref, dqx_ref, gq_ref, dgq_ref, HEAD_DIM ** -0.5), (k_ref, dkx_ref, gk_ref, dgk_ref, 1.0 / LOG2E))):
            dg = jnp.zeros((1, HEAD_DIM), F32)
            for h in range(D // HEAD_DIM):
                cols = slice(h * HEAD_DIM, (h + 1) * HEAD_DIM)
                t = src[:, cols]
                r = lax.rsqrt(jnp.mean(t * t, axis=-1, keepdims=True) + RMS_EPS)
                that = t * r
                dn = dsrc[:, 2 * h * HEAD_DIM:(2 * h + 1) * HEAD_DIM] * mult
                dhat = dn * g_ref[...]
                dt = r * (dhat - that * jnp.mean(dhat * that, axis=-1, keepdims=True))
                dp_ref[:, n * D + h * HEAD_DIM:n * D + (h + 1) * HEAD_DIM] = dt.astype(BF16)
                dg = dg + jnp.sum(dn * that, axis=0, keepdims=True)
            dg_ref[...] += dg
        dp_ref[:, 2 * D:3 * D] = dvb_ref[...]

    part = lambda n: pl.BlockSpec((tm, D), lambda i: (i, n))
    row = pl.BlockSpec((tm, D), lambda i: (i, 0))
    vec = pl.BlockSpec((1, HEAD_DIM), lambda i: (0, 0))
    wide = pl.BlockSpec((tm, 2 * D), lambda i: (i, 0))
    return pl.pallas_call(
        body, name=name, grid=(S // tm,), in_specs=[part(0), part(1), wide, wide, row, vec, vec],
        out_specs=[pl.BlockSpec((tm, 3 * D), lambda i: (i, 0)), vec, vec,
                   pl.BlockSpec((tm, LANES), lambda i: (i, 0))],
        out_shape=[jax.ShapeDtypeStruct((S, 3 * D), BF16), jax.ShapeDtypeStruct((1, HEAD_DIM), F32),
                   jax.ShapeDtypeStruct((1, HEAD_DIM), F32), jax.ShapeDtypeStruct((S, LANES), F32)],
        compiler_params=_params("arbitrary"),
    )(proj, proj, dqx, dkx, dvb, gq, gk)


def _log_sigmoid(z):
    return -(jnp.maximum(-z, 0.0) + jnp.log(1.0 + jnp.exp(-jnp.abs(z))))


def _fgate_fwd(fl, bf, name):
    S = fl.shape[0]
    T = _tile(S, 256, 16)

    def body(fl_ref, bf_ref, c_ref, carry):
        @pl.when(pl.program_id(0) == 0)
        def _():
            carry[...] = jnp.zeros_like(carry)

        logf = _log_sigmoid(fl_ref[...] + bf_ref[...])
        r = lax.broadcasted_iota(jnp.int32, (T, T), 0)
        c = lax.broadcasted_iota(jnp.int32, (T, T), 1)
        tri = (r >= c).astype(F32)
        cum = jnp.dot(tri, logf, precision=lax.Precision.HIGHEST, preferred_element_type=F32) + carry[...]
        c_ref[...] = cum
        carry[...] = cum[T - 1:T, :]

    row = pl.BlockSpec((T, LANES), lambda i: (i, 0))
    return pl.pallas_call(
        body, name=name, grid=(S // T,), in_specs=[row, pl.BlockSpec((1, LANES), lambda i: (0, 0))],
        out_specs=row, out_shape=jax.ShapeDtypeStruct((S, LANES), F32),
        scratch_shapes=[pltpu.VMEM((1, LANES), F32)], compiler_params=_params("arbitrary"),
    )(fl, bf)


def _fgate_bwd(dc, fl, bf, name):
    S = fl.shape[0]
    T = _tile(S, 256, 16)
    nb = S // T

    def body(dc_ref, fl_ref, bf_ref, dfl_ref, dbf_ref, carry):
        @pl.when(pl.program_id(0) == 0)
        def _():
            carry[...] = jnp.zeros_like(carry)
            dbf_ref[...] = jnp.zeros_like(dbf_ref)

        r = lax.broadcasted_iota(jnp.int32, (T, T), 0)
        c = lax.broadcasted_iota(jnp.int32, (T, T), 1)
        triu = (c >= r).astype(F32)
        dlogf = jnp.dot(triu, dc_ref[...], precision=lax.Precision.HIGHEST,
                        preferred_element_type=F32) + carry[...]
        carry[...] = dlogf[0:1, :]
        z = fl_ref[...] + bf_ref[...]
        dz = dlogf * (1.0 / (1.0 + jnp.exp(z)))
        dfl_ref[...] = dz.astype(BF16)
        dbf_ref[...] += jnp.sum(dz, axis=0, keepdims=True)

    row = pl.BlockSpec((T, LANES), lambda i: (nb - 1 - i, 0))
    vec = pl.BlockSpec((1, LANES), lambda i: (0, 0))
    return pl.pallas_call(
        body, name=name, grid=(nb,), in_specs=[row, row, vec], out_specs=[row, vec],
        out_shape=[jax.ShapeDtypeStruct((S, LANES), BF16), jax.ShapeDtypeStruct((1, LANES), F32)],
        scratch_shapes=[pltpu.VMEM((1, LANES), F32)], compiler_params=_params("arbitrary"),
    )(dc, fl, bf)


def _attn_logits(q_ref, aq, k_ref, ak, masked, T):
    qf = jnp.concatenate([q_ref[...], aq], axis=1)
    kf = jnp.concatenate([k_ref[...], ak], axis=1)
    s = _dot(qf, kf, NT)
    if masked:
        r = lax.broadcasted_iota(jnp.int32, (T, T), 0)
        c = lax.broadcasted_iota(jnp.int32, (T, T), 1)
        s = jnp.where(r >= c, s, NEG_INF)
    return s, qf, kf


def _causal_pairs(nb, q_major):
    pairs = ([(i, j) for i in range(nb) for j in range(i + 1)] if q_major
             else [(i, j) for j in range(nb) for i in range(j, nb)])
    return (jnp.asarray(np.array([p[0] for p in pairs], np.int32)),
            jnp.asarray(np.array([p[1] for p in pairs], np.int32)))


def _flash_fwd(q2, kn, vb, aux_k, name, side=None):
    S, D = q2.shape
    H = D // HEAD_DIM
    T = _tile(S, 1024)
    nb = S // T
    q_idx, k_idx = _causal_pairs(nb, True)

    def body(qi_ref, kj_ref, q_ref, k_ref, v_ref, ak_ref, o_ref, aq_ref, m_s, acc_s):
        t = pl.program_id(1)
        i = qi_ref[t]
        j = kj_ref[t]
        lane = lax.broadcasted_iota(jnp.int32, (T, HEAD_DIM), 1)

        @pl.when(j == 0)
        def _():
            m_s[...] = jnp.full_like(m_s, NEG_INF)
            acc_s[...] = jnp.zeros_like(acc_s)

        def step(masked):
            s, _, _ = _attn_logits(q_ref, _aux_block(lane, ones_at=AUX_A), k_ref, ak_ref[...], masked, T)
            m_prev = m_s[...]
            m_new = jnp.maximum(m_prev, jnp.max(s, axis=1, keepdims=True))
            p = jnp.exp2(s - jnp.tile(m_new, (1, T // HEAD_DIM)))
            alpha = jnp.exp2(m_prev - m_new)
            vf = jnp.concatenate([v_ref[...], jnp.ones((T, HEAD_DIM), BF16)], axis=1)
            acc_s[...] = jnp.tile(alpha, (1, 2)) * acc_s[...] + _dot(p.astype(BF16), vf, NN)
            m_s[...] = m_new

        @pl.when(j < i)
        def _():
            step(False)

        @pl.when(j == i)
        def _():
            step(True)
            l = acc_s[:, HEAD_DIM:]
            o_ref[...] = (acc_s[:, :HEAD_DIM] / l).astype(BF16)
            aq_ref[...] = _aux_block(lane, m_s[...] + jnp.log2(l), AUX_B, AUX_A)

    qspec = pl.BlockSpec((T, HEAD_DIM), lambda h, t, qi, kj: (qi[t], h))
    kspec = pl.BlockSpec((T, HEAD_DIM), lambda h, t, qi, kj: (kj[t], h))
    grid = (H, int(q_idx.shape[0]))
    operands = [q_idx, k_idx, q2, kn, vb, aux_k]
    in_specs, out_specs = [qspec, kspec, kspec, kspec], [qspec, qspec]
    out_shapes = [jax.ShapeDtypeStruct((S, D), BF16), jax.ShapeDtypeStruct((S, D), BF16)]
    scratch = [pltpu.VMEM((T, HEAD_DIM), F32), pltpu.VMEM((T, 2 * HEAD_DIM), F32)]
    sem = ("parallel", "arbitrary")
    if side is not None:
        body = _host(side, body, grid, 6, 2, 2)
        operands, in_specs = operands + side.operands, in_specs + [ANY] * len(side.operands)
        out_specs, out_shapes = out_specs + [ANY] * len(side.out_shapes), out_shapes + side.out_shapes
        scratch = scratch + side.sem_shapes
        sem = ("arbitrary", "arbitrary")
    grid_spec = pltpu.PrefetchScalarGridSpec(
        num_scalar_prefetch=2, grid=grid, in_specs=in_specs, out_specs=out_specs, scratch_shapes=scratch)
    outs = pl.pallas_call(
        body, name=name, grid_spec=grid_spec, out_shape=out_shapes, compiler_params=_params(*sem),
    )(*operands)
    return outs[0], outs[1], list(outs[2:])


def _attn_delta(do, o, name):
    S, D = do.shape
    H = D // HEAD_DIM
    tm = _tile(S, 256, 16)

    def body(do_ref, o_ref, ad_ref, dob_ref):
        lane = lax.broadcasted_iota(jnp.int32, (tm, HEAD_DIM), 1)
        for h in range(H):
            cols = slice(h * HEAD_DIM, (h + 1) * HEAD_DIM)
            delta = jnp.sum(do_ref[:, cols] * o_ref[:, cols].astype(F32), axis=1, keepdims=True)
            ad_ref[:, cols] = _aux_block(lane, jnp.broadcast_to(delta, (tm, HEAD_DIM)), AUX_A)
        dob_ref[...] = do_ref[...].astype(BF16)

    row = pl.BlockSpec((tm, D), lambda i: (i, 0))
    return pl.pallas_call(
        body, name=name, grid=(S // tm,), in_specs=[row, row], out_specs=[row, row],
        out_shape=[jax.ShapeDtypeStruct((S, D), BF16), jax.ShapeDtypeStruct((S, D), BF16)],
        compiler_params=_params("parallel"),
    )(do, o)


def _flash_bwd(q2, kn, vb, dob, aux_q, aux_k, aux_do, name, side=None):
    S, D = q2.shape
    H = D // HEAD_DIM
    T = _tile(S, 1024)
    nb = S // T
    q_idx, k_idx = _causal_pairs(nb, False)

    def body(qi_ref, kj_ref, q_ref, k_ref, v_ref, do_ref, aq_ref, ak_ref, ad_ref,
             dqx_ref, dkx_ref, dv_ref, dv_acc):
        t = pl.program_id(1)
        i = qi_ref[t]
        j = kj_ref[t]
        lane = lax.broadcasted_iota(jnp.int32, (T, HEAD_DIM), 1)

        @pl.when(t == 0)
        def _():
            dqx_ref[...] = jnp.zeros_like(dqx_ref)

        @pl.when(i == j)
        def _():
            dkx_ref[...] = jnp.zeros_like(dkx_ref)
            dv_acc[...] = jnp.zeros_like(dv_acc)

        def step(masked):
            s, qf, kf = _attn_logits(q_ref, aq_ref[...], k_ref, ak_ref[...], masked, T)
            p = jnp.exp2(s)
            dof = jnp.concatenate([do_ref[...], ad_ref[...]], axis=1)
            vf = jnp.concatenate([v_ref[...], _aux_block(lane, ones_at=AUX_A)], axis=1)
            ds = p * _dot(dof, vf, NT)
            dsb = ds.astype(BF16)
            dv_acc[...] += _dot(p.astype(BF16), do_ref[...], TN)
            dkx_ref[...] += _dot(dsb, qf, TN)
            rows = pl.ds(pl.multiple_of(i * T, T), T)
            dqx_ref[rows, :] += _dot(dsb, kf, NN)

        @pl.when(i > j)
        def _():
            step(False)

        @pl.when(i == j)
        def _():
            step(True)

        @pl.when(i == nb - 1)
        def _():
            dv_ref[...] = dv_acc[...].astype(BF16)

    qspec = pl.BlockSpec((T, HEAD_DIM), lambda h, t, qi, kj: (qi[t], h))
    kspec = pl.BlockSpec((T, HEAD_DIM), lambda h, t, qi, kj: (kj[t], h))
    grid = (H, int(q_idx.shape[0]))
    operands = [q_idx, k_idx, q2, kn, vb, dob, aux_q, aux_k, aux_do]
    in_specs = [qspec, kspec, kspec, qspec, qspec, kspec, qspec]
    out_specs = [pl.BlockSpec((S, 2 * HEAD_DIM), lambda h, t, qi, kj: (0, h)),
                 pl.BlockSpec((T, 2 * HEAD_DIM), lambda h, t, qi, kj: (kj[t], h)), kspec]
    out_shapes = [jax.ShapeDtypeStruct((S, 2 * D), F32), jax.ShapeDtypeStruct((S, 2 * D), F32),
                  jax.ShapeDtypeStruct((S, D), BF16)]
    scratch = [pltpu.VMEM((T, HEAD_DIM), F32)]
    sem = ("parallel", "arbitrary")
    if side is not None:
        body = _host(side, body, grid, 9, 3, 1)
        operands, in_specs = operands + side.operands, in_specs + [ANY] * len(side.operands)
        out_specs, out_shapes = out_specs + [ANY] * len(side.out_shapes), out_shapes + side.out_shapes
        scratch = scratch + side.sem_shapes
        sem = ("arbitrary", "arbitrary")
    grid_spec = pltpu.PrefetchScalarGridSpec(
        num_scalar_prefetch=2, grid=grid, in_specs=in_specs, out_specs=out_specs, scratch_shapes=scratch)
    outs = pl.pallas_call(
        body, name=name, grid_spec=grid_spec, out_shape=out_shapes, compiler_params=_params(*sem),
    )(*operands)
    return outs[0], outs[1], outs[2], list(outs[3:])


def _pool_counts(first_row, tm, win):
    t = first_row + lax.broadcasted_iota(jnp.int32, (tm, 1), 0)
    return jnp.minimum(t + 1, win).astype(F32)


def _pool_fwd(h, x, wp, b, scale, name):
    S, D = h.shape
    G = D // len(POOL_WINDOWS)
    tm = _tile(S, 256, 16)

    def body(h_ref, halo_ref, x_ref, w_ref, b_ref, s_ref, y_ref, o_ref, ext):
        i = pl.program_id(0)
        ext[POOL_HALO:, :] = h_ref[...]

        @pl.when(i == 0)
        def _():
            ext[0:POOL_HALO, :] = jnp.zeros((POOL_HALO, D), F32)

        @pl.when(i > 0)
        def _():
            ext[0:POOL_HALO, :] = halo_ref[...]

        for g, win in enumerate(POOL_WINDOWS):
            cols = slice(g * G, (g + 1) * G)
            tot = ext[POOL_HALO:POOL_HALO + tm, cols]
            for k in range(1, win):
                tot = tot + ext[POOL_HALO - k:POOL_HALO - k + tm, cols]
            y = (tot / _pool_counts(i * tm, tm, win) - h_ref[:, cols]).astype(BF16)
            y_ref[:, cols] = y
            z = _dot(y, w_ref[g], NN)
            o_ref[:, cols] = x_ref[:, cols] + (z + b_ref[:, cols]) * s_ref[:, cols]

    row = pl.BlockSpec((tm, D), lambda i: (i, 0))
    vec = pl.BlockSpec((1, D), lambda i: (0, 0))
    halo = pl.BlockSpec((POOL_HALO, D), lambda i: (jnp.maximum(i * (tm // POOL_HALO) - 1, 0), 0))
    return pl.pallas_call(
        body, name=name, grid=(S // tm,),
        in_specs=[row, halo, row, pl.BlockSpec((len(POOL_WINDOWS), G, G), lambda i: (0, 0, 0)), vec, vec],
        out_specs=[row, row],
        out_shape=[jax.ShapeDtypeStruct((S, D), BF16), jax.ShapeDtypeStruct((S, D), F32)],
        scratch_shapes=[pltpu.VMEM((tm + POOL_HALO, D), F32)], compiler_params=_params("parallel"),
    )(h, h, x, wp, b, scale)


def _pool_bwd_mix(dout, yb, wp, b, scale, name):
    S, D = dout.shape
    NG = len(POOL_WINDOWS)
    G = D // NG
    tm = _tile(S, 256, 16)

    def body(do_ref, y_ref, w_ref, b_ref, s_ref, dyc_ref, dw_ref, db_ref, ds_ref):
        i = pl.program_id(0)

        @pl.when(i == 0)
        def _():
            dw_ref[...] = jnp.zeros_like(dw_ref)
            db_ref[...] = jnp.zeros_like(db_ref)
            ds_ref[...] = jnp.zeros_like(ds_ref)

        for g, win in enumerate(POOL_WINDOWS):
            cols = slice(g * G, (g + 1) * G)
            y = y_ref[:, cols]
            dz = do_ref[:, cols]
            zb = _dot(y, w_ref[g], NN) + b_ref[:, cols]
            ds_ref[:, cols] += jnp.sum(dz * zb, axis=0, keepdims=True)
            dzb = dz * s_ref[:, cols]
            db_ref[:, cols] += jnp.sum(dzb, axis=0, keepdims=True)
            dzb16 = dzb.astype(BF16)
            dw_ref[g] += _dot(y, dzb16, TN)
            dy = _dot(dzb16, w_ref[g], NT)
            dyc_ref[:, cols] = dy / _pool_counts(i * tm, tm, win)

    row = pl.BlockSpec((tm, D), lambda i: (i, 0))
    vec = pl.BlockSpec((1, D), lambda i: (0, 0))
    wspec = pl.BlockSpec((NG, G, G), lambda i: (0, 0, 0))
    return pl.pallas_call(
        body, name=name, grid=(S // tm,), in_specs=[row, row, wspec, vec, vec],
        out_specs=[row, wspec, vec, vec],
        out_shape=[jax.ShapeDtypeStruct((S, D), F32), jax.ShapeDtypeStruct((NG, G, G), F32),
                   jax.ShapeDtypeStruct((1, D), F32), jax.ShapeDtypeStruct((1, D), F32)],
        compiler_params=_params("arbitrary"),
    )(dout, yb, wp, b, scale)


def _pool_bwd_window(dyc, name):
    S, D = dyc.shape
    G = D // len(POOL_WINDOWS)
    tm = _tile(S, 256, 16)
    nb = S // tm

    def body(d_ref, halo_ref, dh_ref, ext):
        i = pl.program_id(0)
        ext[0:tm, :] = d_ref[...]

        @pl.when(i == nb - 1)
        def _():
            ext[tm:tm + POOL_HALO, :] = jnp.zeros((POOL_HALO, D), F32)

        @pl.when(i < nb - 1)
        def _():
            ext[tm:tm + POOL_HALO, :] = halo_ref[...]

        for g, win in enumerate(POOL_WINDOWS):
            cols = slice(g * G, (g + 1) * G)
            tot = ext[0:tm, cols] * (1.0 - _pool_counts(i * tm, tm, win))
            for k in range(1, win):
                tot = tot + ext[k:k + tm, cols]
            dh_ref[:, cols] = tot

    row = pl.BlockSpec((tm, D), lambda i: (i, 0))
    halo = pl.BlockSpec((POOL_HALO, D),
                        lambda i: (jnp.minimum((i + 1) * (tm // POOL_HALO), S // POOL_HALO - 1), 0))
    return pl.pallas_call(
        body, name=name, grid=(nb,), in_specs=[row, halo], out_specs=row,
        out_shape=jax.ShapeDtypeStruct((S, D), F32),
        scratch_shapes=[pltpu.VMEM((tm + POOL_HALO, D), F32)], compiler_params=_params("parallel"),
    )(dyc, dyc)


def _adamw(w, g, m, v, name):
    R, C = w.shape
    tr = _row_tile(R, C, 4, 1 << 20)

    def body(w_ref, g_ref, m_ref, v_ref, d_ref, nm_ref, nv_ref):
        gv = g_ref[...]
        m_new = ADAM_B1 * m_ref[...] + (1.0 - ADAM_B1) * gv
        v_new = ADAM_B2 * v_ref[...] + (1.0 - ADAM_B2) * (gv * gv)
        m_hat = m_new / (1.0 - ADAM_B1 ** ADAM_STEP)
        v_hat = v_new / (1.0 - ADAM_B2 ** ADAM_STEP)
        d_ref[...] = -ADAM_LR * (m_hat / (jnp.sqrt(v_hat) + ADAM_EPS) + ADAM_WD * w_ref[...])
        nm_ref[...] = m_new
        nv_ref[...] = v_new

    row = pl.BlockSpec((tr, C), lambda i: (i, 0))
    return pl.pallas_call(
        body, name=name, grid=(R // tr,), in_specs=[row] * 4, out_specs=[row] * 3,
        out_shape=[jax.ShapeDtypeStruct((R, C), F32)] * 3, compiler_params=_params("parallel"),
    )(w, g, m, v)


def _sum_core_halves(g, r1, c_idx, name):
    _, _, Rh, C = g.shape
    tr = _row_tile(Rh, C, 4, 2 << 20)

    def body(c_ref, g_ref, r_ref, o_ref, ob_ref):
        total = g_ref[...] + r_ref[...]
        o_ref[...] = total
        ob_ref[...] = total.astype(BF16)

    piece = pl.BlockSpec((None, tr, C), lambda s, r, c_ref: (s, r, 0))
    grid_spec = pltpu.PrefetchScalarGridSpec(
        num_scalar_prefetch=1, grid=(N_CHIPS, Rh // tr),
        in_specs=[pl.BlockSpec((None, None, tr, C), lambda s, r, c_ref: (s, c_ref[0], r, 0)), piece],
        out_specs=[piece, piece])
    return pl.pallas_call(
        body, name=name, grid_spec=grid_spec,
        out_shape=[jax.ShapeDtypeStruct((N_CHIPS, Rh, C), F32), jax.ShapeDtypeStruct((N_CHIPS, Rh, C), BF16)],
        compiler_params=_params("parallel", "parallel"),
    )(c_idx, g, r1)


def _sum_chips(h, r2, place, name):
    _, Rh, C = h.shape
    tr = _row_tile(Rh, C, 4, 1 << 20)

    def body(place_ref, h_ref, r_ref, o_ref):
        o_ref[...] = ((h_ref[...] + r_ref[0].astype(F32)) + r_ref[1].astype(F32)) + r_ref[2].astype(F32)

    grid_spec = pltpu.PrefetchScalarGridSpec(
        num_scalar_prefetch=1, grid=(Rh // tr,),
        in_specs=[pl.BlockSpec((None, tr, C), lambda r, pr: (pr[0], r, 0)),
                  pl.BlockSpec((N_CHIPS - 1, tr, C), lambda r, pr: (0, r, 0))],
        out_specs=pl.BlockSpec((None, tr, C), lambda r, pr: (pr[1], r, 0)))
    return pl.pallas_call(
        body, name=name, grid_spec=grid_spec, out_shape=jax.ShapeDtypeStruct((2, Rh, C), F32),
        compiler_params=_params("parallel"),
    )(place, h, r2)


ANY = pl.BlockSpec(memory_space=pl.ANY)


def _place():
    x, y, c = lax.axis_index("x"), lax.axis_index("y"), lax.axis_index("c")
    chips = [(1 - x, y), (x, 1 - y), (1 - x, 1 - y)]
    return x, y, c, chips, [2 * cx + cy for cx, cy in chips]


def _run_side(side, name):
    def body(*refs):
        _, side_refs = side.split(refs, 0, 0, 0)
        side.start(*side_refs)
        side.finish(*side_refs)

    return pl.pallas_call(
        body, name=name, in_specs=[ANY] * len(side.operands), out_specs=[ANY] * len(side.out_shapes),
        out_shape=side.out_shapes, scratch_shapes=side.sem_shapes,
    )(*side.operands)


def _allgather_side(shards):
    n = len(shards)

    def copy(sems, t, k, src, dst, to):
        return pltpu.make_async_remote_copy(src_ref=src, dst_ref=dst, send_sem=sems[0].at[t, k],
                                            recv_sem=sems[1].at[t, k], device_id=to, device_id_type=MESH)

    def first_copies(ins, outs, sems):
        x, y, c, chips, _ = _place()
        me = 2 * x + y
        return [copy(sems, t, j, ins[t].at[c], outs[t].at[me, c], (*chip, c))
                for t in range(n) for j, chip in enumerate(chips)]

    def start(ins, outs, sems):
        for cp in first_copies(ins, outs, sems):
            cp.start()

    def finish(ins, outs, sems):
        x, y, c, chips, chip_idx = _place()
        sibling = (x, y, 1 - c)
        passed = []
        for t in range(n):
            for j, chip in enumerate(chips):
                landed = outs[t].at[chip_idx[j], c]
                copy(sems, t, j, landed, landed, (*chip, c)).wait_recv()
                fw = copy(sems, t, 3 + j, landed, landed, sibling)
                fw.start()
                passed.append(fw)
        for t in range(n):
            for j in range(3):
                other = outs[t].at[chip_idx[j], 1 - c]
                copy(sems, t, 3 + j, other, other, sibling).wait_recv()
        for cp in first_copies(ins, outs, sems) + passed:
            cp.wait_send()

    return _Side(shards, [jax.ShapeDtypeStruct((N_CHIPS,) + s.shape, s.dtype) for s in shards],
                 [pltpu.SemaphoreType.DMA((n, 6)), pltpu.SemaphoreType.DMA((n, 6))], start, finish)


def _send_other_half_to_sibling(gs, name):
    n = len(gs)

    def body(*refs):
        ins, outs = refs[:n], refs[n:2 * n]
        send_sems, recv_sems = refs[2 * n:]
        x, y, c, _, _ = _place()
        copies = []
        for t in range(n):
            for s in range(N_CHIPS):
                cp = pltpu.make_async_remote_copy(
                    src_ref=ins[t].at[s, 1 - c], dst_ref=outs[t].at[s], send_sem=send_sems.at[t, s],
                    recv_sem=recv_sems.at[t, s], device_id=(x, y, 1 - c), device_id_type=MESH)
                cp.start()
                copies.append(cp)
        for cp in copies:
            cp.wait()

    return pl.pallas_call(
        body, name=name, in_specs=[ANY] * n, out_specs=[ANY] * n,
        out_shape=[jax.ShapeDtypeStruct((N_CHIPS,) + g.shape[2:], g.dtype) for g in gs],
        scratch_shapes=[pltpu.SemaphoreType.DMA((n, N_CHIPS)), pltpu.SemaphoreType.DMA((n, N_CHIPS))],
    )(*gs)


def _exchange_side(hs):
    n = len(hs)

    def copies(ins, outs, sems):
        x, y, c, chips, chip_idx = _place()
        return [pltpu.make_async_remote_copy(
            src_ref=ins[t].at[chip_idx[j]], dst_ref=outs[t].at[j], send_sem=sems[0].at[t, j],
            recv_sem=sems[1].at[t, j], device_id=(*chip, c), device_id_type=MESH)
            for t in range(n) for j, chip in enumerate(chips)]

    def start(ins, outs, sems):
        for cp in copies(ins, outs, sems):
            cp.start()

    def finish(ins, outs, sems):
        for cp in copies(ins, outs, sems):
            cp.wait()

    return _Side(hs, [jax.ShapeDtypeStruct((N_CHIPS - 1,) + h.shape[1:], h.dtype) for h in hs],
                 [pltpu.SemaphoreType.DMA((n, 3)), pltpu.SemaphoreType.DMA((n, 3))], start, finish)


def _join_core_halves(bufs, name):
    n = len(bufs)

    def body(*refs):
        outs = refs[n:2 * n]
        send_sems, recv_sems = refs[2 * n:]
        x, y, c, _, _ = _place()
        sends = []
        for t in range(n):
            cp = pltpu.make_async_remote_copy(
                src_ref=outs[t].at[c], dst_ref=outs[t].at[c], send_sem=send_sems.at[t], recv_sem=recv_sems.at[t],
                device_id=(x, y, 1 - c), device_id_type=MESH)
            cp.start()
            sends.append(cp)
        for t in range(n):
            other = outs[t].at[1 - c]
            pltpu.make_async_remote_copy(
                src_ref=other, dst_ref=other, send_sem=send_sems.at[t], recv_sem=recv_sems.at[t],
                device_id=(x, y, 1 - c), device_id_type=MESH).wait_recv()
        for cp in sends:
            cp.wait_send()

    return pl.pallas_call(
        body, name=name, in_specs=[ANY] * n, out_specs=[ANY] * n,
        out_shape=[jax.ShapeDtypeStruct(b.shape, b.dtype) for b in bufs],
        input_output_aliases={t: t for t in range(n)},
        scratch_shapes=[pltpu.SemaphoreType.DMA((n,)), pltpu.SemaphoreType.DMA((n,))],
    )(*bufs)


def _allreduce_small(vec, name):
    R = vec.shape[0]

    def body(v_ref, o_ref, buf, send_sems, recv_sems):
        x, y, c = lax.axis_index("x"), lax.axis_index("y"), lax.axis_index("c")
        me = 4 * x + 2 * y + c
        buf[me] = v_ref[...]
        peers = []
        for k in range(1, N_DEV):
            px = 1 - x if k & 4 else x
            py = 1 - y if k & 2 else y
            pc = 1 - c if k & 1 else c
            peers.append(((px, py, pc), 4 * px + 2 * py + pc))
        sends = []
        for k, (peer, _) in enumerate(peers):
            cp = pltpu.make_async_remote_copy(
                src_ref=v_ref, dst_ref=buf.at[me], send_sem=send_sems.at[k], recv_sem=recv_sems.at[k],
                device_id=peer, device_id_type=MESH)
            cp.start()
            sends.append(cp)
        for k, (peer, idx) in enumerate(peers):
            pltpu.make_async_remote_copy(
                src_ref=v_ref, dst_ref=buf.at[idx], send_sem=send_sems.at[k], recv_sem=recv_sems.at[k],
                device_id=peer, device_id_type=MESH).wait_recv()
        for cp in sends:
            cp.wait_send()
        total = buf[0]
        for d in range(1, N_DEV):
            total = total + buf[d]
        o_ref[...] = total

    vm = pl.BlockSpec(memory_space=pltpu.VMEM)
    return pl.pallas_call(
        body, name=name, in_specs=[vm], out_specs=vm, out_shape=jax.ShapeDtypeStruct((R, LANES), F32),
        scratch_shapes=[pltpu.VMEM((N_DEV, R, LANES), F32), pltpu.SemaphoreType.DMA((N_DEV - 1,)),
                        pltpu.SemaphoreType.DMA((N_DEV - 1,))],
    )(vec)


def _halves(a):
    lead = 1
    for d in a.shape[:-1]:
        lead *= d
    return a.reshape(2, lead // 2, a.shape[-1])


def _cols_from_shards(g):
    return g.transpose(1, 0, 2).reshape(g.shape[1], N_CHIPS * g.shape[2])


def _shards_from_cols(w):
    return w.reshape(w.shape[0], N_CHIPS, w.shape[1] // N_CHIPS).transpose(1, 0, 2)


def _pack(parts, rows):
    flat = jnp.concatenate([p.reshape(-1).astype(F32) for p in parts])
    return jnp.pad(flat, (0, rows * LANES - flat.shape[0])).reshape(rows, LANES)


def _unpack(packed, shapes):
    flat = packed.reshape(-1)
    out, off = [], 0
    for s in shapes:
        n = 1
        for d in s:
            n *= d
        out.append(flat[off:off + n].reshape(s))
        off += n
    return out


def _packed_rows(shapes):
    n = 0
    for s in shapes:
        k = 1
        for d in s:
            k *= d
        n += k
    return -(-n // (8 * LANES)) * 8


def _pad_lanes(a):
    return jnp.pad(a, ((0, 0), (0, LANES - a.shape[1])))


def kernel(x, mix_norm_g, ffn_norm_g, fox_w_in, fox_b_f, fox_q_norm_g, fox_k_norm_g, fox_w_out, pool_w, pool_b, pool_scale, ffn_w_gate_up, ffn_w_down, loss_target, m_mix_norm_g, m_ffn_norm_g, m_fox_w_in, m_fox_b_f, m_fox_q_norm_g, m_fox_k_norm_g, m_fox_w_out, m_pool_w, m_pool_b, m_pool_scale, m_ffn_w_gate_up, m_ffn_w_down, v_mix_norm_g, v_ffn_norm_g, v_fox_w_in, v_fox_b_f, v_fox_q_norm_g, v_fox_k_norm_g, v_fox_w_out, v_pool_w, v_pool_b, v_pool_scale, v_ffn_w_gate_up, v_ffn_w_down):
    _, S, D = x.shape
    H = D // HEAD_DIM
    depth = mix_norm_g.shape[0]
    n_pool = pool_w.shape[0]
    NG = len(POOL_WINDOWS)
    G = D // NG
    F = ffn_w_down.shape[1] * N_CHIPS
    ax, ay, ac = lax.axis_index("x"), lax.axis_index("y"), lax.axis_index("c")
    chip = 2 * ax + ay
    c_idx = jnp.reshape(ac, (1,)).astype(jnp.int32)
    place = jnp.stack([chip, ac]).astype(jnp.int32)
    xs = x[0]
    target = loss_target[0]

    Dq = D // N_CHIPS
    small_fwd_shapes = [(n_pool, D), (n_pool, D)]
    placed = []
    for p in (pool_b, pool_scale):
        full = lax.dynamic_update_slice(jnp.zeros((n_pool, D), F32), p, (0, chip * Dq))
        placed.append(jnp.where(ac == 0, full, jnp.zeros_like(full)))
    pool_b_full, pool_scale_full = _unpack(
        _allreduce_small(_pack(placed, _packed_rows(small_fwd_shapes)), "gather_pool_vectors"), small_fwd_shapes)

    def weight_shards(i):
        mixer = ([fox_w_in[i // 2], fox_w_out[i // 2]] if i % 2 == 0 else [pool_w[i // 2]])
        return [_halves(w.astype(BF16)) for w in mixer + [ffn_w_gate_up[i], ffn_w_down[i]]]

    def with_own(shards, got):
        return [lax.dynamic_update_slice(g, sh[None], (chip, 0, 0, 0)) for g, sh in zip(got, shards)]

    def mixer_weights(i, shards, got):
        got = with_own(shards, got)
        if i % 2 == 1:
            return {"w_pool": got[0].reshape(N_CHIPS, NG, G // N_CHIPS, G).transpose(1, 0, 2, 3).reshape(NG, G, G)}
        w_in = _cols_from_shards(got[0].reshape(N_CHIPS, D, -1))
        return {"w_qkv": w_in[:, :3 * D], "w_f": _pad_lanes(w_in[:, 3 * D:]), "w_out": got[1].reshape(D, D)}

    def ffn_weights(shards, got):
        got = with_own(shards, got)
        return {"w_gu": _cols_from_shards(got[0].reshape(N_CHIPS, D, -1)), "w_down": got[1].reshape(F, D)}

    def layer_weights(i, shards, got):
        return {**mixer_weights(i, shards[:-2], got[:-2]), **ffn_weights(shards[-2:], got[-2:])}

    shards0 = weight_shards(0)
    weights = [mixer_weights(0, shards0[:-2], _run_side(_allgather_side(shards0[:-2]), "allgather_weights_first"))]
    own_ffn = shards0[-2:]

    saved = []
    cur = xs
    for i in range(depth):
        j = i // 2
        lw = weights[i]
        sv = {"x": cur}
        next_shards = weight_shards(i + 1) if i + 1 < depth else None
        gather_next = _allgather_side(next_shards) if next_shards is not None else None
        g_mix = mix_norm_g[i][None]
        if i % 2 == 0:
            h1b = _rmsnorm_fwd(cur, g_mix, BF16, "rmsnorm_fwd_bf16")
            proj = _matmul(h1b, lw["w_qkv"], "nn", F32, "qkv_proj", M=S, N=3 * D, K=D)
            fl = _matmul(h1b, lw["w_f"], "nn", F32, "forget_proj", M=S, N=LANES, K=D)
            bf = _pad_lanes(fox_b_f[j][None])
            cum = _fgate_fwd(fl, bf, "forget_cumsum")
            q2, kn, vb, aux_k = _qkv_post(proj, fox_q_norm_g[j][None], fox_k_norm_g[j][None], cum, "qk_norm")
            gather_own = _allgather_side(own_ffn) if own_ffn is not None else None
            o, aux_q, got = _flash_fwd(q2, kn, vb, aux_k, "fox_attention_fwd",
                                       side=_merge_sides([gather_own, gather_next]))
            if own_ffn is not None:
                lw.update(ffn_weights(own_ffn, got[:2]))
                got, own_ffn = got[2:], None
            gather_next = None
            x1 = _matmul(o, lw["w_out"], "nn", F32, "attn_out_proj", M=S, N=D, K=D, res=cur)
            sv.update(h1b=h1b, proj=proj, fl=fl, bf=bf, q2=q2, kn=kn, vb=vb, aux_k=aux_k, aux_q=aux_q, o=o)
        else:
            h1 = _rmsnorm_fwd(cur, g_mix, F32, "rmsnorm_fwd_f32")
            yb, x1 = _pool_fwd(h1, cur, lw["w_pool"], pool_b_full[j][None], pool_scale_full[j][None], "pool_fwd")
            sv.update(yb=yb)
        h2b = _rmsnorm_fwd(x1, ffn_norm_g[i][None], BF16, "rmsnorm_fwd_bf16")
        if gather_next is not None:
            gu, act, got = _ffn_up(h2b, lw["w_gu"], "ffn_gate_up_gather", side=gather_next)
        else:
            gu, act, _ = _ffn_up(h2b, lw["w_gu"], "ffn_gate_up")
        if next_shards is not None:
            weights.append(layer_weights(i + 1, next_shards, got))
        x2 = _matmul(act, lw["w_down"], "nn", F32, "ffn_down", M=S, N=D, K=F, tk=2816, res=x1)
        sv.update(x1=x1, h2b=h2b, gu=gu, act=act)
        saved.append(sv)
        cur = x2

    dcur, dcur_b, sq = _loss_grad(cur, target, "loss_grad")
    loss = lax.psum(sq[0, 0] * (0.5 / D), ("x", "y", "c"))

    g_mix_rows, g_ffn_rows = [None] * depth, [None] * depth
    g_bf, g_gq, g_gk = [None] * (depth - n_pool), [None] * (depth - n_pool), [None] * (depth - n_pool)
    g_pb, g_ps = [None] * n_pool, [None] * n_pool
    red_mix, red_ffn = [None] * depth, [None] * depth

    def half_sums(big, group):
        gs = [b.reshape(N_CHIPS, 2, -1, b.shape[-1]) for b in big]
        r1 = _send_other_half_to_sibling(gs, f"grads_to_sibling_{group}")
        sums = [_sum_core_halves(g, r, c_idx, "sum_core_halves") for g, r in zip(gs, r1)]
        return [s[0] for s in sums], [s[1] for s in sums]

    def finish_reduce(layer, group, hs, r2):
        rs = [_sum_chips(h, r, place, "sum_chips") for h, r in zip(hs, r2)]
        joined = _join_core_halves(rs, f"grads_join_{group}")
        if group != "ffn":
            red_mix[layer] = joined if group == "mix" else joined[:-2]
        if group != "mix":
            red_ffn[layer] = joined[-2:]

    pending = None
    for i in reversed(range(depth)):
        j = i // 2
        lw, sv = weights[i], saved[i]
        dgu = _ffn_act_bwd(dcur_b, lw["w_down"], sv["gu"], "ffn_act_bwd")
        d_w_down = _matmul(sv["act"], dcur_b, "tn", F32, "ffn_down_dw", M=F, N=D, K=S, tm=1408)
        dh2 = _matmul(
            dgu, lw["w_gu"], "nt", F32, "ffn_up_dx", M=S, N=D, K=2 * F, tk=_tile(F, 2816),
            a_spec=lambda tm, tn, tk: pl.BlockSpec(
                (None, tm, tk), lambda i_, j_, k_: (k_ // (F // tk), i_, k_ % (F // tk))))
        d_w_gu = _matmul(
            sv["h2b"], dgu, "tn", F32, "ffn_up_dw" if pending is None else "ffn_up_dw_exchange",
            M=D, N=2 * F, K=S, tn=_tile(F, 1408), out_shards=N_CHIPS,
            b_spec=lambda tm, tn, tk: pl.BlockSpec(
                (None, tk, tn), lambda i_, j_, k_: (j_ // (F // tn), k_, j_ % (F // tn))),
            side=None if pending is None else _exchange_side(pending[3]))
        if pending is not None:
            d_w_gu, r2 = d_w_gu
            finish_reduce(pending[0], pending[1], pending[2], r2)
        ffn_big = [d_w_gu, d_w_down.reshape(N_CHIPS, F // N_CHIPS, D)]
        dx1, dx1b, g_ffn_rows[i] = _rmsnorm_bwd(sv["x1"], ffn_norm_g[i][None], dh2, dcur, "rmsnorm_bwd")
        g_mix = mix_norm_g[i][None]
        if i % 2 == 0:
            do = _matmul(dx1b, lw["w_out"], "nt", F32, "attn_out_dx", M=S, N=D, K=D)
            d_w_out = _matmul(sv["o"], dx1b, "tn", F32, "attn_out_dw", M=D, N=D, K=S)
            aux_do, dob = _attn_delta(do, sv["o"], "attn_delta")
            ffn_hs, ffn_hb = half_sums(ffn_big, "ffn")
            dqx, dkx, dvb, r2 = _flash_bwd(sv["q2"], sv["kn"], sv["vb"], dob, sv["aux_q"], sv["aux_k"], aux_do,
                                           "fox_attention_bwd", side=_exchange_side(ffn_hb))
            finish_reduce(i, "ffn", ffn_hs, r2)
            dproj, dgq, dgk, dc = _qkv_post_bwd(sv["proj"], dqx, dkx, dvb, fox_q_norm_g[j][None],
                                                fox_k_norm_g[j][None], "qk_norm_bwd")
            dfl, dbf = _fgate_bwd(dc, sv["fl"], sv["bf"], "forget_cumsum_bwd")
            d_w_qkv = _matmul(sv["h1b"], dproj, "tn", F32, "qkv_dw", M=D, N=3 * D, K=S)
            d_w_f = _matmul(sv["h1b"], dfl, "tn", F32, "forget_dw", M=D, N=LANES, K=S)
            dh1f = _matmul(dfl, lw["w_f"], "nt", F32, "forget_dx", M=S, N=D, K=LANES)
            dh1 = _matmul(dproj, lw["w_qkv"], "nt", F32, "qkv_dx", M=S, N=D, K=3 * D, res=dh1f)
            d_w_in = jnp.concatenate([d_w_qkv, d_w_f[:, :H]], axis=1)
            big, group = [_shards_from_cols(d_w_in), d_w_out.reshape(N_CHIPS, D // N_CHIPS, D)], "mix"
            g_bf[j], g_gq[j], g_gk[j] = dbf[0, :H], dgq[0], dgk[0]
        else:
            dyc, d_wp, dpb, dps = _pool_bwd_mix(dx1, sv["yb"], lw["w_pool"], pool_b_full[j][None],
                                                pool_scale_full[j][None], "pool_bwd_mix")
            dh1 = _pool_bwd_window(dyc, "pool_bwd_window")
            big, group = [d_wp.reshape(NG, N_CHIPS, G // N_CHIPS, G).transpose(1, 0, 2, 3)] + ffn_big, "all"
            g_pb[j], g_ps[j] = dpb[0], dps[0]
        dcur, dcur_b, g_mix_rows[i] = _rmsnorm_bwd(sv["x"], g_mix, dh1, dx1, "rmsnorm_bwd")
        pending = (i, group) + half_sums(big, group)
    finish_reduce(pending[0], pending[1], pending[2],
                  _run_side(_exchange_side(pending[3]), "grads_to_chips_last"))

    small_shapes = [(depth, D), (depth, D), fox_b_f.shape, fox_q_norm_g.shape, fox_k_norm_g.shape,
                    (n_pool, D), (n_pool, D)]
    small = [jnp.stack(g_mix_rows)[:, 0], jnp.stack(g_ffn_rows)[:, 0], jnp.stack(g_bf), jnp.stack(g_gq),
             jnp.stack(g_gk), jnp.stack(g_pb), jnp.stack(g_ps)]
    (gr_mix, gr_ffn, gr_bf, gr_gq, gr_gk, gr_pb_full, gr_ps_full) = _unpack(
        _allreduce_small(_pack(small, _packed_rows(small_shapes)), "allreduce_small_grads"), small_shapes)
    gr_pb = lax.dynamic_slice(gr_pb_full, (0, chip * Dq), (n_pool, Dq))
    gr_ps = lax.dynamic_slice(gr_ps_full, (0, chip * Dq), (n_pool, Dq))

    fox_layers = [i for i in range(depth) if i % 2 == 0]
    pool_layers = [i for i in range(depth) if i % 2 == 1]
    gr_w_in = jnp.stack([red_mix[i][0].reshape(fox_w_in.shape[1:]) for i in fox_layers])
    gr_w_out = jnp.stack([red_mix[i][1].reshape(fox_w_out.shape[1:]) for i in fox_layers])
    gr_pool_w = jnp.stack([red_mix[i][0].reshape(pool_w.shape[1:]) for i in pool_layers])
    gr_gu = jnp.stack([red_ffn[i][0].reshape(ffn_w_gate_up.shape[1:]) for i in range(depth)])
    gr_down = jnp.stack([red_ffn[i][1].reshape(ffn_w_down.shape[1:]) for i in range(depth)])

    def update_big(w, g, m, v, name):
        flat = lambda a: a.reshape(-1, a.shape[-1])
        return [o.reshape(w.shape) for o in _adamw(flat(w), flat(g), flat(m), flat(v), name)]

    names = ["mix_norm_g", "ffn_norm_g", "fox_w_in", "fox_b_f", "fox_q_norm_g", "fox_k_norm_g", "fox_w_out",
             "pool_w", "pool_b", "pool_scale", "ffn_w_gate_up", "ffn_w_down"]
    ws = dict(zip(names, [mix_norm_g, ffn_norm_g, fox_w_in, fox_b_f, fox_q_norm_g, fox_k_norm_g, fox_w_out,
                          pool_w, pool_b, pool_scale, ffn_w_gate_up, ffn_w_down]))
    ms = dict(zip(names, [m_mix_norm_g, m_ffn_norm_g, m_fox_w_in, m_fox_b_f, m_fox_q_norm_g, m_fox_k_norm_g,
                          m_fox_w_out, m_pool_w, m_pool_b, m_pool_scale, m_ffn_w_gate_up, m_ffn_w_down]))
    vs = dict(zip(names, [v_mix_norm_g, v_ffn_norm_g, v_fox_w_in, v_fox_b_f, v_fox_q_norm_g, v_fox_k_norm_g,
                          v_fox_w_out, v_pool_w, v_pool_b, v_pool_scale, v_ffn_w_gate_up, v_ffn_w_down]))
    grads = dict(mix_norm_g=gr_mix, ffn_norm_g=gr_ffn, fox_w_in=gr_w_in, fox_b_f=gr_bf, fox_q_norm_g=gr_gq,
                 fox_k_norm_g=gr_gk, fox_w_out=gr_w_out, pool_w=gr_pool_w, pool_b=gr_pb, pool_scale=gr_ps,
                 ffn_w_gate_up=gr_gu, ffn_w_down=gr_down)
    big_names = ["fox_w_in", "fox_w_out", "pool_w", "ffn_w_gate_up", "ffn_w_down"]
    small_names = [n for n in names if n not in big_names]
    delta, new_m, new_v = {}, {}, {}
    for n in big_names:
        delta[n], new_m[n], new_v[n] = update_big(ws[n], grads[n], ms[n], vs[n], "adamw_" + n)
    shapes = [ws[n].shape for n in small_names]
    rows = _packed_rows(shapes)
    packed = _adamw(_pack([ws[n] for n in small_names], rows), _pack([grads[n] for n in small_names], rows),
                    _pack([ms[n] for n in small_names], rows), _pack([vs[n] for n in small_names], rows),
                    "adamw_small")
    for dst, pk in zip((delta, new_m, new_v), packed):
        for n, a in zip(small_names, _unpack(pk, shapes)):
            dst[n] = a

    return (loss, dcur[None], *[grads[n] for n in names], *[delta[n] for n in names],
            *[new_m[n] for n in names], *[new_v[n] for n in names])
```

```python
import functools

import numpy as np
import jax
import jax.numpy as jnp
from jax import lax
from jax.experimental import pallas as pl
from jax.experimental.pallas import tpu as pltpu

F32 = jnp.float32
BF16 = jnp.bfloat16
MESH = pl.DeviceIdType.MESH

HEAD_DIM = 128
RMS_EPS = 1e-6
NEG_INF = -1e30
POOL_WINDOWS = (2, 4, 8, 16)
POOL_HALO = 16
LANES = 128
N_CHIPS = 4
N_DEV = 8
VMEM_LIMIT = 56 * 1024 * 1024

ADAM_LR = 0.001
ADAM_B1 = 0.9
ADAM_B2 = 0.999
ADAM_EPS = 1e-08
ADAM_WD = 0.01
ADAM_STEP = 10

NN = (((1,), (0,)), ((), ()))
NT = (((1,), (1,)), ((), ()))
TN = (((0,), (0,)), ((), ()))


def _params(*sem):
    return pltpu.CompilerParams(dimension_semantics=sem if sem else None, vmem_limit_bytes=VMEM_LIMIT)


def _tile(n, pref, unit=LANES):
    best = None
    t = unit
    while t <= min(n, pref):
        if n % t == 0:
            best = t
        t += unit
    return best if best is not None else n


def _row_tile(rows, cols, itemsize, budget):
    best = None
    for t in range(16, rows + 1, 16):
        if rows % t == 0 and t * cols * itemsize <= budget:
            best = t
    return best if best is not None else rows


def _dot(a, b, dims):
    return lax.dot_general(a, b, dims, preferred_element_type=F32)


class _Side:
    def __init__(self, operands, out_shapes, sem_shapes, start, finish):
        self.operands, self.out_shapes, self.sem_shapes = list(operands), list(out_shapes), list(sem_shapes)
        self.start, self.finish = start, finish

    def split(self, refs, n_in, n_out, n_scratch):
        si, so, ss = len(self.operands), len(self.out_shapes), len(self.sem_shapes)
        refs = list(refs)
        cuts = [n_in, si, n_out, so, n_scratch, ss]
        parts, at = [], 0
        for c in cuts:
            parts.append(refs[at:at + c])
            at += c
        ins, s_ins, outs, s_outs, scratch, sems = parts
        return ins + outs + scratch, (s_ins, s_outs, sems)

    def bracket(self, side_refs, first, last, compute):
        @pl.when(first)
        def _():
            self.start(*side_refs)

        compute()

        @pl.when(last)
        def _():
            self.finish(*side_refs)


def _merge_sides(sides):
    sides = [s for s in sides if s is not None]
    if len(sides) < 2:
        return sides[0] if sides else None

    def each(ins, outs, sems, what):
        i = o = m = 0
        for s in sides:
            ni, no, nm = len(s.operands), len(s.out_shapes), len(s.sem_shapes)
            getattr(s, what)(ins[i:i + ni], outs[o:o + no], sems[m:m + nm])
            i, o, m = i + ni, o + no, m + nm

    return _Side([a for s in sides for a in s.operands], [a for s in sides for a in s.out_shapes],
                 [a for s in sides for a in s.sem_shapes],
                 lambda ins, outs, sems: each(ins, outs, sems, "start"),
                 lambda ins, outs, sems: each(ins, outs, sems, "finish"))


def _grid_ends(grid):
    ids = [pl.program_id(a) for a in range(len(grid))]
    first, last = ids[0] == 0, ids[0] == grid[0] - 1
    for pid, n in zip(ids[1:], grid[1:]):
        first, last = first & (pid == 0), last & (pid == n - 1)
    return first, last


def _host(side, body, grid, n_in, n_out, n_scratch):
    if side is None:
        return body

    def hosted(*refs):
        own, side_refs = side.split(refs, n_in, n_out, n_scratch)
        first, last = _grid_ends(grid)
        side.bracket(side_refs, first, last, lambda: body(*own))

    return hosted


def _matmul(a, b, mode, out_dtype, name, *, M, N, K, tm=1024, tn=1024, tk=2048, res=None,
            a_spec=None, b_spec=None, out_shards=None, side=None):
    tm, tn, tk = _tile(M, tm), _tile(N, tn), _tile(K, tk)
    nk = K // tk
    dims = {"nn": NN, "nt": NT, "tn": TN}[mode]
    if a_spec is None:
        a_spec = (pl.BlockSpec((tk, tm), lambda i, j, k: (k, i)) if mode == "tn"
                  else pl.BlockSpec((tm, tk), lambda i, j, k: (i, k)))
    else:
        a_spec = a_spec(tm, tn, tk)
    if b_spec is None:
        b_spec = (pl.BlockSpec((tn, tk), lambda i, j, k: (j, k)) if mode == "nt"
                  else pl.BlockSpec((tk, tn), lambda i, j, k: (k, j)))
    else:
        b_spec = b_spec(tm, tn, tk)
    if out_shards is None:
        o_spec = pl.BlockSpec((tm, tn), lambda i, j, k: (i, j))
        out_shape = jax.ShapeDtypeStruct((M, N), out_dtype)
    else:
        per = N // out_shards // tn
        o_spec = pl.BlockSpec((None, tm, tn), lambda i, j, k: (j // per, i, j % per))
        out_shape = jax.ShapeDtypeStruct((out_shards, M, N // out_shards), out_dtype)
    has_res = res is not None

    def body(a_ref, b_ref, *rest):
        if has_res:
            r_ref, o_ref, acc = rest
        else:
            o_ref, acc = rest

        def finish(total):
            if has_res:
                total = total + r_ref[...]
            o_ref[...] = total.astype(o_ref.dtype)

        part = _dot(a_ref[...], b_ref[...], dims)
        if nk == 1:
            finish(part)
        else:
            k = pl.program_id(2)

            @pl.when(k == 0)
            def _():
                acc[...] = part

            @pl.when(k > 0)
            def _():
                acc[...] += part

            @pl.when(k == nk - 1)
            def _():
                finish(acc[...])

    grid = (M // tm, N // tn, nk)
    operands = [a, b] + ([res] if has_res else [])
    in_specs = [a_spec, b_spec] + ([o_spec] if has_res else [])
    out_specs, out_shapes = [o_spec], [out_shape]
    scratch = [pltpu.VMEM((tm, tn) if nk > 1 else (8, LANES), F32)]
    sem = ("parallel", "parallel", "arbitrary")
    if side is not None:
        body = _host(side, body, grid, len(operands), 1, 1)
        operands, in_specs = operands + side.operands, in_specs + [ANY] * len(side.operands)
        out_specs, out_shapes = out_specs + [ANY] * len(side.out_shapes), out_shapes + side.out_shapes
        scratch = scratch + side.sem_shapes
        sem = ("arbitrary",) * 3
    outs = pl.pallas_call(
        body, name=name, grid=grid, in_specs=in_specs, out_specs=out_specs, out_shape=out_shapes,
        scratch_shapes=scratch, compiler_params=_params(*sem),
    )(*operands)
    return outs[0] if side is None else (outs[0], list(outs[1:]))


def _rmsnorm_fwd(x, g, out_dtype, name):
    S, D = x.shape
    tm = _tile(S, 512, 16)

    def body(x_ref, g_ref, o_ref):
        xv = x_ref[...]
        r = lax.rsqrt(jnp.mean(xv * xv, axis=-1, keepdims=True) + RMS_EPS)
        o_ref[...] = ((xv * r) * g_ref[...]).astype(o_ref.dtype)

    row = pl.BlockSpec((tm, D), lambda i: (i, 0))
    return pl.pallas_call(
        body, name=name, grid=(S // tm,), in_specs=[row, pl.BlockSpec((1, D), lambda i: (0, 0))],
        out_specs=row, out_shape=jax.ShapeDtypeStruct((S, D), out_dtype),
        compiler_params=_params("parallel"),
    )(x, g)


def _rmsnorm_bwd(x, g, dh, dres, name):
    S, D = x.shape
    tm = _tile(S, 256, 16)

    def body(x_ref, g_ref, dh_ref, dres_ref, dx_ref, dxb_ref, dg_ref):
        xv = x_ref[...]
        r = lax.rsqrt(jnp.mean(xv * xv, axis=-1, keepdims=True) + RMS_EPS)
        xhat = xv * r
        dhv = dh_ref[...]
        dxhat = dhv * g_ref[...]
        dx = dres_ref[...] + r * (dxhat - xhat * jnp.mean(dxhat * xhat, axis=-1, keepdims=True))
        dx_ref[...] = dx
        dxb_ref[...] = dx.astype(BF16)

        @pl.when(pl.program_id(0) == 0)
        def _():
            dg_ref[...] = jnp.zeros_like(dg_ref)

        dg_ref[...] += jnp.sum(dhv * xhat, axis=0, keepdims=True)

    row = pl.BlockSpec((tm, D), lambda i: (i, 0))
    vec = pl.BlockSpec((1, D), lambda i: (0, 0))
    return pl.pallas_call(
        body, name=name, grid=(S // tm,), in_specs=[row, vec, row, row], out_specs=[row, row, vec],
        out_shape=[jax.ShapeDtypeStruct((S, D), F32), jax.ShapeDtypeStruct((S, D), BF16),
                   jax.ShapeDtypeStruct((1, D), F32)],
        compiler_params=_params("arbitrary"),
    )(x, g, dh, dres)


def _loss_grad(y, target, name):
    S, D = y.shape
    tm = _tile(S, 256, 16)

    def body(y_ref, t_ref, dy_ref, dyb_ref, sq_ref):
        e = y_ref[...] - t_ref[...]
        dy = e / D
        dy_ref[...] = dy
        dyb_ref[...] = dy.astype(BF16)

        @pl.when(pl.program_id(0) == 0)
        def _():
            sq_ref[...] = jnp.zeros_like(sq_ref)

        total = jnp.sum(jnp.sum(e * e, axis=1, keepdims=True), axis=0, keepdims=True)
        sq_ref[...] += jnp.broadcast_to(total, sq_ref.shape)

    row = pl.BlockSpec((tm, D), lambda i: (i, 0))
    vec = pl.BlockSpec((1, LANES), lambda i: (0, 0))
    return pl.pallas_call(
        body, name=name, grid=(S // tm,), in_specs=[row, row], out_specs=[row, row, vec],
        out_shape=[jax.ShapeDtypeStruct((S, D), F32), jax.ShapeDtypeStruct((S, D), BF16),
                   jax.ShapeDtypeStruct((1, LANES), F32)],
        compiler_params=_params("arbitrary"),
    )(y, target)


def _ffn_up(hb, w_gu, name, side=None):
    S, D = hb.shape
    F = w_gu.shape[1] // 2
    tm, tn, tk = _tile(S, 1024), _tile(F, 512), _tile(D, 2048)
    nk, nj = D // tk, F // tn

    def body(a_ref, wg_ref, wu_ref, gu_ref, act_ref, accg, accu):
        pg = _dot(a_ref[...], wg_ref[...], NN)
        pu = _dot(a_ref[...], wu_ref[...], NN)

        def finish(g, u):
            gu_ref[0] = g
            gu_ref[1] = u
            act_ref[...] = ((g / (1.0 + jnp.exp(-g))) * u).astype(BF16)

        if nk == 1:
            finish(pg, pu)
        else:
            k = pl.program_id(2)

            @pl.when(k == 0)
            def _():
                accg[...] = pg
                accu[...] = pu

            @pl.when(k > 0)
            def _():
                accg[...] += pg
                accu[...] += pu

            @pl.when(k == nk - 1)
            def _():
                finish(accg[...], accu[...])

    acc_shape = (tm, tn) if nk > 1 else (8, LANES)
    grid = (S // tm, nj, nk)
    operands = [hb, w_gu, w_gu]
    in_specs = [pl.BlockSpec((tm, tk), lambda i, j, k: (i, k)),
                pl.BlockSpec((tk, tn), lambda i, j, k: (k, j)),
                pl.BlockSpec((tk, tn), lambda i, j, k: (k, j + nj))]
    out_specs = [pl.BlockSpec((2, tm, tn), lambda i, j, k: (0, i, j)),
                 pl.BlockSpec((tm, tn), lambda i, j, k: (i, j))]
    out_shapes = [jax.ShapeDtypeStruct((2, S, F), F32), jax.ShapeDtypeStruct((S, F), BF16)]
    scratch = [pltpu.VMEM(acc_shape, F32), pltpu.VMEM(acc_shape, F32)]
    sem = ("parallel", "parallel", "arbitrary")
    if side is not None:
        body = _host(side, body, grid, 3, 2, 2)
        operands, in_specs = operands + side.operands, in_specs + [ANY] * len(side.operands)
        out_specs, out_shapes = out_specs + [ANY] * len(side.out_shapes), out_shapes + side.out_shapes
        scratch = scratch + side.sem_shapes
        sem = ("arbitrary",) * 3
    outs = pl.pallas_call(
        body, name=name, grid=grid, in_specs=in_specs, out_specs=out_specs, out_shape=out_shapes,
        scratch_shapes=scratch, compiler_params=_params(*sem),
    )(*operands)
    return outs[0], outs[1], list(outs[2:])


def _ffn_act_bwd(dyb, w_down, gu, name):
    S, D = dyb.shape
    F = w_down.shape[0]
    tm, tn, tk = _tile(S, 1024), _tile(F, 512), _tile(D, 2048)
    nk = D // tk

    def body(a_ref, w_ref, gu_ref, dgu_ref, acc):
        part = _dot(a_ref[...], w_ref[...], NT)

        def finish(dact):
            g = gu_ref[0]
            u = gu_ref[1]
            sg = 1.0 / (1.0 + jnp.exp(-g))
            dgu_ref[0] = (dact * u * (sg * (1.0 + g * (1.0 - sg)))).astype(BF16)
            dgu_ref[1] = (dact * (g * sg)).astype(BF16)

        if nk == 1:
            finish(part)
        else:
            k = pl.program_id(2)

            @pl.when(k == 0)
            def _():
                acc[...] = part

            @pl.when(k > 0)
            def _():
                acc[...] += part

            @pl.when(k == nk - 1)
            def _():
                finish(acc[...])

    gu_spec = pl.BlockSpec((2, tm, tn), lambda i, j, k: (0, i, j))
    return pl.pallas_call(
        body, name=name, grid=(S // tm, F // tn, nk),
        in_specs=[pl.BlockSpec((tm, tk), lambda i, j, k: (i, k)),
                  pl.BlockSpec((tn, tk), lambda i, j, k: (j, k)), gu_spec],
        out_specs=gu_spec, out_shape=jax.ShapeDtypeStruct((2, S, F), BF16),
        scratch_shapes=[pltpu.VMEM((tm, tn) if nk > 1 else (8, LANES), F32)],
        compiler_params=_params("parallel", "parallel", "arbitrary"),
    )(dyb, w_down, gu)


LOG2E = 1.4426950408889634
Q_PRESCALE = HEAD_DIM ** -0.5 * LOG2E
AUX_A = 0
AUX_B = 3


def _aux_block(lane, minus=None, minus_at=None, ones_at=None):
    out = jnp.zeros(lane.shape, BF16)
    if minus is not None:
        x = -minus
        hi = x.astype(BF16)
        r1 = x - hi.astype(F32)
        mid = r1.astype(BF16)
        lo = (r1 - mid.astype(F32)).astype(BF16)
        for n, piece in enumerate((hi, mid, lo)):
            out = jnp.where(lane == minus_at + n, piece, out)
    if ones_at is not None:
        out = jnp.where((lane >= ones_at) & (lane < ones_at + 3), jnp.ones(lane.shape, BF16), out)
    return out


def _qkv_post(proj, gq, gk, cum, name):
    S, D3 = proj.shape
    D = D3 // 3
    tm = _tile(S, 256, 16)

    def body(q_ref, k_ref, v_ref, gq_ref, gk_ref, c_ref, qn_ref, kn_ref, vb_ref, ak_ref):
        lane = lax.broadcasted_iota(jnp.int32, (tm, HEAD_DIM), 1)
        for src, g_ref, dst, mult in ((q_ref, gq_ref, qn_ref, Q_PRESCALE), (k_ref, gk_ref, kn_ref, None)):
            for h in range(D // HEAD_DIM):
                cols = slice(h * HEAD_DIM, (h + 1) * HEAD_DIM)
                t = src[:, cols]
                r = lax.rsqrt(jnp.mean(t * t, axis=-1, keepdims=True) + RMS_EPS)
                n = (t * r) * g_ref[...]
                dst[:, cols] = (n if mult is None else n * mult).astype(BF16)
        for h in range(D // HEAD_DIM):
            ch = jnp.broadcast_to(c_ref[:, h:h + 1] * LOG2E, (tm, HEAD_DIM))
            ak_ref[:, h * HEAD_DIM:(h + 1) * HEAD_DIM] = _aux_block(lane, ch, AUX_A, AUX_B)
        vb_ref[...] = v_ref[...].astype(BF16)

    part = lambda n: pl.BlockSpec((tm, D), lambda i: (i, n))
    vec = pl.BlockSpec((1, HEAD_DIM), lambda i: (0, 0))
    row = pl.BlockSpec((tm, D), lambda i: (i, 0))
    return pl.pallas_call(
        body, name=name, grid=(S // tm,),
        in_specs=[part(0), part(1), part(2), vec, vec, pl.BlockSpec((tm, LANES), lambda i: (i, 0))],
        out_specs=[row, row, row, row], out_shape=[jax.ShapeDtypeStruct((S, D), BF16)] * 4,
        compiler_params=_params("parallel"),
    )(proj, proj, proj, gq, gk, cum)


def _qkv_post_bwd(proj, dqx, dkx, dvb, gq, gk, name):
    S, D = dvb.shape
    tm = _tile(S, 128, 16)

    def body(q_ref, k_ref, dqx_ref, dkx_ref, dvb_ref, gq_ref, gk_ref, dp_ref, dgq_ref, dgk_ref, dc_ref):
        @pl.when(pl.program_id(0) == 0)
        def _():
            dgq_ref[...] = jnp.zeros_like(dgq_ref)
            dgk_ref[...] = jnp.zeros_like(dgk_ref)

        lane = lax.broadcasted_iota(jnp.int32, (tm, LANES), 1)
        dc = jnp.zeros((tm, LANES), F32)
        for h in range(D // HEAD_DIM):
            at = (2 * h + 1) * HEAD_DIM
            diff = dqx_ref[:, at + AUX_B:at + AUX_B + 1] - dkx_ref[:, at + AUX_A:at + AUX_A + 1]
            dc = jnp.where(lane == h, jnp.broadcast_to(diff, (tm, LANES)), dc)
        dc_ref[...] = dc

        for n, (src, dsrc, g_ref, dg_ref, mult) in enumerate(
                ((q_ref, dqx_ref, gq_ref, dgq_ref, HEAD_DIM ** -0.5), (k_ref, dkx_ref, gk_ref, dgk_ref, 1.0 / LOG2E))):
            dg = jnp.zeros((1, HEAD_DIM), F32)
            for h in range(D // HEAD_DIM):
                cols = slice(h * HEAD_DIM, (h + 1) * HEAD_DIM)
                t = src[:, cols]
                r = lax.rsqrt(jnp.mean(t * t, axis=-1, keepdims=True) + RMS_EPS)
                that = t * r
                dn = dsrc[:, 2 * h * HEAD_DIM:(2 * h + 1) * HEAD_DIM] * mult
                dhat = dn * g_ref[...]
                dt = r * (dhat - that * jnp.mean(dhat * that, axis=-1, keepdims=True))
                dp_ref[:, n * D + h * HEAD_DIM:n * D + (h + 1) * HEAD_DIM] = dt.astype(BF16)
                dg = dg + jnp.sum(dn * that, axis=0, keepdims=True)
            dg_ref[...] += dg
        dp_ref[:, 2 * D:3 * D] = dvb_ref[...]

    part = lambda n: pl.BlockSpec((tm, D), lambda i: (i, n))
    row = pl.BlockSpec((tm, D), lambda i: (i, 0))
    vec = pl.BlockSpec((1, HEAD_DIM), lambda i: (0, 0))
    wide = pl.BlockSpec((tm, 2 * D), lambda i: (i, 0))
    return pl.pallas_call(
        body, name=name, grid=(S // tm,), in_specs=[part(0), part(1), wide, wide, row, vec, vec],
        out_specs=[pl.BlockSpec((tm, 3 * D), lambda i: (i, 0)), vec, vec,
                   pl.BlockSpec((tm, LANES), lambda i: (i, 0))],
        out_shape=[jax.ShapeDtypeStruct((S, 3 * D), BF16), jax.ShapeDtypeStruct((1, HEAD_DIM), F32),
                   jax.ShapeDtypeStruct((1, HEAD_DIM), F32), jax.ShapeDtypeStruct((S, LANES), F32)],
        compiler_params=_params("arbitrary"),
    )(proj, proj, dqx, dkx, dvb, gq, gk)


def _log_sigmoid(z):
    return -(jnp.maximum(-z, 0.0) + jnp.log(1.0 + jnp.exp(-jnp.abs(z))))


def _fgate_fwd(fl, bf, name):
    S = fl.shape[0]
    T = _tile(S, 256, 16)

    def body(fl_ref, bf_ref, c_ref, carry):
        @pl.when(pl.program_id(0) == 0)
        def _():
            carry[...] = jnp.zeros_like(carry)

        logf = _log_sigmoid(fl_ref[...] + bf_ref[...])
        r = lax.broadcasted_iota(jnp.int32, (T, T), 0)
        c = lax.broadcasted_iota(jnp.int32, (T, T), 1)
        tri = (r >= c).astype(F32)
        cum = jnp.dot(tri, logf, precision=lax.Precision.HIGHEST, preferred_element_type=F32) + carry[...]
        c_ref[...] = cum
        carry[...] = cum[T - 1:T, :]

    row = pl.BlockSpec((T, LANES), lambda i: (i, 0))
    return pl.pallas_call(
        body, name=name, grid=(S // T,), in_specs=[row, pl.BlockSpec((1, LANES), lambda i: (0, 0))],
        out_specs=row, out_shape=jax.ShapeDtypeStruct((S, LANES), F32),
        scratch_shapes=[pltpu.VMEM((1, LANES), F32)], compiler_params=_params("arbitrary"),
    )(fl, bf)


def _fgate_bwd(dc, fl, bf, name):
    S = fl.shape[0]
    T = _tile(S, 256, 16)
    nb = S // T

    def body(dc_ref, fl_ref, bf_ref, dfl_ref, dbf_ref, carry):
        @pl.when(pl.program_id(0) == 0)
        def _():
            carry[...] = jnp.zeros_like(carry)
            dbf_ref[...] = jnp.zeros_like(dbf_ref)

        r = lax.broadcasted_iota(jnp.int32, (T, T), 0)
        c = lax.broadcasted_iota(jnp.int32, (T, T), 1)
        triu = (c >= r).astype(F32)
        dlogf = jnp.dot(triu, dc_ref[...], precision=lax.Precision.HIGHEST,
                        preferred_element_type=F32) + carry[...]
        carry[...] = dlogf[0:1, :]
        z = fl_ref[...] + bf_ref[...]
        dz = dlogf * (1.0 / (1.0 + jnp.exp(z)))
        dfl_ref[...] = dz.astype(BF16)
        dbf_ref[...] += jnp.sum(dz, axis=0, keepdims=True)

    row = pl.BlockSpec((T, LANES), lambda i: (nb - 1 - i, 0))
    vec = pl.BlockSpec((1, LANES), lambda i: (0, 0))
    return pl.pallas_call(
        body, name=name, grid=(nb,), in_specs=[row, row, vec], out_specs=[row, vec],
        out_shape=[jax.ShapeDtypeStruct((S, LANES), BF16), jax.ShapeDtypeStruct((1, LANES), F32)],
        scratch_shapes=[pltpu.VMEM((1, LANES), F32)], compiler_params=_params("arbitrary"),
    )(dc, fl, bf)


def _attn_logits(q_ref, aq, k_ref, ak, masked, T):
    qf = jnp.concatenate([q_ref[...], aq], axis=1)
    kf = jnp.concatenate([k_ref[...], ak], axis=1)
    s = _dot(qf, kf, NT)
    if masked:
        r = lax.broadcasted_iota(jnp.int32, (T, T), 0)
        c = lax.broadcasted_iota(jnp.int32, (T, T), 1)
        s = jnp.where(r >= c, s, NEG_INF)
    return s, qf, kf


def _causal_pairs(nb, q_major):
    pairs = ([(i, j) for i in range(nb) for j in range(i + 1)] if q_major
             else [(i, j) for j in range(nb) for i in range(j, nb)])
    return (jnp.asarray(np.array([p[0] for p in pairs], np.int32)),
            jnp.asarray(np.array([p[1] for p in pairs], np.int32)))


def _flash_fwd(q2, kn, vb, aux_k, name, side=None):
    S, D = q2.shape
    H = D // HEAD_DIM
    T = _tile(S, 1024)
    nb = S // T
    q_idx, k_idx = _causal_pairs(nb, True)

    def body(qi_ref, kj_ref, q_ref, k_ref, v_ref, ak_ref, o_ref, aq_ref, m_s, acc_s):
        t = pl.program_id(1)
        i = qi_ref[t]
        j = kj_ref[t]
        lane = lax.broadcasted_iota(jnp.int32, (T, HEAD_DIM), 1)

        @pl.when(j == 0)
        def _():
            m_s[...] = jnp.full_like(m_s, NEG_INF)
            acc_s[...] = jnp.zeros_like(acc_s)

        def step(masked):
            s, _, _ = _attn_logits(q_ref, _aux_block(lane, ones_at=AUX_A), k_ref, ak_ref[...], masked, T)
            m_prev = m_s[...]
            m_new = jnp.maximum(m_prev, jnp.max(s, axis=1, keepdims=True))
            p = jnp.exp2(s - jnp.tile(m_new, (1, T // HEAD_DIM)))
            alpha = jnp.exp2(m_prev - m_new)
            vf = jnp.concatenate([v_ref[...], jnp.ones((T, HEAD_DIM), BF16)], axis=1)
            acc_s[...] = jnp.tile(alpha, (1, 2)) * acc_s[...] + _dot(p.astype(BF16), vf, NN)
            m_s[...] = m_new

        @pl.when(j < i)
        def _():
            step(False)

        @pl.when(j == i)
        def _():
            step(True)
            l = acc_s[:, HEAD_DIM:]
            o_ref[...] = (acc_s[:, :HEAD_DIM] / l).astype(BF16)
            aq_ref[...] = _aux_block(lane, m_s[...] + jnp.log2(l), AUX_B, AUX_A)

    qspec = pl.BlockSpec((T, HEAD_DIM), lambda h, t, qi, kj: (qi[t], h))
    kspec = pl.BlockSpec((T, HEAD_DIM), lambda h, t, qi, kj: (kj[t], h))
    grid = (H, int(q_idx.shape[0]))
    operands = [q_idx, k_idx, q2, kn, vb, aux_k]
    in_specs, out_specs = [qspec, kspec, kspec, kspec], [qspec, qspec]
    out_shapes = [jax.ShapeDtypeStruct((S, D), BF16), jax.ShapeDtypeStruct((S, D), BF16)]
    scratch = [pltpu.VMEM((T, HEAD_DIM), F32), pltpu.VMEM((T, 2 * HEAD_DIM), F32)]
    sem = ("parallel", "arbitrary")
    if side is not None:
        body = _host(side, body, grid, 6, 2, 2)
        operands, in_specs = operands + side.operands, in_specs + [ANY] * len(side.operands)
        out_specs, out_shapes = out_specs + [ANY] * len(side.out_shapes), out_shapes + side.out_shapes
        scratch = scratch + side.sem_shapes
        sem = ("arbitrary", "arbitrary")
    grid_spec = pltpu.PrefetchScalarGridSpec(
        num_scalar_prefetch=2, grid=grid, in_specs=in_specs, out_specs=out_specs, scratch_shapes=scratch)
    outs = pl.pallas_call(
        body, name=name, grid_spec=grid_spec, out_shape=out_shapes, compiler_params=_params(*sem),
    )(*operands)
    return outs[0], outs[1], list(outs[2:])


def _attn_delta(do, o, name):
    S, D = do.shape
    H = D // HEAD_DIM
    tm = _tile(S, 256, 16)

    def body(do_ref, o_ref, ad_ref, dob_ref):
        lane = lax.broadcasted_iota(jnp.int32, (tm, HEAD_DIM), 1)
        for h in range(H):
            cols = slice(h * HEAD_DIM, (h + 1) * HEAD_DIM)
            delta = jnp.sum(do_ref[:, cols] * o_ref[:, cols].astype(F32), axis=1, keepdims=True)
            ad_ref[:, cols] = _aux_block(lane, jnp.broadcast_to(delta, (tm, HEAD_DIM)), AUX_A)
        dob_ref[...] = do_ref[...].astype(BF16)

    row = pl.BlockSpec((tm, D), lambda i: (i, 0))
    return pl.pallas_call(
        body, name=name, grid=(S // tm,), in_specs=[row, row], out_specs=[row, row],
        out_shape=[jax.ShapeDtypeStruct((S, D), BF16), jax.ShapeDtypeStruct((S, D), BF16)],
        compiler_params=_params("parallel"),
    )(do, o)


def _flash_bwd(q2, kn, vb, dob, aux_q, aux_k, aux_do, name, side=None):
    S, D = q2.shape
    H = D // HEAD_DIM
    T = _tile(S, 1024)
    nb = S // T
    q_idx, k_idx = _causal_pairs(nb, False)

    def body(qi_ref, kj_ref, q_ref, k_ref, v_ref, do_ref, aq_ref, ak_ref, ad_ref,
             dqx_ref, dkx_ref, dv_ref, dv_acc):
        t = pl.program_id(1)
        i = qi_ref[t]
        j = kj_ref[t]
        lane = lax.broadcasted_iota(jnp.int32, (T, HEAD_DIM), 1)

        @pl.when(t == 0)
        def _():
            dqx_ref[...] = jnp.zeros_like(dqx_ref)

        @pl.when(i == j)
        def _():
            dkx_ref[...] = jnp.zeros_like(dkx_ref)
            dv_acc[...] = jnp.zeros_like(dv_acc)

        def step(masked):
            s, qf, kf = _attn_logits(q_ref, aq_ref[...], k_ref, ak_ref[...], masked, T)
            p = jnp.exp2(s)
            dof = jnp.concatenate([do_ref[...], ad_ref[...]], axis=1)
            vf = jnp.concatenate([v_ref[...], _aux_block(lane, ones_at=AUX_A)], axis=1)
            ds = p * _dot(dof, vf, NT)
            dsb = ds.astype(BF16)
            dv_acc[...] += _dot(p.astype(BF16), do_ref[...], TN)
            dkx_ref[...] += _dot(dsb, qf, TN)
            rows = pl.ds(pl.multiple_of(i * T, T), T)
            dqx_ref[rows, :] += _dot(dsb, kf, NN)

        @pl.when(i > j)
        def _():
            step(False)

        @pl.when(i == j)
        def _():
            step(True)

        @pl.when(i == nb - 1)
        def _():
            dv_ref[...] = dv_acc[...].astype(BF16)

    qspec = pl.BlockSpec((T, HEAD_DIM), lambda h, t, qi, kj: (qi[t], h))
    kspec = pl.BlockSpec((T, HEAD_DIM), lambda h, t, qi, kj: (kj[t], h))
    grid = (H, int(q_idx.shape[0]))
    operands = [q_idx, k_idx, q2, kn, vb, dob, aux_q, aux_k, aux_do]
    in_specs = [qspec, kspec, kspec, qspec, qspec, kspec, qspec]
    out_specs = [pl.BlockSpec((S, 2 * HEAD_DIM), lambda h, t, qi, kj: (0, h)),
                 pl.BlockSpec((T, 2 * HEAD_DIM), lambda h, t, qi, kj: (kj[t], h)), kspec]
    out_shapes = [jax.ShapeDtypeStruct((S, 2 * D), F32), jax.ShapeDtypeStruct((S, 2 * D), F32),
                  jax.ShapeDtypeStruct((S, D), BF16)]
    scratch = [pltpu.VMEM((T, HEAD_DIM), F32)]
    sem = ("parallel", "arbitrary")
    if side is not None:
        body = _host(side, body, grid, 9, 3, 1)
        operands, in_specs = operands + side.operands, in_specs + [ANY] * len(side.operands)
        out_specs, out_shapes = out_specs + [ANY] * len(side.out_shapes), out_shapes + side.out_shapes
        scratch = scratch + side.sem_shapes
        sem = ("arbitrary", "arbitrary")
    grid_spec = pltpu.PrefetchScalarGridSpec(
        num_scalar_prefetch=2, grid=grid, in_specs=in_specs, out_specs=out_specs, scratch_shapes=scratch)
    outs = pl.pallas_call(
        body, name=name, grid_spec=grid_spec, out_shape=out_shapes, compiler_params=_params(*sem),
    )(*operands)
    return outs[0], outs[1], outs[2], list(outs[3:])


def _pool_counts(first_row, tm, win):
    t = first_row + lax.broadcasted_iota(jnp.int32, (tm, 1), 0)
    return jnp.minimum(t + 1, win).astype(F32)


def _pool_fwd(h, x, wp, b, scale, name):
    S, D = h.shape
    G = D // len(POOL_WINDOWS)
    tm = _tile(S, 256, 16)

    def body(h_ref, halo_ref, x_ref, w_ref, b_ref, s_ref, y_ref, o_ref, ext):
        i = pl.program_id(0)
        ext[POOL_HALO:, :] = h_ref[...]

        @pl.when(i == 0)
        def _():
            ext[0:POOL_HALO, :] = jnp.zeros((POOL_HALO, D), F32)

        @pl.when(i > 0)
        def _():
            ext[0:POOL_HALO, :] = halo_ref[...]

        for g, win in enumerate(POOL_WINDOWS):
            cols = slice(g * G, (g + 1) * G)
            tot = ext[POOL_HALO:POOL_HALO + tm, cols]
            for k in range(1, win):
                tot = tot + ext[POOL_HALO - k:POOL_HALO - k + tm, cols]
            y = (tot / _pool_counts(i * tm, tm, win) - h_ref[:, cols]).astype(BF16)
            y_ref[:, cols] = y
            z = _dot(y, w_ref[g], NN)
            o_ref[:, cols] = x_ref[:, cols] + (z + b_ref[:, cols]) * s_ref[:, cols]

    row = pl.BlockSpec((tm, D), lambda i: (i, 0))
    vec = pl.BlockSpec((1, D), lambda i: (0, 0))
    halo = pl.BlockSpec((POOL_HALO, D), lambda i: (jnp.maximum(i * (tm // POOL_HALO) - 1, 0), 0))
    return pl.pallas_call(
        body, name=name, grid=(S // tm,),
        in_specs=[row, halo, row, pl.BlockSpec((len(POOL_WINDOWS), G, G), lambda i: (0, 0, 0)), vec, vec],
        out_specs=[row, row],
        out_shape=[jax.ShapeDtypeStruct((S, D), BF16), jax.ShapeDtypeStruct((S, D), F32)],
        scratch_shapes=[pltpu.VMEM((tm + POOL_HALO, D), F32)], compiler_params=_params("parallel"),
    )(h, h, x, wp, b, scale)


def _pool_bwd_mix(dout, yb, wp, b, scale, name):
    S, D = dout.shape
    NG = len(POOL_WINDOWS)
    G = D // NG
    tm = _tile(S, 256, 16)

    def body(do_ref, y_ref, w_ref, b_ref, s_ref, dyc_ref, dw_ref, db_ref, ds_ref):
        i = pl.program_id(0)

        @pl.when(i == 0)
        def _():
            dw_ref[...] = jnp.zeros_like(dw_ref)
            db_ref[...] = jnp.zeros_like(db_ref)
            ds_ref[...] = jnp.zeros_like(ds_ref)

        for g, win in enumerate(POOL_WINDOWS):
            cols = slice(g * G, (g + 1) * G)
            y = y_ref[:, cols]
            dz = do_ref[:, cols]
            zb = _dot(y, w_ref[g], NN) + b_ref[:, cols]
            ds_ref[:, cols] += jnp.sum(dz * zb, axis=0, keepdims=True)
            dzb = dz * s_ref[:, cols]
            db_ref[:, cols] += jnp.sum(dzb, axis=0, keepdims=True)
            dzb16 = dzb.astype(BF16)
            dw_ref[g] += _dot(y, dzb16, TN)
            dy = _dot(dzb16, w_ref[g], NT)
            dyc_ref[:, cols] = dy / _pool_counts(i * tm, tm, win)

    row = pl.BlockSpec((tm, D), lambda i: (i, 0))
    vec = pl.BlockSpec((1, D), lambda i: (0, 0))
    wspec = pl.BlockSpec((NG, G, G), lambda i: (0, 0, 0))
    return pl.pallas_call(
        body, name=name, grid=(S // tm,), in_specs=[row, row, wspec, vec, vec],
        out_specs=[row, wspec, vec, vec],
        out_shape=[jax.ShapeDtypeStruct((S, D), F32), jax.ShapeDtypeStruct((NG, G, G), F32),
                   jax.ShapeDtypeStruct((1, D), F32), jax.ShapeDtypeStruct((1, D), F32)],
        compiler_params=_params("arbitrary"),
    )(dout, yb, wp, b, scale)


def _pool_bwd_window(dyc, name):
    S, D = dyc.shape
    G = D // len(POOL_WINDOWS)
    tm = _tile(S, 256, 16)
    nb = S // tm

    def body(d_ref, halo_ref, dh_ref, ext):
        i = pl.program_id(0)
        ext[0:tm, :] = d_ref[...]

        @pl.when(i == nb - 1)
        def _():
            ext[tm:tm + POOL_HALO, :] = jnp.zeros((POOL_HALO, D), F32)

        @pl.when(i < nb - 1)
        def _():
            ext[tm:tm + POOL_HALO, :] = halo_ref[...]

        for g, win in enumerate(POOL_WINDOWS):
            cols = slice(g * G, (g + 1) * G)
            tot = ext[0:tm, cols] * (1.0 - _pool_counts(i * tm, tm, win))
            for k in range(1, win):
                tot = tot + ext[k:k + tm, cols]
            dh_ref[:, cols] = tot

    row = pl.BlockSpec((tm, D), lambda i: (i, 0))
    halo = pl.BlockSpec((POOL_HALO, D),
                        lambda i: (jnp.minimum((i + 1) * (tm // POOL_HALO), S // POOL_HALO - 1), 0))
    return pl.pallas_call(
        body, name=name, grid=(nb,), in_specs=[row, halo], out_specs=row,
        out_shape=jax.ShapeDtypeStruct((S, D), F32),
        scratch_shapes=[pltpu.VMEM((tm + POOL_HALO, D), F32)], compiler_params=_params("parallel"),
    )(dyc, dyc)


def _adamw(w, g, m, v, name):
    R, C = w.shape
    tr = _row_tile(R, C, 4, 1 << 20)

    def body(w_ref, g_ref, m_ref, v_ref, d_ref, nm_ref, nv_ref):
        gv = g_ref[...]
        m_new = ADAM_B1 * m_ref[...] + (1.0 - ADAM_B1) * gv
        v_new = ADAM_B2 * v_ref[...] + (1.0 - ADAM_B2) * (gv * gv)
        m_hat = m_new / (1.0 - ADAM_B1 ** ADAM_STEP)
        v_hat = v_new / (1.0 - ADAM_B2 ** ADAM_STEP)
        d_ref[...] = -ADAM_LR * (m_hat / (jnp.sqrt(v_hat) + ADAM_EPS) + ADAM_WD * w_ref[...])
        nm_ref[...] = m_new
        nv_ref[...] = v_new

    row = pl.BlockSpec((tr, C), lambda i: (i, 0))
    return pl.pallas_call(
        body, name=name, grid=(R // tr,), in_specs=[row] * 4, out_specs=[row] * 3,
        out_shape=[jax.ShapeDtypeStruct((R, C), F32)] * 3, compiler_params=_params("parallel"),
    )(w, g, m, v)


def _sum_core_halves(g, r1, c_idx, name):
    _, _, Rh, C = g.shape
    tr = _row_tile(Rh, C, 4, 2 << 20)

    def body(c_ref, g_ref, r_ref, o_ref, ob_ref):
        total = g_ref[...] + r_ref[...]
        o_ref[...] = total
        ob_ref[...] = total.astype(BF16)

    piece = pl.BlockSpec((None, tr, C), lambda s, r, c_ref: (s, r, 0))
    grid_spec = pltpu.PrefetchScalarGridSpec(
        num_scalar_prefetch=1, grid=(N_CHIPS, Rh // tr),
        in_specs=[pl.BlockSpec((None, None, tr, C), lambda s, r, c_ref: (s, c_ref[0], r, 0)), piece],
        out_specs=[piece, piece])
    return pl.pallas_call(
        body, name=name, grid_spec=grid_spec,
        out_shape=[jax.ShapeDtypeStruct((N_CHIPS, Rh, C), F32), jax.ShapeDtypeStruct((N_CHIPS, Rh, C), BF16)],
        compiler_params=_params("parallel", "parallel"),
    )(c_idx, g, r1)


def _sum_chips(h, r2, place, name):
    _, Rh, C = h.shape
    tr = _row_tile(Rh, C, 4, 1 << 20)

    def body(place_ref, h_ref, r_ref, o_ref):
        o_ref[...] = ((h_ref[...] + r_ref[0].astype(F32)) + r_ref[1].astype(F32)) + r_ref[2].astype(F32)

    grid_spec = pltpu.PrefetchScalarGridSpec(
        num_scalar_prefetch=1, grid=(Rh // tr,),
        in_specs=[pl.BlockSpec((None, tr, C), lambda r, pr: (pr[0], r, 0)),
                  pl.BlockSpec((N_CHIPS - 1, tr, C), lambda r, pr: (0, r, 0))],
        out_specs=pl.BlockSpec((None, tr, C), lambda r, pr: (pr[1], r, 0)))
    return pl.pallas_call(
        body, name=name, grid_spec=grid_spec, out_shape=jax.ShapeDtypeStruct((2, Rh, C), F32),
        compiler_params=_params("parallel"),
    )(place, h, r2)


ANY = pl.BlockSpec(memory_space=pl.ANY)


def _place():
    x, y, c = lax.axis_index("x"), lax.axis_index("y"), lax.axis_index("c")
    chips = [(1 - x, y), (x, 1 - y), (1 - x, 1 - y)]
    return x, y, c, chips, [2 * cx + cy for cx, cy in chips]


def _run_side(side, name):
    def body(*refs):
        _, side_refs = side.split(refs, 0, 0, 0)
        side.start(*side_refs)
        side.finish(*side_refs)

    return pl.pallas_call(
        body, name=name, in_specs=[ANY] * len(side.operands), out_specs=[ANY] * len(side.out_shapes),
        out_shape=side.out_shapes, scratch_shapes=side.sem_shapes,
    )(*side.operands)


def _allgather_side(shards):
    n = len(shards)

    def copy(sems, t, k, src, dst, to):
        return pltpu.make_async_remote_copy(src_ref=src, dst_ref=dst, send_sem=sems[0].at[t, k],
                                            recv_sem=sems[1].at[t, k], device_id=to, device_id_type=MESH)

    def first_copies(ins, outs, sems):
        x, y, c, chips, _ = _place()
        me = 2 * x + y
        return [copy(sems, t, j, ins[t].at[c], outs[t].at[me, c], (*chip, c))
                for t in range(n) for j, chip in enumerate(chips)]

    def start(ins, outs, sems):
        for cp in first_copies(ins, outs, sems):
            cp.start()

    def finish(ins, outs, sems):
        x, y, c, chips, chip_idx = _place()
        sibling = (x, y, 1 - c)
        passed = []
        for t in range(n):
            for j, chip in enumerate(chips):
                landed = outs[t].at[chip_idx[j], c]
                copy(sems, t, j, landed, landed, (*chip, c)).wait_recv()
                fw = copy(sems, t, 3 + j, landed, landed, sibling)
                fw.start()
                passed.append(fw)
        for t in range(n):
            for j in range(3):
                other = outs[t].at[chip_idx[j], 1 - c]
                copy(sems, t, 3 + j, other, other, sibling).wait_recv()
        for cp in first_copies(ins, outs, sems) + passed:
            cp.wait_send()

    return _Side(shards, [jax.ShapeDtypeStruct((N_CHIPS,) + s.shape, s.dtype) for s in shards],
                 [pltpu.SemaphoreType.DMA((n, 6)), pltpu.SemaphoreType.DMA((n, 6))], start, finish)


def _send_other_half_to_sibling(gs, name):
    n = len(gs)

    def body(*refs):
        ins, outs = refs[:n], refs[n:2 * n]
        send_sems, recv_sems = refs[2 * n:]
        x, y, c, _, _ = _place()
        copies = []
        for t in range(n):
            for s in range(N_CHIPS):
                cp = pltpu.make_async_remote_copy(
                    src_ref=ins[t].at[s, 1 - c], dst_ref=outs[t].at[s], send_sem=send_sems.at[t, s],
                    recv_sem=recv_sems.at[t, s], device_id=(x, y, 1 - c), device_id_type=MESH)
                cp.start()
                copies.append(cp)
        for cp in copies:
            cp.wait()

    return pl.pallas_call(
        body, name=name, in_specs=[ANY] * n, out_specs=[ANY] * n,
        out_shape=[jax.ShapeDtypeStruct((N_CHIPS,) + g.shape[2:], g.dtype) for g in gs],
        scratch_shapes=[pltpu.SemaphoreType.DMA((n, N_CHIPS)), pltpu.SemaphoreType.DMA((n, N_CHIPS))],
    )(*gs)


def _exchange_side(hs):
    n = len(hs)

    def copies(ins, outs, sems):
        x, y, c, chips, chip_idx = _place()
        return [pltpu.make_async_remote_copy(
            src_ref=ins[t].at[chip_idx[j]], dst_ref=outs[t].at[j], send_sem=sems[0].at[t, j],
            recv_sem=sems[1].at[t, j], device_id=(*chip, c), device_id_type=MESH)
            for t in range(n) for j, chip in enumerate(chips)]

    def start(ins, outs, sems):
        for cp in copies(ins, outs, sems):
            cp.start()

    def finish(ins, outs, sems):
        for cp in copies(ins, outs, sems):
            cp.wait()

    return _Side(hs, [jax.ShapeDtypeStruct((N_CHIPS - 1,) + h.shape[1:], h.dtype) for h in hs],
                 [pltpu.SemaphoreType.DMA((n, 3)), pltpu.SemaphoreType.DMA((n, 3))], start, finish)


def _join_core_halves(bufs, name):
    n = len(bufs)

    def body(*refs):
        outs = refs[n:2 * n]
        send_sems, recv_sems = refs[2 * n:]
        x, y, c, _, _ = _place()
        sends = []
        for t in range(n):
            cp = pltpu.make_async_remote_copy(
                src_ref=outs[t].at[c], dst_ref=outs[t].at[c], send_sem=send_sems.at[t], recv_sem=recv_sems.at[t],
                device_id=(x, y, 1 - c), device_id_type=MESH)
            cp.start()
            sends.append(cp)
        for t in range(n):
            other = outs[t].at[1 - c]
            pltpu.make_async_remote_copy(
                src_ref=other, dst_ref=other, send_sem=send_sems.at[t], recv_sem=recv_sems.at[t],
                device_id=(x, y, 1 - c), device_id_type=MESH).wait_recv()
        for cp in sends:
            cp.wait_send()

    return pl.pallas_call(
        body, name=name, in_specs=[ANY] * n, out_specs=[ANY] * n,
        out_shape=[jax.ShapeDtypeStruct(b.shape, b.dtype) for b in bufs],
        input_output_aliases={t: t for t in range(n)},
        scratch_shapes=[pltpu.SemaphoreType.DMA((n,)), pltpu.SemaphoreType.DMA((n,))],
    )(*bufs)


def _allreduce_small(vec, name):
    R = vec.shape[0]

    def body(v_ref, o_ref, buf, send_sems, recv_sems):
        x, y, c = lax.axis_index("x"), lax.axis_index("y"), lax.axis_index("c")
        me = 4 * x + 2 * y + c
        buf[me] = v_ref[...]
        peers = []
        for k in range(1, N_DEV):
            px = 1 - x if k & 4 else x
            py = 1 - y if k & 2 else y
            pc = 1 - c if k & 1 else c
            peers.append(((px, py, pc), 4 * px + 2 * py + pc))
        sends = []
        for k, (peer, _) in enumerate(peers):
            cp = pltpu.make_async_remote_copy(
                src_ref=v_ref, dst_ref=buf.at[me], send_sem=send_sems.at[k], recv_sem=recv_sems.at[k],
                device_id=peer, device_id_type=MESH)
            cp.start()
            sends.append(cp)
        for k, (peer, idx) in enumerate(peers):
            pltpu.make_async_remote_copy(
                src_ref=v_ref, dst_ref=buf.at[idx], send_sem=send_sems.at[k], recv_sem=recv_sems.at[k],
                device_id=peer, device_id_type=MESH).wait_recv()
        for cp in sends:
            cp.wait_send()
        total = buf[0]
        for d in range(1, N_DEV):
            total = total + buf[d]
        o_ref[...] = total

    vm = pl.BlockSpec(memory_space=pltpu.VMEM)
    return pl.pallas_call(
        body, name=name, in_specs=[vm], out_specs=vm, out_shape=jax.ShapeDtypeStruct((R, LANES), F32),
        scratch_shapes=[pltpu.VMEM((N_DEV, R, LANES), F32), pltpu.SemaphoreType.DMA((N_DEV - 1,)),
                        pltpu.SemaphoreType.DMA((N_DEV - 1,))],
    )(vec)


def _halves(a):
    lead = 1
    for d in a.shape[:-1]:
        lead *= d
    return a.reshape(2, lead // 2, a.shape[-1])


def _cols_from_shards(g):
    return g.transpose(1, 0, 2).reshape(g.shape[1], N_CHIPS * g.shape[2])


def _shards_from_cols(w):
    return w.reshape(w.shape[0], N_CHIPS, w.shape[1] // N_CHIPS).transpose(1, 0, 2)


def _pack(parts, rows):
    flat = jnp.concatenate([p.reshape(-1).astype(F32) for p in parts])
    return jnp.pad(flat, (0, rows * LANES - flat.shape[0])).reshape(rows, LANES)


def _unpack(packed, shapes):
    flat = packed.reshape(-1)
    out, off = [], 0
    for s in shapes:
        n = 1
        for d in s:
            n *= d
        out.append(flat[off:off + n].reshape(s))
        off += n
    return out


def _packed_rows(shapes):
    n = 0
    for s in shapes:
        k = 1
        for d in s:
            k *= d
        n += k
    return -(-n // (8 * LANES)) * 8


def _pad_lanes(a):
    return jnp.pad(a, ((0, 0), (0, LANES - a.shape[1])))


def kernel(x, mix_norm_g, ffn_norm_g, fox_w_in, fox_b_f, fox_q_norm_g, fox_k_norm_g, fox_w_out, pool_w, pool_b, pool_scale, ffn_w_gate_up, ffn_w_down, loss_target, m_mix_norm_g, m_ffn_norm_g, m_fox_w_in, m_fox_b_f, m_fox_q_norm_g, m_fox_k_norm_g, m_fox_w_out, m_pool_w, m_pool_b, m_pool_scale, m_ffn_w_gate_up, m_ffn_w_down, v_mix_norm_g, v_ffn_norm_g, v_fox_w_in, v_fox_b_f, v_fox_q_norm_g, v_fox_k_norm_g, v_fox_w_out, v_pool_w, v_pool_b, v_pool_scale, v_ffn_w_gate_up, v_ffn_w_down):
    _, S, D = x.shape
    H = D // HEAD_DIM
    depth = mix_norm_g.shape[0]
    n_pool = pool_w.shape[0]
    NG = len(POOL_WINDOWS)
    G = D // NG
    F = ffn_w_down.shape[1] * N_CHIPS
    ax, ay, ac = lax.axis_index("x"), lax.axis_index("y"), lax.axis_index("c")
    chip = 2 * ax + ay
    c_idx = jnp.reshape(ac, (1,)).astype(jnp.int32)
    place = jnp.stack([chip, ac]).astype(jnp.int32)
    xs = x[0]
    target = loss_target[0]

    Dq = D // N_CHIPS
    small_fwd_shapes = [(n_pool, D), (n_pool, D)]
    placed = []
    for p in (pool_b, pool_scale):
        full = lax.dynamic_update_slice(jnp.zeros((n_pool, D), F32), p, (0, chip * Dq))
        placed.append(jnp.where(ac == 0, full, jnp.zeros_like(full)))
    pool_b_full, pool_scale_full = _unpack(
        _allreduce_small(_pack(placed, _packed_rows(small_fwd_shapes)), "gather_pool_vectors"), small_fwd_shapes)

    def weight_shards(i):
        mixer = ([fox_w_in[i // 2], fox_w_out[i // 2]] if i % 2 == 0 else [pool_w[i // 2]])
        return [_halves(w.astype(BF16)) for w in mixer + [ffn_w_gate_up[i], ffn_w_down[i]]]

    def with_own(shards, got):
        return [lax.dynamic_update_slice(g, sh[None], (chip, 0, 0, 0)) for g, sh in zip(got, shards)]

    def mixer_weights(i, shards, got):
        got = with_own(shards, got)
        if i % 2 == 1:
            return {"w_pool": got[0].reshape(N_CHIPS, NG, G // N_CHIPS, G).transpose(1, 0, 2, 3).reshape(NG, G, G)}
        w_in = _cols_from_shards(got[0].reshape(N_CHIPS, D, -1))
        return {"w_qkv": w_in[:, :3 * D], "w_f": _pad_lanes(w_in[:, 3 * D:]), "w_out": got[1].reshape(D, D)}

    def ffn_weights(shards, got):
        got = with_own(shards, got)
        return {"w_gu": _cols_from_shards(got[0].reshape(N_CHIPS, D, -1)), "w_down": got[1].reshape(F, D)}

    def layer_weights(i, shards, got):
        return {**mixer_weights(i, shards[:-2], got[:-2]), **ffn_weights(shards[-2:], got[-2:])}

    shards0 = weight_shards(0)
    weights = [mixer_weights(0, shards0[:-2], _run_side(_allgather_side(shards0[:-2]), "allgather_weights_first"))]
    own_ffn = shards0[-2:]

    saved = []
    cur = xs
    for i in range(depth):
        j = i // 2
        lw = weights[i]
        sv = {"x": cur}
        next_shards = weight_shards(i + 1) if i + 1 < depth else None
        gather_next = _allgather_side(next_shards) if next_shards is not None else None
        g_mix = mix_norm_g[i][None]
        if i % 2 == 0:
            h1b = _rmsnorm_fwd(cur, g_mix, BF16, "rmsnorm_fwd_bf16")
            proj = _matmul(h1b, lw["w_qkv"], "nn", F32, "qkv_proj", M=S, N=3 * D, K=D)
            fl = _matmul(h1b, lw["w_f"], "nn", F32, "forget_proj", M=S, N=LANES, K=D)
            bf = _pad_lanes(fox_b_f[j][None])
            cum = _fgate_fwd(fl, bf, "forget_cumsum")
            q2, kn, vb, aux_k = _qkv_post(proj, fox_q_norm_g[j][None], fox_k_norm_g[j][None], cum, "qk_norm")
            gather_own = _allgather_side(own_ffn) if own_ffn is not None else None
            o, aux_q, got = _flash_fwd(q2, kn, vb, aux_k, "fox_attention_fwd",
                                       side=_merge_sides([gather_own, gather_next]))
            if own_ffn is not None:
                lw.update(ffn_weights(own_ffn, got[:2]))
                got, own_ffn = got[2:], None
            gather_next = None
            x1 = _matmul(o, lw["w_out"], "nn", F32, "attn_out_proj", M=S, N=D, K=D, res=cur)
            sv.update(h1b=h1b, proj=proj, fl=fl, bf=bf, q2=q2, kn=kn, vb=vb, aux_k=aux_k, aux_q=aux_q, o=o)
        else:
            h1 = _rmsnorm_fwd(cur, g_mix, F32, "rmsnorm_fwd_f32")
            yb, x1 = _pool_fwd(h1, cur, lw["w_pool"], pool_b_full[j][None], pool_scale_full[j][None], "pool_fwd")
            sv.update(yb=yb)
        h2b = _rmsnorm_fwd(x1, ffn_norm_g[i][None], BF16, "rmsnorm_fwd_bf16")
        if gather_next is not None:
            gu, act, got_ffn = _ffn_up(h2b, lw["w_gu"], "ffn_gate_up_gather", side=_allgather_side(next_shards[-2:]))
            x2, got_mix = _matmul(act, lw["w_down"], "nn", F32, "ffn_down_gather", M=S, N=D, K=F, tk=2816, res=x1,
                                  side=_allgather_side(next_shards[:-2]))
            got = got_mix + got_ffn
        else:
            gu, act, _ = _ffn_up(h2b, lw["w_gu"], "ffn_gate_up")
            x2 = _matmul(act, lw["w_down"], "nn", F32, "ffn_down", M=S, N=D, K=F, tk=2816, res=x1)
        if next_shards is not None:
            weights.append(layer_weights(i + 1, next_shards, got))
        sv.update(x1=x1, h2b=h2b, gu=gu, act=act)
        saved.append(sv)
        cur = x2

    dcur, dcur_b, sq = _loss_grad(cur, target, "loss_grad")
    loss = lax.psum(sq[0, 0] * (0.5 / D), ("x", "y", "c"))

    g_mix_rows, g_ffn_rows = [None] * depth, [None] * depth
    g_bf, g_gq, g_gk = [None] * (depth - n_pool), [None] * (depth - n_pool), [None] * (depth - n_pool)
    g_pb, g_ps = [None] * n_pool, [None] * n_pool
    red_mix, red_ffn = [None] * depth, [None] * depth

    def half_sums(big, group):
        gs = [b.reshape(N_CHIPS, 2, -1, b.shape[-1]) for b in big]
        r1 = _send_other_half_to_sibling(gs, f"grads_to_sibling_{group}")
        sums = [_sum_core_halves(g, r, c_idx, "sum_core_halves") for g, r in zip(gs, r1)]
        return [s[0] for s in sums], [s[1] for s in sums]

    def finish_reduce(layer, group, hs, r2):
        rs = [_sum_chips(h, r, place, "sum_chips") for h, r in zip(hs, r2)]
        joined = _join_core_halves(rs, f"grads_join_{group}")
        if group != "ffn":
            red_mix[layer] = joined if group == "mix" else joined[:-2]
        if group != "mix":
            red_ffn[layer] = joined[-2:]

    pending = None
    for i in reversed(range(depth)):
        j = i // 2
        lw, sv = weights[i], saved[i]
        dgu = _ffn_act_bwd(dcur_b, lw["w_down"], sv["gu"], "ffn_act_bwd")
        d_w_down = _matmul(sv["act"], dcur_b, "tn", F32, "ffn_down_dw", M=F, N=D, K=S, tm=1408)
        dh2 = _matmul(
            dgu, lw["w_gu"], "nt", F32, "ffn_up_dx", M=S, N=D, K=2 * F, tk=_tile(F, 2816),
            a_spec=lambda tm, tn, tk: pl.BlockSpec(
                (None, tm, tk), lambda i_, j_, k_: (k_ // (F // tk), i_, k_ % (F // tk))))
        d_w_gu = _matmul(
            sv["h2b"], dgu, "tn", F32, "ffn_up_dw" if pending is None else "ffn_up_dw_exchange",
            M=D, N=2 * F, K=S, tn=_tile(F, 1408), out_shards=N_CHIPS,
            b_spec=lambda tm, tn, tk: pl.BlockSpec(
                (None, tk, tn), lambda i_, j_, k_: (j_ // (F // tn), k_, j_ % (F // tn))),
            side=None if pending is None else _exchange_side(pending[3]))
        if pending is not None:
            d_w_gu, r2 = d_w_gu
            finish_reduce(pending[0], pending[1], pending[2], r2)
        ffn_big = [d_w_gu, d_w_down.reshape(N_CHIPS, F // N_CHIPS, D)]
        dx1, dx1b, g_ffn_rows[i] = _rmsnorm_bwd(sv["x1"], ffn_norm_g[i][None], dh2, dcur, "rmsnorm_bwd")
        g_mix = mix_norm_g[i][None]
        if i % 2 == 0:
            do = _matmul(dx1b, lw["w_out"], "nt", F32, "attn_out_dx", M=S, N=D, K=D)
            d_w_out = _matmul(sv["o"], dx1b, "tn", F32, "attn_out_dw", M=D, N=D, K=S)
            aux_do, dob = _attn_delta(do, sv["o"], "attn_delta")
            ffn_hs, ffn_hb = half_sums(ffn_big, "ffn")
            dqx, dkx, dvb, r2 = _flash_bwd(sv["q2"], sv["kn"], sv["vb"], dob, sv["aux_q"], sv["aux_k"], aux_do,
                                           "fox_attention_bwd", side=_exchange_side(ffn_hb))
            finish_reduce(i, "ffn", ffn_hs, r2)
            dproj, dgq, dgk, dc = _qkv_post_bwd(sv["proj"], dqx, dkx, dvb, fox_q_norm_g[j][None],
                                                fox_k_norm_g[j][None], "qk_norm_bwd")
            dfl, dbf = _fgate_bwd(dc, sv["fl"], sv["bf"], "forget_cumsum_bwd")
            d_w_qkv = _matmul(sv["h1b"], dproj, "tn", F32, "qkv_dw", M=D, N=3 * D, K=S)
            d_w_f = _matmul(sv["h1b"], dfl, "tn", F32, "forget_dw", M=D, N=LANES, K=S)
            dh1f = _matmul(dfl, lw["w_f"], "nt", F32, "forget_dx", M=S, N=D, K=LANES)
            dh1 = _matmul(dproj, lw["w_qkv"], "nt", F32, "qkv_dx", M=S, N=D, K=3 * D, res=dh1f)
            d_w_in = jnp.concatenate([d_w_qkv, d_w_f[:, :H]], axis=1)
            big, group = [_shards_from_cols(d_w_in), d_w_out.reshape(N_CHIPS, D // N_CHIPS, D)], "mix"
            g_bf[j], g_gq[j], g_gk[j] = dbf[0, :H], dgq[0], dgk[0]
        else:
            dyc, d_wp, dpb, dps = _pool_bwd_mix(dx1, sv["yb"], lw["w_pool"], pool_b_full[j][None],
                                                pool_scale_full[j][None], "pool_bwd_mix")
            dh1 = _pool_bwd_window(dyc, "pool_bwd_window")
            big, group = [d_wp.reshape(NG, N_CHIPS, G // N_CHIPS, G).transpose(1, 0, 2, 3)] + ffn_big, "all"
            g_pb[j], g_ps[j] = dpb[0], dps[0]
        dcur, dcur_b, g_mix_rows[i] = _rmsnorm_bwd(sv["x"], g_mix, dh1, dx1, "rmsnorm_bwd")
        pending = (i, group) + half_sums(big, group)
    finish_reduce(pending[0], pending[1], pending[2],
                  _run_side(_exchange_side(pending[3]), "grads_to_chips_last"))

    small_shapes = [(depth, D), (depth, D), fox_b_f.shape, fox_q_norm_g.shape, fox_k_norm_g.shape,
                    (n_pool, D), (n_pool, D)]
    small = [jnp.stack(g_mix_rows)[:, 0], jnp.stack(g_ffn_rows)[:, 0], jnp.stack(g_bf), jnp.stack(g_gq),
             jnp.stack(g_gk), jnp.stack(g_pb), jnp.stack(g_ps)]
    (gr_mix, gr_ffn, gr_bf, gr_gq, gr_gk, gr_pb_full, gr_ps_full) = _unpack(
        _allreduce_small(_pack(small, _packed_rows(small_shapes)), "allreduce_small_grads"), small_shapes)
    gr_pb = lax.dynamic_slice(gr_pb_full, (0, chip * Dq), (n_pool, Dq))
    gr_ps = lax.dynamic_slice(gr_ps_full, (0, chip * Dq), (n_pool, Dq))

    fox_layers = [i for i in range(depth) if i % 2 == 0]
    pool_layers = [i for i in range(depth) if i % 2 == 1]
    gr_w_in = jnp.stack([red_mix[i][0].reshape(fox_w_in.shape[1:]) for i in fox_layers])
    gr_w_out = jnp.stack([red_mix[i][1].reshape(fox_w_out.shape[1:]) for i in fox_layers])
    gr_pool_w = jnp.stack([red_mix[i][0].reshape(pool_w.shape[1:]) for i in pool_layers])
    gr_gu = jnp.stack([red_ffn[i][0].reshape(ffn_w_gate_up.shape[1:]) for i in range(depth)])
    gr_down = jnp.stack([red_ffn[i][1].reshape(ffn_w_down.shape[1:]) for i in range(depth)])

    def update_big(w, g, m, v, name):
        flat = lambda a: a.reshape(-1, a.shape[-1])
        return [o.reshape(w.shape) for o in _adamw(flat(w), flat(g), flat(m), flat(v), name)]

    names = ["mix_norm_g", "ffn_norm_g", "fox_w_in", "fox_b_f", "fox_q_norm_g", "fox_k_norm_g", "fox_w_out",
             "pool_w", "pool_b", "pool_scale", "ffn_w_gate_up", "ffn_w_down"]
    ws = dict(zip(names, [mix_norm_g, ffn_norm_g, fox_w_in, fox_b_f, fox_q_norm_g, fox_k_norm_g, fox_w_out,
                          pool_w, pool_b, pool_scale, ffn_w_gate_up, ffn_w_down]))
    ms = dict(zip(names, [m_mix_norm_g, m_ffn_norm_g, m_fox_w_in, m_fox_b_f, m_fox_q_norm_g, m_fox_k_norm_g,
                          m_fox_w_out, m_pool_w, m_pool_b, m_pool_scale, m_ffn_w_gate_up, m_ffn_w_down]))
    vs = dict(zip(names, [v_mix_norm_g, v_ffn_norm_g, v_fox_w_in, v_fox_b_f, v_fox_q_norm_g, v_fox_k_norm_g,
                          v_fox_w_out, v_pool_w, v_pool_b, v_pool_scale, v_ffn_w_gate_up, v_ffn_w_down]))
    grads = dict(mix_norm_g=gr_mix, ffn_norm_g=gr_ffn, fox_w_in=gr_w_in, fox_b_f=gr_bf, fox_q_norm_g=gr_gq,
                 fox_k_norm_g=gr_gk, fox_w_out=gr_w_out, pool_w=gr_pool_w, pool_b=gr_pb, pool_scale=gr_ps,
                 ffn_w_gate_up=gr_gu, ffn_w_down=gr_down)
    big_names = ["fox_w_in", "fox_w_out", "pool_w", "ffn_w_gate_up", "ffn_w_down"]
    small_names = [n for n in names if n not in big_names]
    delta, new_m, new_v = {}, {}, {}
    for n in big_names:
        delta[n], new_m[n], new_v[n] = update_big(ws[n], grads[n], ms[n], vs[n], "adamw_" + n)
    shapes = [ws[n].shape for n in small_names]
    rows = _packed_rows(shapes)
    packed = _adamw(_pack([ws[n] for n in small_names], rows), _pack([grads[n] for n in small_names], rows),
                    _pack([ms[n] for n in small_names], rows), _pack([vs[n] for n in small_names], rows),
                    "adamw_small")
    for dst, pk in zip((delta, new_m, new_v), packed):
        for n, a in zip(small_names, _unpack(pk, shapes)):
            dst[n] = a

    return (loss, dcur[None], *[grads[n] for n in names], *[delta[n] for n in names],
            *[new_m[n] for n in names], *[new_v[n] for n in names])
```

```python
import functools

import numpy as np
import jax
import jax.numpy as jnp
from jax import lax
from jax.experimental import pallas as pl
from jax.experimental.pallas import tpu as pltpu

F32 = jnp.float32
BF16 = jnp.bfloat16
MESH = pl.DeviceIdType.MESH

HEAD_DIM = 128
RMS_EPS = 1e-6
NEG_INF = -1e30
POOL_WINDOWS = (2, 4, 8, 16)
POOL_HALO = 16
LANES = 128
N_CHIPS = 4
N_DEV = 8
VMEM_LIMIT = 56 * 1024 * 1024

ADAM_LR = 0.001
ADAM_B1 = 0.9
ADAM_B2 = 0.999
ADAM_EPS = 1e-08
ADAM_WD = 0.01
ADAM_STEP = 10

NN = (((1,), (0,)), ((), ()))
NT = (((1,), (1,)), ((), ()))
TN = (((0,), (0,)), ((), ()))


def _params(*sem):
    return pltpu.CompilerParams(dimension_semantics=sem if sem else None, vmem_limit_bytes=VMEM_LIMIT)


def _tile(n, pref, unit=LANES):
    best = None
    t = unit
    while t <= min(n, pref):
        if n % t == 0:
            best = t
        t += unit
    return best if best is not None else n


def _row_tile(rows, cols, itemsize, budget):
    best = None
    for t in range(16, rows + 1, 16):
        if rows % t == 0 and t * cols * itemsize <= budget:
            best = t
    return best if best is not None else rows


def _dot(a, b, dims):
    return lax.dot_general(a, b, dims, preferred_element_type=F32)


class _Side:
    def __init__(self, operands, out_shapes, sem_shapes, start, finish):
        self.operands, self.out_shapes, self.sem_shapes = list(operands), list(out_shapes), list(sem_shapes)
        self.start, self.finish = start, finish

    def split(self, refs, n_in, n_out, n_scratch):
        si, so, ss = len(self.operands), len(self.out_shapes), len(self.sem_shapes)
        refs = list(refs)
        cuts = [n_in, si, n_out, so, n_scratch, ss]
        parts, at = [], 0
        for c in cuts:
            parts.append(refs[at:at + c])
            at += c
        ins, s_ins, outs, s_outs, scratch, sems = parts
        return ins + outs + scratch, (s_ins, s_outs, sems)

    def bracket(self, side_refs, first, last, compute):
        @pl.when(first)
        def _():
            self.start(*side_refs)

        compute()

        @pl.when(last)
        def _():
            self.finish(*side_refs)


def _merge_sides(sides):
    sides = [s for s in sides if s is not None]
    if len(sides) < 2:
        return sides[0] if sides else None

    def each(ins, outs, sems, what):
        i = o = m = 0
        for s in sides:
            ni, no, nm = len(s.operands), len(s.out_shapes), len(s.sem_shapes)
            getattr(s, what)(ins[i:i + ni], outs[o:o + no], sems[m:m + nm])
            i, o, m = i + ni, o + no, m + nm

    return _Side([a for s in sides for a in s.operands], [a for s in sides for a in s.out_shapes],
                 [a for s in sides for a in s.sem_shapes],
                 lambda ins, outs, sems: each(ins, outs, sems, "start"),
                 lambda ins, outs, sems: each(ins, outs, sems, "finish"))


def _grid_ends(grid):
    ids = [pl.program_id(a) for a in range(len(grid))]
    first, last = ids[0] == 0, ids[0] == grid[0] - 1
    for pid, n in zip(ids[1:], grid[1:]):
        first, last = first & (pid == 0), last & (pid == n - 1)
    return first, last


def _host(side, body, grid, n_in, n_out, n_scratch):
    if side is None:
        return body

    def hosted(*refs):
        own, side_refs = side.split(refs, n_in, n_out, n_scratch)
        first, last = _grid_ends(grid)
        side.bracket(side_refs, first, last, lambda: body(*own))

    return hosted


def _matmul(a, b, mode, out_dtype, name, *, M, N, K, tm=1024, tn=1024, tk=2048, res=None,
            a_spec=None, b_spec=None, out_shards=None, side=None):
    tm, tn, tk = _tile(M, tm), _tile(N, tn), _tile(K, tk)
    nk = K // tk
    dims = {"nn": NN, "nt": NT, "tn": TN}[mode]
    if a_spec is None:
        a_spec = (pl.BlockSpec((tk, tm), lambda i, j, k: (k, i)) if mode == "tn"
                  else pl.BlockSpec((tm, tk), lambda i, j, k: (i, k)))
    else:
        a_spec = a_spec(tm, tn, tk)
    if b_spec is None:
        b_spec = (pl.BlockSpec((tn, tk), lambda i, j, k: (j, k)) if mode == "nt"
                  else pl.BlockSpec((tk, tn), lambda i, j, k: (k, j)))
    else:
        b_spec = b_spec(tm, tn, tk)
    if out_shards is None:
        o_spec = pl.BlockSpec((tm, tn), lambda i, j, k: (i, j))
        out_shape = jax.ShapeDtypeStruct((M, N), out_dtype)
    else:
        per = N // out_shards // tn
        o_spec = pl.BlockSpec((None, tm, tn), lambda i, j, k: (j // per, i, j % per))
        out_shape = jax.ShapeDtypeStruct((out_shards, M, N // out_shards), out_dtype)
    has_res = res is not None

    def body(a_ref, b_ref, *rest):
        if has_res:
            r_ref, o_ref, acc = rest
        else:
            o_ref, acc = rest

        def finish(total):
            if has_res:
                total = total + r_ref[...]
            o_ref[...] = total.astype(o_ref.dtype)

        part = _dot(a_ref[...], b_ref[...], dims)
        if nk == 1:
            finish(part)
        else:
            k = pl.program_id(2)

            @pl.when(k == 0)
            def _():
                acc[...] = part

            @pl.when(k > 0)
            def _():
                acc[...] += part

            @pl.when(k == nk - 1)
            def _():
                finish(acc[...])

    grid = (M // tm, N // tn, nk)
    operands = [a, b] + ([res] if has_res else [])
    in_specs = [a_spec, b_spec] + ([o_spec] if has_res else [])
    out_specs, out_shapes = [o_spec], [out_shape]
    scratch = [pltpu.VMEM((tm, tn) if nk > 1 else (8, LANES), F32)]
    sem = ("parallel", "parallel", "arbitrary")
    if side is not None:
        body = _host(side, body, grid, len(operands), 1, 1)
        operands, in_specs = operands + side.operands, in_specs + [ANY] * len(side.operands)
        out_specs, out_shapes = out_specs + [ANY] * len(side.out_shapes), out_shapes + side.out_shapes
        scratch = scratch + side.sem_shapes
        sem = ("arbitrary",) * 3
    outs = pl.pallas_call(
        body, name=name, grid=grid, in_specs=in_specs, out_specs=out_specs, out_shape=out_shapes,
        scratch_shapes=scratch, compiler_params=_params(*sem),
    )(*operands)
    return outs[0] if side is None else (outs[0], list(outs[1:]))


def _rmsnorm_fwd(x, g, out_dtype, name):
    S, D = x.shape
    tm = _tile(S, 512, 16)

    def body(x_ref, g_ref, o_ref):
        xv = x_ref[...]
        r = lax.rsqrt(jnp.mean(xv * xv, axis=-1, keepdims=True) + RMS_EPS)
        o_ref[...] = ((xv * r) * g_ref[...]).astype(o_ref.dtype)

    row = pl.BlockSpec((tm, D), lambda i: (i, 0))
    return pl.pallas_call(
        body, name=name, grid=(S // tm,), in_specs=[row, pl.BlockSpec((1, D), lambda i: (0, 0))],
        out_specs=row, out_shape=jax.ShapeDtypeStruct((S, D), out_dtype),
        compiler_params=_params("parallel"),
    )(x, g)


def _rmsnorm_bwd(x, g, dh, dres, name):
    S, D = x.shape
    tm = _tile(S, 256, 16)

    def body(x_ref, g_ref, dh_ref, dres_ref, dx_ref, dxb_ref, dg_ref):
        xv = x_ref[...]
        r = lax.rsqrt(jnp.mean(xv * xv, axis=-1, keepdims=True) + RMS_EPS)
        xhat = xv * r
        dhv = dh_ref[...]
        dxhat = dhv * g_ref[...]
        dx = dres_ref[...] + r * (dxhat - xhat * jnp.mean(dxhat * xhat, axis=-1, keepdims=True))
        dx_ref[...] = dx
        dxb_ref[...] = dx.astype(BF16)

        @pl.when(pl.program_id(0) == 0)
        def _():
            dg_ref[...] = jnp.zeros_like(dg_ref)

        dg_ref[...] += jnp.sum(dhv * xhat, axis=0, keepdims=True)

    row = pl.BlockSpec((tm, D), lambda i: (i, 0))
    vec = pl.BlockSpec((1, D), lambda i: (0, 0))
    return pl.pallas_call(
        body, name=name, grid=(S // tm,), in_specs=[row, vec, row, row], out_specs=[row, row, vec],
        out_shape=[jax.ShapeDtypeStruct((S, D), F32), jax.ShapeDtypeStruct((S, D), BF16),
                   jax.ShapeDtypeStruct((1, D), F32)],
        compiler_params=_params("arbitrary"),
    )(x, g, dh, dres)


def _loss_grad(y, target, name):
    S, D = y.shape
    tm = _tile(S, 256, 16)

    def body(y_ref, t_ref, dy_ref, dyb_ref, sq_ref):
        e = y_ref[...] - t_ref[...]
        dy = e / D
        dy_ref[...] = dy
        dyb_ref[...] = dy.astype(BF16)

        @pl.when(pl.program_id(0) == 0)
        def _():
            sq_ref[...] = jnp.zeros_like(sq_ref)

        total = jnp.sum(jnp.sum(e * e, axis=1, keepdims=True), axis=0, keepdims=True)
        sq_ref[...] += jnp.broadcast_to(total, sq_ref.shape)

    row = pl.BlockSpec((tm, D), lambda i: (i, 0))
    vec = pl.BlockSpec((1, LANES), lambda i: (0, 0))
    return pl.pallas_call(
        body, name=name, grid=(S // tm,), in_specs=[row, row], out_specs=[row, row, vec],
        out_shape=[jax.ShapeDtypeStruct((S, D), F32), jax.ShapeDtypeStruct((S, D), BF16),
                   jax.ShapeDtypeStruct((1, LANES), F32)],
        compiler_params=_params("arbitrary"),
    )(y, target)


def _ffn_up(hb, w_gu, name, side=None):
    S, D = hb.shape
    F = w_gu.shape[1] // 2
    tm, tn, tk = _tile(S, 1024), _tile(F, 512), _tile(D, 2048)
    nk, nj = D // tk, F // tn

    def body(a_ref, wg_ref, wu_ref, gu_ref, act_ref, accg, accu):
        pg = _dot(a_ref[...], wg_ref[...], NN)
        pu = _dot(a_ref[...], wu_ref[...], NN)

        def finish(g, u):
            gu_ref[0] = g
            gu_ref[1] = u
            act_ref[...] = ((g / (1.0 + jnp.exp(-g))) * u).astype(BF16)

        if nk == 1:
            finish(pg, pu)
        else:
            k = pl.program_id(2)

            @pl.when(k == 0)
            def _():
                accg[...] = pg
                accu[...] = pu

            @pl.when(k > 0)
            def _():
                accg[...] += pg
                accu[...] += pu

            @pl.when(k == nk - 1)
            def _():
                finish(accg[...], accu[...])

    acc_shape = (tm, tn) if nk > 1 else (8, LANES)
    grid = (S // tm, nj, nk)
    operands = [hb, w_gu, w_gu]
    in_specs = [pl.BlockSpec((tm, tk), lambda i, j, k: (i, k)),
                pl.BlockSpec((tk, tn), lambda i, j, k: (k, j)),
                pl.BlockSpec((tk, tn), lambda i, j, k: (k, j + nj))]
    out_specs = [pl.BlockSpec((2, tm, tn), lambda i, j, k: (0, i, j)),
                 pl.BlockSpec((tm, tn), lambda i, j, k: (i, j))]
    out_shapes = [jax.ShapeDtypeStruct((2, S, F), F32), jax.ShapeDtypeStruct((S, F), BF16)]
    scratch = [pltpu.VMEM(acc_shape, F32), pltpu.VMEM(acc_shape, F32)]
    sem = ("parallel", "parallel", "arbitrary")
    if side is not None:
        body = _host(side, body, grid, 3, 2, 2)
        operands, in_specs = operands + side.operands, in_specs + [ANY] * len(side.operands)
        out_specs, out_shapes = out_specs + [ANY] * len(side.out_shapes), out_shapes + side.out_shapes
        scratch = scratch + side.sem_shapes
        sem = ("arbitrary",) * 3
    outs = pl.pallas_call(
        body, name=name, grid=grid, in_specs=in_specs, out_specs=out_specs, out_shape=out_shapes,
        scratch_shapes=scratch, compiler_params=_params(*sem),
    )(*operands)
    return outs[0], outs[1], list(outs[2:])


def _ffn_act_bwd(dyb, w_down, gu, name):
    S, D = dyb.shape
    F = w_down.shape[0]
    tm, tn, tk = _tile(S, 1024), _tile(F, 512), _tile(D, 2048)
    nk = D // tk

    def body(a_ref, w_ref, gu_ref, dgu_ref, acc):
        part = _dot(a_ref[...], w_ref[...], NT)

        def finish(dact):
            g = gu_ref[0]
            u = gu_ref[1]
            sg = 1.0 / (1.0 + jnp.exp(-g))
            dgu_ref[0] = (dact * u * (sg * (1.0 + g * (1.0 - sg)))).astype(BF16)
            dgu_ref[1] = (dact * (g * sg)).astype(BF16)

        if nk == 1:
            finish(part)
        else:
            k = pl.program_id(2)

            @pl.when(k == 0)
            def _():
                acc[...] = part

            @pl.when(k > 0)
            def _():
                acc[...] += part

            @pl.when(k == nk - 1)
            def _():
                finish(acc[...])

    gu_spec = pl.BlockSpec((2, tm, tn), lambda i, j, k: (0, i, j))
    return pl.pallas_call(
        body, name=name, grid=(S // tm, F // tn, nk),
        in_specs=[pl.BlockSpec((tm, tk), lambda i, j, k: (i, k)),
                  pl.BlockSpec((tn, tk), lambda i, j, k: (j, k)), gu_spec],
        out_specs=gu_spec, out_shape=jax.ShapeDtypeStruct((2, S, F), BF16),
        scratch_shapes=[pltpu.VMEM((tm, tn) if nk > 1 else (8, LANES), F32)],
        compiler_params=_params("parallel", "parallel", "arbitrary"),
    )(dyb, w_down, gu)


LOG2E = 1.4426950408889634
Q_PRESCALE = HEAD_DIM ** -0.5 * LOG2E
AUX_A = 0
AUX_B = 3


def _aux_block(lane, minus=None, minus_at=None, ones_at=None):
    out = jnp.zeros(lane.shape, BF16)
    if minus is not None:
        x = -minus
        hi = x.astype(BF16)
        r1 = x - hi.astype(F32)
        mid = r1.astype(BF16)
        lo = (r1 - mid.astype(F32)).astype(BF16)
        for n, piece in enumerate((hi, mid, lo)):
            out = jnp.where(lane == minus_at + n, piece, out)
    if ones_at is not None:
        out = jnp.where((lane >= ones_at) & (lane < ones_at + 3), jnp.ones(lane.shape, BF16), out)
    return out


def _qkv_post(proj, gq, gk, cum, name):
    S, D3 = proj.shape
    D = D3 // 3
    tm = _tile(S, 256, 16)

    def body(q_ref, k_ref, v_ref, gq_ref, gk_ref, c_ref, qn_ref, kn_ref, vb_ref, ak_ref):
        lane = lax.broadcasted_iota(jnp.int32, (tm, HEAD_DIM), 1)
        for src, g_ref, dst, mult in ((q_ref, gq_ref, qn_ref, Q_PRESCALE), (k_ref, gk_ref, kn_ref, None)):
            for h in range(D // HEAD_DIM):
                cols = slice(h * HEAD_DIM, (h + 1) * HEAD_DIM)
                t = src[:, cols]
                r = lax.rsqrt(jnp.mean(t * t, axis=-1, keepdims=True) + RMS_EPS)
                n = (t * r) * g_ref[...]
                dst[:, cols] = (n if mult is None else n * mult).astype(BF16)
        for h in range(D // HEAD_DIM):
            ch = jnp.broadcast_to(c_ref[:, h:h + 1] * LOG2E, (tm, HEAD_DIM))
            ak_ref[:, h * HEAD_DIM:(h + 1) * HEAD_DIM] = _aux_block(lane, ch, AUX_A, AUX_B)
        vb_ref[...] = v_ref[...].astype(BF16)

    part = lambda n: pl.BlockSpec((tm, D), lambda i: (i, n))
    vec = pl.BlockSpec((1, HEAD_DIM), lambda i: (0, 0))
    row = pl.BlockSpec((tm, D), lambda i: (i, 0))
    return pl.pallas_call(
        body, name=name, grid=(S // tm,),
        in_specs=[part(0), part(1), part(2), vec, vec, pl.BlockSpec((tm, LANES), lambda i: (i, 0))],
        out_specs=[row, row, row, row], out_shape=[jax.ShapeDtypeStruct((S, D), BF16)] * 4,
        compiler_params=_params("parallel"),
    )(proj, proj, proj, gq, gk, cum)


def _qkv_post_bwd(proj, dqx, dkx, dvb, gq, gk, name):
    S, D = dvb.shape
    tm = _tile(S, 128, 16)

    def body(q_ref, k_ref, dqx_ref, dkx_ref, dvb_ref, gq_ref, gk_ref, dp_ref, dgq_ref, dgk_ref, dc_ref):
        @pl.when(pl.program_id(0) == 0)
        def _():
            dgq_ref[...] = jnp.zeros_like(dgq_ref)
            dgk_ref[...] = jnp.zeros_like(dgk_ref)

        lane = lax.broadcasted_iota(jnp.int32, (tm, LANES), 1)
        dc = jnp.zeros((tm, LANES), F32)
        for h in range(D // HEAD_DIM):
            at = (2 * h + 1) * HEAD_DIM
            diff = dqx_ref[:, at + AUX_B:at + AUX_B + 1] - dkx_ref[:, at + AUX_A:at + AUX_A + 1]
            dc = jnp.where(lane == h, jnp.broadcast_to(diff, (tm, LANES)), dc)
        dc_ref[...] = dc

        for n, (src, dsrc, g_ref, dg_ref, mult) in enumerate(
                ((q_ref, dqx_ref, gq_ref, dgq_ref, HEAD_DIM ** -0.5), (k_ref, dkx_ref, gk_ref, dgk_ref, 1.0 / LOG2E))):
            dg = jnp.zeros((1, HEAD_DIM), F32)
            for h in range(D // HEAD_DIM):
                cols = slice(h * HEAD_DIM, (h + 1) * HEAD_DIM)
                t = src[:, cols]
                r = lax.rsqrt(jnp.mean(t * t, axis=-1, keepdims=True) + RMS_EPS)
                that = t * r
                dn = dsrc[:, 2 * h * HEAD_DIM:(2 * h + 1) * HEAD_DIM] * mult
                dhat = dn * g_ref[...]
                dt = r * (dhat - that * jnp.mean(dhat * that, axis=-1, keepdims=True))
                dp_ref[:, n * D + h * HEAD_DIM:n * D + (h + 1) * HEAD_DIM] = dt.astype(BF16)
                dg = dg + jnp.sum(dn * that, axis=0, keepdims=True)
            dg_ref[...] += dg
        dp_ref[:, 2 * D:3 * D] = dvb_ref[...]

    part = lambda n: pl.BlockSpec((tm, D), lambda i: (i, n))
    row = pl.BlockSpec((tm, D), lambda i: (i, 0))
    vec = pl.BlockSpec((1, HEAD_DIM), lambda i: (0, 0))
    wide = pl.BlockSpec((tm, 2 * D), lambda i: (i, 0))
    return pl.pallas_call(
        body, name=name, grid=(S // tm,), in_specs=[part(0), part(1), wide, wide, row, vec, vec],
        out_specs=[pl.BlockSpec((tm, 3 * D), lambda i: (i, 0)), vec, vec,
                   pl.BlockSpec((tm, LANES), lambda i: (i, 0))],
        out_shape=[jax.ShapeDtypeStruct((S, 3 * D), BF16), jax.ShapeDtypeStruct((1, HEAD_DIM), F32),
                   jax.ShapeDtypeStruct((1, HEAD_DIM), F32), jax.ShapeDtypeStruct((S, LANES), F32)],
        compiler_params=_params("arbitrary"),
    )(proj, proj, dqx, dkx, dvb, gq, gk)


def _log_sigmoid(z):
    return -(jnp.maximum(-z, 0.0) + jnp.log(1.0 + jnp.exp(-jnp.abs(z))))


def _fgate_fwd(fl, bf, name):
    S = fl.shape[0]
    T = _tile(S, 256, 16)

    def body(fl_ref, bf_ref, c_ref, carry):
        @pl.when(pl.program_id(0) == 0)
        def _():
            carry[...] = jnp.zeros_like(carry)

        logf = _log_sigmoid(fl_ref[...] + bf_ref[...])
        r = lax.broadcasted_iota(jnp.int32, (T, T), 0)
        c = lax.broadcasted_iota(jnp.int32, (T, T), 1)
        tri = (r >= c).astype(F32)
        cum = jnp.dot(tri, logf, precision=lax.Precision.HIGHEST, preferred_element_type=F32) + carry[...]
        c_ref[...] = cum
        carry[...] = cum[T - 1:T, :]

    row = pl.BlockSpec((T, LANES), lambda i: (i, 0))
    return pl.pallas_call(
        body, name=name, grid=(S // T,), in_specs=[row, pl.BlockSpec((1, LANES), lambda i: (0, 0))],
        out_specs=row, out_shape=jax.ShapeDtypeStruct((S, LANES), F32),
        scratch_shapes=[pltpu.VMEM((1, LANES), F32)], compiler_params=_params("arbitrary"),
    )(fl, bf)


def _fgate_bwd(dc, fl, bf, name):
    S = fl.shape[0]
    T = _tile(S, 256, 16)
    nb = S // T

    def body(dc_ref, fl_ref, bf_ref, dfl_ref, dbf_ref, carry):
        @pl.when(pl.program_id(0) == 0)
        def _():
            carry[...] = jnp.zeros_like(carry)
            dbf_ref[...] = jnp.zeros_like(dbf_ref)

        r = lax.broadcasted_iota(jnp.int32, (T, T), 0)
        c = lax.broadcasted_iota(jnp.int32, (T, T), 1)
        triu = (c >= r).astype(F32)
        dlogf = jnp.dot(triu, dc_ref[...], precision=lax.Precision.HIGHEST,
                        preferred_element_type=F32) + carry[...]
        carry[...] = dlogf[0:1, :]
        z = fl_ref[...] + bf_ref[...]
        dz = dlogf * (1.0 / (1.0 + jnp.exp(z)))
        dfl_ref[...] = dz.astype(BF16)
        dbf_ref[...] += jnp.sum(dz, axis=0, keepdims=True)

    row = pl.BlockSpec((T, LANES), lambda i: (nb - 1 - i, 0))
    vec = pl.BlockSpec((1, LANES), lambda i: (0, 0))
    return pl.pallas_call(
        body, name=name, grid=(nb,), in_specs=[row, row, vec], out_specs=[row, vec],
        out_shape=[jax.ShapeDtypeStruct((S, LANES), BF16), jax.ShapeDtypeStruct((1, LANES), F32)],
        scratch_shapes=[pltpu.VMEM((1, LANES), F32)], compiler_params=_params("arbitrary"),
    )(dc, fl, bf)


def _attn_logits(q_ref, aq, k_ref, ak, masked, T):
    qf = jnp.concatenate([q_ref[...], aq], axis=1)
    kf = jnp.concatenate([k_ref[...], ak], axis=1)
    s = _dot(qf, kf, NT)
    if masked:
        r = lax.broadcasted_iota(jnp.int32, (T, T), 0)
        c = lax.broadcasted_iota(jnp.int32, (T, T), 1)
        s = jnp.where(r >= c, s, NEG_INF)
    return s, qf, kf


def _causal_pairs(nb, q_major):
    pairs = ([(i, j) for i in range(nb) for j in range(i + 1)] if q_major
             else [(i, j) for j in range(nb) for i in range(j, nb)])
    return (jnp.asarray(np.array([p[0] for p in pairs], np.int32)),
            jnp.asarray(np.array([p[1] for p in pairs], np.int32)))


def _flash_fwd(q2, kn, vb, aux_k, name, side=None):
    S, D = q2.shape
    H = D // HEAD_DIM
    T = _tile(S, 1024)
    nb = S // T
    q_idx, k_idx = _causal_pairs(nb, True)

    def body(qi_ref, kj_ref, q_ref, k_ref, v_ref, ak_ref, o_ref, aq_ref, m_s, acc_s):
        t = pl.program_id(1)
        i = qi_ref[t]
        j = kj_ref[t]
        lane = lax.broadcasted_iota(jnp.int32, (T, HEAD_DIM), 1)

        @pl.when(j == 0)
        def _():
            m_s[...] = jnp.full_like(m_s, NEG_INF)
            acc_s[...] = jnp.zeros_like(acc_s)

        def step(masked):
            s, _, _ = _attn_logits(q_ref, _aux_block(lane, ones_at=AUX_A), k_ref, ak_ref[...], masked, T)
            m_prev = m_s[...]
            m_new = jnp.maximum(m_prev, jnp.max(s, axis=1, keepdims=True))
            p = jnp.exp2((s - jnp.tile(m_new, (1, T // HEAD_DIM))).astype(BF16))
            alpha = jnp.exp2(m_prev - m_new)
            vf = jnp.concatenate([v_ref[...], jnp.ones((T, HEAD_DIM), BF16)], axis=1)
            acc_s[...] = jnp.tile(alpha, (1, 2)) * acc_s[...] + _dot(p.astype(BF16), vf, NN)
            m_s[...] = m_new

        @pl.when(j < i)
        def _():
            step(False)

        @pl.when(j == i)
        def _():
            step(True)
            l = acc_s[:, HEAD_DIM:]
            o_ref[...] = (acc_s[:, :HEAD_DIM] / l).astype(BF16)
            aq_ref[...] = _aux_block(lane, m_s[...] + jnp.log2(l), AUX_B, AUX_A)

    qspec = pl.BlockSpec((T, HEAD_DIM), lambda h, t, qi, kj: (qi[t], h))
    kspec = pl.BlockSpec((T, HEAD_DIM), lambda h, t, qi, kj: (kj[t], h))
    grid = (H, int(q_idx.shape[0]))
    operands = [q_idx, k_idx, q2, kn, vb, aux_k]
    in_specs, out_specs = [qspec, kspec, kspec, kspec], [qspec, qspec]
    out_shapes = [jax.ShapeDtypeStruct((S, D), BF16), jax.ShapeDtypeStruct((S, D), BF16)]
    scratch = [pltpu.VMEM((T, HEAD_DIM), F32), pltpu.VMEM((T, 2 * HEAD_DIM), F32)]
    sem = ("parallel", "arbitrary")
    if side is not None:
        body = _host(side, body, grid, 6, 2, 2)
        operands, in_specs = operands + side.operands, in_specs + [ANY] * len(side.operands)
        out_specs, out_shapes = out_specs + [ANY] * len(side.out_shapes), out_shapes + side.out_shapes
        scratch = scratch + side.sem_shapes
        sem = ("arbitrary", "arbitrary")
    grid_spec = pltpu.PrefetchScalarGridSpec(
        num_scalar_prefetch=2, grid=grid, in_specs=in_specs, out_specs=out_specs, scratch_shapes=scratch)
    outs = pl.pallas_call(
        body, name=name, grid_spec=grid_spec, out_shape=out_shapes, compiler_params=_params(*sem),
    )(*operands)
    return outs[0], outs[1], list(outs[2:])


def _attn_delta(do, o, name):
    S, D = do.shape
    H = D // HEAD_DIM
    tm = _tile(S, 256, 16)

    def body(do_ref, o_ref, ad_ref, dob_ref):
        lane = lax.broadcasted_iota(jnp.int32, (tm, HEAD_DIM), 1)
        for h in range(H):
            cols = slice(h * HEAD_DIM, (h + 1) * HEAD_DIM)
            delta = jnp.sum(do_ref[:, cols] * o_ref[:, cols].astype(F32), axis=1, keepdims=True)
            ad_ref[:, cols] = _aux_block(lane, jnp.broadcast_to(delta, (tm, HEAD_DIM)), AUX_A)
        dob_ref[...] = do_ref[...].astype(BF16)

    row = pl.BlockSpec((tm, D), lambda i: (i, 0))
    return pl.pallas_call(
        body, name=name, grid=(S // tm,), in_specs=[row, row], out_specs=[row, row],
        out_shape=[jax.ShapeDtypeStruct((S, D), BF16), jax.ShapeDtypeStruct((S, D), BF16)],
        compiler_params=_params("parallel"),
    )(do, o)


def _flash_bwd(q2, kn, vb, dob, aux_q, aux_k, aux_do, name, side=None):
    S, D = q2.shape
    H = D // HEAD_DIM
    T = _tile(S, 1024)
    nb = S // T
    q_idx, k_idx = _causal_pairs(nb, False)

    def body(qi_ref, kj_ref, q_ref, k_ref, v_ref, do_ref, aq_ref, ak_ref, ad_ref,
             dqx_ref, dkx_ref, dv_ref, dv_acc):
        t = pl.program_id(1)
        i = qi_ref[t]
        j = kj_ref[t]
        lane = lax.broadcasted_iota(jnp.int32, (T, HEAD_DIM), 1)

        @pl.when(t == 0)
        def _():
            dqx_ref[...] = jnp.zeros_like(dqx_ref)

        @pl.when(i == j)
        def _():
            dkx_ref[...] = jnp.zeros_like(dkx_ref)
            dv_acc[...] = jnp.zeros_like(dv_acc)

        def step(masked):
            s, qf, kf = _attn_logits(q_ref, aq_ref[...], k_ref, ak_ref[...], masked, T)
            p = jnp.exp2(s)
            dof = jnp.concatenate([do_ref[...], ad_ref[...]], axis=1)
            vf = jnp.concatenate([v_ref[...], _aux_block(lane, ones_at=AUX_A)], axis=1)
            ds = p * _dot(dof, vf, NT)
            dsb = ds.astype(BF16)
            dv_acc[...] += _dot(p.astype(BF16), do_ref[...], TN)
            dkx_ref[...] += _dot(dsb, qf, TN)
            rows = pl.ds(pl.multiple_of(i * T, T), T)
            dqx_ref[rows, :] += _dot(dsb, kf, NN)

        @pl.when(i > j)
        def _():
            step(False)

        @pl.when(i == j)
        def _():
            step(True)

        @pl.when(i == nb - 1)
        def _():
            dv_ref[...] = dv_acc[...].astype(BF16)

    qspec = pl.BlockSpec((T, HEAD_DIM), lambda h, t, qi, kj: (qi[t], h))
    kspec = pl.BlockSpec((T, HEAD_DIM), lambda h, t, qi, kj: (kj[t], h))
    grid = (H, int(q_idx.shape[0]))
    operands = [q_idx, k_idx, q2, kn, vb, dob, aux_q, aux_k, aux_do]
    in_specs = [qspec, kspec, kspec, qspec, qspec, kspec, qspec]
    out_specs = [pl.BlockSpec((S, 2 * HEAD_DIM), lambda h, t, qi, kj: (0, h)),
                 pl.BlockSpec((T, 2 * HEAD_DIM), lambda h, t, qi, kj: (kj[t], h)), kspec]
    out_shapes = [jax.ShapeDtypeStruct((S, 2 * D), F32), jax.ShapeDtypeStruct((S, 2 * D), F32),
                  jax.ShapeDtypeStruct((S, D), BF16)]
    scratch = [pltpu.VMEM((T, HEAD_DIM), F32)]
    sem = ("parallel", "arbitrary")
    if side is not None:
        body = _host(side, body, grid, 9, 3, 1)
        operands, in_specs = operands + side.operands, in_specs + [ANY] * len(side.operands)
        out_specs, out_shapes = out_specs + [ANY] * len(side.out_shapes), out_shapes + side.out_shapes
        scratch = scratch + side.sem_shapes
        sem = ("arbitrary", "arbitrary")
    grid_spec = pltpu.PrefetchScalarGridSpec(
        num_scalar_prefetch=2, grid=grid, in_specs=in_specs, out_specs=out_specs, scratch_shapes=scratch)
    outs = pl.pallas_call(
        body, name=name, grid_spec=grid_spec, out_shape=out_shapes, compiler_params=_params(*sem),
    )(*operands)
    return outs[0], outs[1], outs[2], list(outs[3:])


def _pool_counts(first_row, tm, win):
    t = first_row + lax.broadcasted_iota(jnp.int32, (tm, 1), 0)
    return jnp.minimum(t + 1, win).astype(F32)


def _pool_fwd(h, x, wp, b, scale, name):
    S, D = h.shape
    G = D // len(POOL_WINDOWS)
    tm = _tile(S, 256, 16)

    def body(h_ref, halo_ref, x_ref, w_ref, b_ref, s_ref, y_ref, o_ref, ext):
        i = pl.program_id(0)
        ext[POOL_HALO:, :] = h_ref[...]

        @pl.when(i == 0)
        def _():
            ext[0:POOL_HALO, :] = jnp.zeros((POOL_HALO, D), F32)

        @pl.when(i > 0)
        def _():
            ext[0:POOL_HALO, :] = halo_ref[...]

        for g, win in enumerate(POOL_WINDOWS):
            cols = slice(g * G, (g + 1) * G)
            tot = ext[POOL_HALO:POOL_HALO + tm, cols]
            for k in range(1, win):
                tot = tot + ext[POOL_HALO - k:POOL_HALO - k + tm, cols]
            y = (tot / _pool_counts(i * tm, tm, win) - h_ref[:, cols]).astype(BF16)
            y_ref[:, cols] = y
            z = _dot(y, w_ref[g], NN)
            o_ref[:, cols] = x_ref[:, cols] + (z + b_ref[:, cols]) * s_ref[:, cols]

    row = pl.BlockSpec((tm, D), lambda i: (i, 0))
    vec = pl.BlockSpec((1, D), lambda i: (0, 0))
    halo = pl.BlockSpec((POOL_HALO, D), lambda i: (jnp.maximum(i * (tm // POOL_HALO) - 1, 0), 0))
    return pl.pallas_call(
        body, name=name, grid=(S // tm,),
        in_specs=[row, halo, row, pl.BlockSpec((len(POOL_WINDOWS), G, G), lambda i: (0, 0, 0)), vec, vec],
        out_specs=[row, row],
        out_shape=[jax.ShapeDtypeStruct((S, D), BF16), jax.ShapeDtypeStruct((S, D), F32)],
        scratch_shapes=[pltpu.VMEM((tm + POOL_HALO, D), F32)], compiler_params=_params("parallel"),
    )(h, h, x, wp, b, scale)


def _pool_bwd_mix(dout, yb, wp, b, scale, name):
    S, D = dout.shape
    NG = len(POOL_WINDOWS)
    G = D // NG
    tm = _tile(S, 256, 16)

    def body(do_ref, y_ref, w_ref, b_ref, s_ref, dyc_ref, dw_ref, db_ref, ds_ref):
        i = pl.program_id(0)

        @pl.when(i == 0)
        def _():
            dw_ref[...] = jnp.zeros_like(dw_ref)
            db_ref[...] = jnp.zeros_like(db_ref)
            ds_ref[...] = jnp.zeros_like(ds_ref)

        for g, win in enumerate(POOL_WINDOWS):
            cols = slice(g * G, (g + 1) * G)
            y = y_ref[:, cols]
            dz = do_ref[:, cols]
            zb = _dot(y, w_ref[g], NN) + b_ref[:, cols]
            ds_ref[:, cols] += jnp.sum(dz * zb, axis=0, keepdims=True)
            dzb = dz * s_ref[:, cols]
            db_ref[:, cols] += jnp.sum(dzb, axis=0, keepdims=True)
            dzb16 = dzb.astype(BF16)
            dw_ref[g] += _dot(y, dzb16, TN)
            dy = _dot(dzb16, w_ref[g], NT)
            dyc_ref[:, cols] = dy / _pool_counts(i * tm, tm, win)

    row = pl.BlockSpec((tm, D), lambda i: (i, 0))
    vec = pl.BlockSpec((1, D), lambda i: (0, 0))
    wspec = pl.BlockSpec((NG, G, G), lambda i: (0, 0, 0))
    return pl.pallas_call(
        body, name=name, grid=(S // tm,), in_specs=[row, row, wspec, vec, vec],
        out_specs=[row, wspec, vec, vec],
        out_shape=[jax.ShapeDtypeStruct((S, D), F32), jax.ShapeDtypeStruct((NG, G, G), F32),
                   jax.ShapeDtypeStruct((1, D), F32), jax.ShapeDtypeStruct((1, D), F32)],
        compiler_params=_params("arbitrary"),
    )(dout, yb, wp, b, scale)


def _pool_bwd_window(dyc, name):
    S, D = dyc.shape
    G = D // len(POOL_WINDOWS)
    tm = _tile(S, 256, 16)
    nb = S // tm

    def body(d_ref, halo_ref, dh_ref, ext):
        i = pl.program_id(0)
        ext[0:tm, :] = d_ref[...]

        @pl.when(i == nb - 1)
        def _():
            ext[tm:tm + POOL_HALO, :] = jnp.zeros((POOL_HALO, D), F32)

        @pl.when(i < nb - 1)
        def _():
            ext[tm:tm + POOL_HALO, :] = halo_ref[...]

        for g, win in enumerate(POOL_WINDOWS):
            cols = slice(g * G, (g + 1) * G)
            tot = ext[0:tm, cols] * (1.0 - _pool_counts(i * tm, tm, win))
            for k in range(1, win):
                tot = tot + ext[k:k + tm, cols]
            dh_ref[:, cols] = tot

    row = pl.BlockSpec((tm, D), lambda i: (i, 0))
    halo = pl.BlockSpec((POOL_HALO, D),
                        lambda i: (jnp.minimum((i + 1) * (tm // POOL_HALO), S // POOL_HALO - 1), 0))
    return pl.pallas_call(
        body, name=name, grid=(nb,), in_specs=[row, halo], out_specs=row,
        out_shape=jax.ShapeDtypeStruct((S, D), F32),
        scratch_shapes=[pltpu.VMEM((tm + POOL_HALO, D), F32)], compiler_params=_params("parallel"),
    )(dyc, dyc)


def _adamw(w, g, m, v, name):
    R, C = w.shape
    tr = _row_tile(R, C, 4, 1 << 20)

    def body(w_ref, g_ref, m_ref, v_ref, d_ref, nm_ref, nv_ref):
        gv = g_ref[...]
        m_new = ADAM_B1 * m_ref[...] + (1.0 - ADAM_B1) * gv
        v_new = ADAM_B2 * v_ref[...] + (1.0 - ADAM_B2) * (gv * gv)
        m_hat = m_new / (1.0 - ADAM_B1 ** ADAM_STEP)
        v_hat = v_new / (1.0 - ADAM_B2 ** ADAM_STEP)
        d_ref[...] = -ADAM_LR * (m_hat / (jnp.sqrt(v_hat) + ADAM_EPS) + ADAM_WD * w_ref[...])
        nm_ref[...] = m_new
        nv_ref[...] = v_new

    row = pl.BlockSpec((tr, C), lambda i: (i, 0))
    return pl.pallas_call(
        body, name=name, grid=(R // tr,), in_specs=[row] * 4, out_specs=[row] * 3,
        out_shape=[jax.ShapeDtypeStruct((R, C), F32)] * 3, compiler_params=_params("parallel"),
    )(w, g, m, v)


def _sum_core_halves(g, r1, c_idx, name):
    _, _, Rh, C = g.shape
    tr = _row_tile(Rh, C, 4, 2 << 20)

    def body(c_ref, g_ref, r_ref, o_ref, ob_ref):
        total = g_ref[...] + r_ref[...]
        o_ref[...] = total
        ob_ref[...] = total.astype(BF16)

    piece = pl.BlockSpec((None, tr, C), lambda s, r, c_ref: (s, r, 0))
    grid_spec = pltpu.PrefetchScalarGridSpec(
        num_scalar_prefetch=1, grid=(N_CHIPS, Rh // tr),
        in_specs=[pl.BlockSpec((None, None, tr, C), lambda s, r, c_ref: (s, c_ref[0], r, 0)), piece],
        out_specs=[piece, piece])
    return pl.pallas_call(
        body, name=name, grid_spec=grid_spec,
        out_shape=[jax.ShapeDtypeStruct((N_CHIPS, Rh, C), F32), jax.ShapeDtypeStruct((N_CHIPS, Rh, C), BF16)],
        compiler_params=_params("parallel", "parallel"),
    )(c_idx, g, r1)


def _sum_chips(h, r2, place, name):
    _, Rh, C = h.shape
    tr = _row_tile(Rh, C, 4, 1 << 20)

    def body(place_ref, h_ref, r_ref, o_ref):
        o_ref[...] = ((h_ref[...] + r_ref[0].astype(F32)) + r_ref[1].astype(F32)) + r_ref[2].astype(F32)

    grid_spec = pltpu.PrefetchScalarGridSpec(
        num_scalar_prefetch=1, grid=(Rh // tr,),
        in_specs=[pl.BlockSpec((None, tr, C), lambda r, pr: (pr[0], r, 0)),
                  pl.BlockSpec((N_CHIPS - 1, tr, C), lambda r, pr: (0, r, 0))],
        out_specs=pl.BlockSpec((None, tr, C), lambda r, pr: (pr[1], r, 0)))
    return pl.pallas_call(
        body, name=name, grid_spec=grid_spec, out_shape=jax.ShapeDtypeStruct((2, Rh, C), F32),
        compiler_params=_params("parallel"),
    )(place, h, r2)


ANY = pl.BlockSpec(memory_space=pl.ANY)


def _place():
    x, y, c = lax.axis_index("x"), lax.axis_index("y"), lax.axis_index("c")
    chips = [(1 - x, y), (x, 1 - y), (1 - x, 1 - y)]
    return x, y, c, chips, [2 * cx + cy for cx, cy in chips]


def _run_side(side, name):
    def body(*refs):
        _, side_refs = side.split(refs, 0, 0, 0)
        side.start(*side_refs)
        side.finish(*side_refs)

    return pl.pallas_call(
        body, name=name, in_specs=[ANY] * len(side.operands), out_specs=[ANY] * len(side.out_shapes),
        out_shape=side.out_shapes, scratch_shapes=side.sem_shapes,
    )(*side.operands)


def _allgather_side(shards):
    n = len(shards)

    def copy(sems, t, k, src, dst, to):
        return pltpu.make_async_remote_copy(src_ref=src, dst_ref=dst, send_sem=sems[0].at[t, k],
                                            recv_sem=sems[1].at[t, k], device_id=to, device_id_type=MESH)

    def first_copies(ins, outs, sems):
        x, y, c, chips, _ = _place()
        me = 2 * x + y
        return [copy(sems, t, j, ins[t].at[c], outs[t].at[me, c], (*chip, c))
                for t in range(n) for j, chip in enumerate(chips)]

    def start(ins, outs, sems):
        for cp in first_copies(ins, outs, sems):
            cp.start()

    def finish(ins, outs, sems):
        x, y, c, chips, chip_idx = _place()
        sibling = (x, y, 1 - c)
        passed = []
        for t in range(n):
            for j, chip in enumerate(chips):
                landed = outs[t].at[chip_idx[j], c]
                copy(sems, t, j, landed, landed, (*chip, c)).wait_recv()
                fw = copy(sems, t, 3 + j, landed, landed, sibling)
                fw.start()
                passed.append(fw)
        for t in range(n):
            for j in range(3):
                other = outs[t].at[chip_idx[j], 1 - c]
                copy(sems, t, 3 + j, other, other, sibling).wait_recv()
        for cp in first_copies(ins, outs, sems) + passed:
            cp.wait_send()

    return _Side(shards, [jax.ShapeDtypeStruct((N_CHIPS,) + s.shape, s.dtype) for s in shards],
                 [pltpu.SemaphoreType.DMA((n, 6)), pltpu.SemaphoreType.DMA((n, 6))], start, finish)


def _send_other_half_to_sibling(gs, name):
    n = len(gs)

    def body(*refs):
        ins, outs = refs[:n], refs[n:2 * n]
        send_sems, recv_sems = refs[2 * n:]
        x, y, c, _, _ = _place()
        copies = []
        for t in range(n):
            for s in range(N_CHIPS):
                cp = pltpu.make_async_remote_copy(
                    src_ref=ins[t].at[s, 1 - c], dst_ref=outs[t].at[s], send_sem=send_sems.at[t, s],
                    recv_sem=recv_sems.at[t, s], device_id=(x, y, 1 - c), device_id_type=MESH)
                cp.start()
                copies.append(cp)
        for cp in copies:
            cp.wait()

    return pl.pallas_call(
        body, name=name, in_specs=[ANY] * n, out_specs=[ANY] * n,
        out_shape=[jax.ShapeDtypeStruct((N_CHIPS,) + g.shape[2:], g.dtype) for g in gs],
        scratch_shapes=[pltpu.SemaphoreType.DMA((n, N_CHIPS)), pltpu.SemaphoreType.DMA((n, N_CHIPS))],
    )(*gs)


def _exchange_side(hs):
    n = len(hs)

    def copies(ins, outs, sems):
        x, y, c, chips, chip_idx = _place()
        return [pltpu.make_async_remote_copy(
            src_ref=ins[t].at[chip_idx[j]], dst_ref=outs[t].at[j], send_sem=sems[0].at[t, j],
            recv_sem=sems[1].at[t, j], device_id=(*chip, c), device_id_type=MESH)
            for t in range(n) for j, chip in enumerate(chips)]

    def start(ins, outs, sems):
        for cp in copies(ins, outs, sems):
            cp.start()

    def finish(ins, outs, sems):
        for cp in copies(ins, outs, sems):
            cp.wait()

    return _Side(hs, [jax.ShapeDtypeStruct((N_CHIPS - 1,) + h.shape[1:], h.dtype) for h in hs],
                 [pltpu.SemaphoreType.DMA((n, 3)), pltpu.SemaphoreType.DMA((n, 3))], start, finish)


def _join_core_halves(bufs, name):
    n = len(bufs)

    def body(*refs):
        outs = refs[n:2 * n]
        send_sems, recv_sems = refs[2 * n:]
        x, y, c, _, _ = _place()
        sends = []
        for t in range(n):
            cp = pltpu.make_async_remote_copy(
                src_ref=outs[t].at[c], dst_ref=outs[t].at[c], send_sem=send_sems.at[t], recv_sem=recv_sems.at[t],
                device_id=(x, y, 1 - c), device_id_type=MESH)
            cp.start()
            sends.append(cp)
        for t in range(n):
            other = outs[t].at[1 - c]
            pltpu.make_async_remote_copy(
                src_ref=other, dst_ref=other, send_sem=send_sems.at[t], recv_sem=recv_sems.at[t],
                device_id=(x, y, 1 - c), device_id_type=MESH).wait_recv()
        for cp in sends:
            cp.wait_send()

    return pl.pallas_call(
        body, name=name, in_specs=[ANY] * n, out_specs=[ANY] * n,
        out_shape=[jax.ShapeDtypeStruct(b.shape, b.dtype) for b in bufs],
        input_output_aliases={t: t for t in range(n)},
        scratch_shapes=[pltpu.SemaphoreType.DMA((n,)), pltpu.SemaphoreType.DMA((n,))],
    )(*bufs)


def _allreduce_small(vec, name):
    R = vec.shape[0]

    def body(v_ref, o_ref, buf, send_sems, recv_sems):
        x, y, c = lax.axis_index("x"), lax.axis_index("y"), lax.axis_index("c")
        me = 4 * x + 2 * y + c
        buf[me] = v_ref[...]
        peers = []
        for k in range(1, N_DEV):
            px = 1 - x if k & 4 else x
            py = 1 - y if k & 2 else y
            pc = 1 - c if k & 1 else c
            peers.append(((px, py, pc), 4 * px + 2 * py + pc))
        sends = []
        for k, (peer, _) in enumerate(peers):
            cp = pltpu.make_async_remote_copy(
                src_ref=v_ref, dst_ref=buf.at[me], send_sem=send_sems.at[k], recv_sem=recv_sems.at[k],
                device_id=peer, device_id_type=MESH)
            cp.start()
            sends.append(cp)
        for k, (peer, idx) in enumerate(peers):
            pltpu.make_async_remote_copy(
                src_ref=v_ref, dst_ref=buf.at[idx], send_sem=send_sems.at[k], recv_sem=recv_sems.at[k],
                device_id=peer, device_id_type=MESH).wait_recv()
        for cp in sends:
            cp.wait_send()
        total = buf[0]
        for d in range(1, N_DEV):
            total = total + buf[d]
        o_ref[...] = total

    vm = pl.BlockSpec(memory_space=pltpu.VMEM)
    return pl.pallas_call(
        body, name=name, in_specs=[vm], out_specs=vm, out_shape=jax.ShapeDtypeStruct((R, LANES), F32),
        scratch_shapes=[pltpu.VMEM((N_DEV, R, LANES), F32), pltpu.SemaphoreType.DMA((N_DEV - 1,)),
                        pltpu.SemaphoreType.DMA((N_DEV - 1,))],
    )(vec)


def _halves(a):
    lead = 1
    for d in a.shape[:-1]:
        lead *= d
    return a.reshape(2, lead // 2, a.shape[-1])


def _cols_from_shards(g):
    return g.transpose(1, 0, 2).reshape(g.shape[1], N_CHIPS * g.shape[2])


def _shards_from_cols(w):
    return w.reshape(w.shape[0], N_CHIPS, w.shape[1] // N_CHIPS).transpose(1, 0, 2)


def _pack(parts, rows):
    flat = jnp.concatenate([p.reshape(-1).astype(F32) for p in parts])
    return jnp.pad(flat, (0, rows * LANES - flat.shape[0])).reshape(rows, LANES)


def _unpack(packed, shapes):
    flat = packed.reshape(-1)
    out, off = [], 0
    for s in shapes:
        n = 1
        for d in s:
            n *= d
        out.append(flat[off:off + n].reshape(s))
        off += n
    return out


def _packed_rows(shapes):
    n = 0
    for s in shapes:
        k = 1
        for d in s:
            k *= d
        n += k
    return -(-n // (8 * LANES)) * 8


def _pad_lanes(a):
    return jnp.pad(a, ((0, 0), (0, LANES - a.shape[1])))


def kernel(x, mix_norm_g, ffn_norm_g, fox_w_in, fox_b_f, fox_q_norm_g, fox_k_norm_g, fox_w_out, pool_w, pool_b, pool_scale, ffn_w_gate_up, ffn_w_down, loss_target, m_mix_norm_g, m_ffn_norm_g, m_fox_w_in, m_fox_b_f, m_fox_q_norm_g, m_fox_k_norm_g, m_fox_w_out, m_pool_w, m_pool_b, m_pool_scale, m_ffn_w_gate_up, m_ffn_w_down, v_mix_norm_g, v_ffn_norm_g, v_fox_w_in, v_fox_b_f, v_fox_q_norm_g, v_fox_k_norm_g, v_fox_w_out, v_pool_w, v_pool_b, v_pool_scale, v_ffn_w_gate_up, v_ffn_w_down):
    _, S, D = x.shape
    H = D // HEAD_DIM
    depth = mix_norm_g.shape[0]
    n_pool = pool_w.shape[0]
    NG = len(POOL_WINDOWS)
    G = D // NG
    F = ffn_w_down.shape[1] * N_CHIPS
    ax, ay, ac = lax.axis_index("x"), lax.axis_index("y"), lax.axis_index("c")
    chip = 2 * ax + ay
    c_idx = jnp.reshape(ac, (1,)).astype(jnp.int32)
    place = jnp.stack([chip, ac]).astype(jnp.int32)
    xs = x[0]
    target = loss_target[0]

    Dq = D // N_CHIPS
    small_fwd_shapes = [(n_pool, D), (n_pool, D)]
    placed = []
    for p in (pool_b, pool_scale):
        full = lax.dynamic_update_slice(jnp.zeros((n_pool, D), F32), p, (0, chip * Dq))
        placed.append(jnp.where(ac == 0, full, jnp.zeros_like(full)))
    pool_b_full, pool_scale_full = _unpack(
        _allreduce_small(_pack(placed, _packed_rows(small_fwd_shapes)), "gather_pool_vectors"), small_fwd_shapes)

    def weight_shards(i):
        mixer = ([fox_w_in[i // 2], fox_w_out[i // 2]] if i % 2 == 0 else [pool_w[i // 2]])
        return [_halves(w.astype(BF16)) for w in mixer + [ffn_w_gate_up[i], ffn_w_down[i]]]

    def with_own(shards, got):
        return [lax.dynamic_update_slice(g, sh[None], (chip, 0, 0, 0)) for g, sh in zip(got, shards)]

    def mixer_weights(i, shards, got):
        got = with_own(shards, got)
        if i % 2 == 1:
            return {"w_pool": got[0].reshape(N_CHIPS, NG, G // N_CHIPS, G).transpose(1, 0, 2, 3).reshape(NG, G, G)}
        w_in = _cols_from_shards(got[0].reshape(N_CHIPS, D, -1))
        return {"w_qkv": w_in[:, :3 * D], "w_f": _pad_lanes(w_in[:, 3 * D:]), "w_out": got[1].reshape(D, D)}

    def ffn_weights(shards, got):
        got = with_own(shards, got)
        return {"w_gu": _cols_from_shards(got[0].reshape(N_CHIPS, D, -1)), "w_down": got[1].reshape(F, D)}

    def layer_weights(i, shards, got):
        return {**mixer_weights(i, shards[:-2], got[:-2]), **ffn_weights(shards[-2:], got[-2:])}

    shards0 = weight_shards(0)
    weights = [mixer_weights(0, shards0[:-2], _run_side(_allgather_side(shards0[:-2]), "allgather_weights_first"))]
    own_ffn = shards0[-2:]

    saved = []
    cur = xs
    for i in range(depth):
        j = i // 2
        lw = weights[i]
        sv = {"x": cur}
        next_shards = weight_shards(i + 1) if i + 1 < depth else None
        gather_next = _allgather_side(next_shards) if next_shards is not None else None
        g_mix = mix_norm_g[i][None]
        if i % 2 == 0:
            h1b = _rmsnorm_fwd(cur, g_mix, BF16, "rmsnorm_fwd_bf16")
            proj = _matmul(h1b, lw["w_qkv"], "nn", F32, "qkv_proj", M=S, N=3 * D, K=D)
            fl = _matmul(h1b, lw["w_f"], "nn", F32, "forget_proj", M=S, N=LANES, K=D)
            bf = _pad_lanes(fox_b_f[j][None])
            cum = _fgate_fwd(fl, bf, "forget_cumsum")
            q2, kn, vb, aux_k = _qkv_post(proj, fox_q_norm_g[j][None], fox_k_norm_g[j][None], cum, "qk_norm")
            gather_own = _allgather_side(own_ffn) if own_ffn is not None else None
            o, aux_q, got = _flash_fwd(q2, kn, vb, aux_k, "fox_attention_fwd",
                                       side=_merge_sides([gather_own, gather_next]))
            if own_ffn is not None:
                lw.update(ffn_weights(own_ffn, got[:2]))
                got, own_ffn = got[2:], None
            gather_next = None
            x1 = _matmul(o, lw["w_out"], "nn", F32, "attn_out_proj", M=S, N=D, K=D, res=cur)
            sv.update(h1b=h1b, proj=proj, fl=fl, bf=bf, q2=q2, kn=kn, vb=vb, aux_k=aux_k, aux_q=aux_q, o=o)
        else:
            h1 = _rmsnorm_fwd(cur, g_mix, F32, "rmsnorm_fwd_f32")
            yb, x1 = _pool_fwd(h1, cur, lw["w_pool"], pool_b_full[j][None], pool_scale_full[j][None], "pool_fwd")
            sv.update(yb=yb)
        h2b = _rmsnorm_fwd(x1, ffn_norm_g[i][None], BF16, "rmsnorm_fwd_bf16")
        if gather_next is not None:
            gu, act, got_ffn = _ffn_up(h2b, lw["w_gu"], "ffn_gate_up_gather", side=_allgather_side(next_shards[-2:]))
            x2, got_mix = _matmul(act, lw["w_down"], "nn", F32, "ffn_down_gather", M=S, N=D, K=F, tk=2816, res=x1,
                                  side=_allgather_side(next_shards[:-2]))
            got = got_mix + got_ffn
        else:
            gu, act, _ = _ffn_up(h2b, lw["w_gu"], "ffn_gate_up")
            x2 = _matmul(act, lw["w_down"], "nn", F32, "ffn_down", M=S, N=D, K=F, tk=2816, res=x1)
        if next_shards is not None:
            weights.append(layer_weights(i + 1, next_shards, got))
        sv.update(x1=x1, h2b=h2b, gu=gu, act=act)
        saved.append(sv)
        cur = x2

    dcur, dcur_b, sq = _loss_grad(cur, target, "loss_grad")
    loss = lax.psum(sq[0, 0] * (0.5 / D), ("x", "y", "c"))

    g_mix_rows, g_ffn_rows = [None] * depth, [None] * depth
    g_bf, g_gq, g_gk = [None] * (depth - n_pool), [None] * (depth - n_pool), [None] * (depth - n_pool)
    g_pb, g_ps = [None] * n_pool, [None] * n_pool
    red_mix, red_ffn = [None] * depth, [None] * depth

    def half_sums(big, group):
        gs = [b.reshape(N_CHIPS, 2, -1, b.shape[-1]) for b in big]
        r1 = _send_other_half_to_sibling(gs, f"grads_to_sibling_{group}")
        sums = [_sum_core_halves(g, r, c_idx, "sum_core_halves") for g, r in zip(gs, r1)]
        return [s[0] for s in sums], [s[1] for s in sums]

    def finish_reduce(layer, group, hs, r2):
        rs = [_sum_chips(h, r, place, "sum_chips") for h, r in zip(hs, r2)]
        joined = _join_core_halves(rs, f"grads_join_{group}")
        if group != "ffn":
            red_mix[layer] = joined if group == "mix" else joined[:-2]
        if group != "mix":
            red_ffn[layer] = joined[-2:]

    pending = None
    for i in reversed(range(depth)):
        j = i // 2
        lw, sv = weights[i], saved[i]
        dgu = _ffn_act_bwd(dcur_b, lw["w_down"], sv["gu"], "ffn_act_bwd")
        d_w_down = _matmul(sv["act"], dcur_b, "tn", F32, "ffn_down_dw", M=F, N=D, K=S, tm=1408)
        dh2 = _matmul(
            dgu, lw["w_gu"], "nt", F32, "ffn_up_dx", M=S, N=D, K=2 * F, tk=_tile(F, 2816),
            a_spec=lambda tm, tn, tk: pl.BlockSpec(
                (None, tm, tk), lambda i_, j_, k_: (k_ // (F // tk), i_, k_ % (F // tk))))
        d_w_gu = _matmul(
            sv["h2b"], dgu, "tn", F32, "ffn_up_dw" if pending is None else "ffn_up_dw_exchange",
            M=D, N=2 * F, K=S, tn=_tile(F, 1408), out_shards=N_CHIPS,
            b_spec=lambda tm, tn, tk: pl.BlockSpec(
                (None, tk, tn), lambda i_, j_, k_: (j_ // (F // tn), k_, j_ % (F // tn))),
            side=None if pending is None else _exchange_side(pending[3]))
        if pending is not None:
            d_w_gu, r2 = d_w_gu
            finish_reduce(pending[0], pending[1], pending[2], r2)
        ffn_big = [d_w_gu, d_w_down.reshape(N_CHIPS, F // N_CHIPS, D)]
        dx1, dx1b, g_ffn_rows[i] = _rmsnorm_bwd(sv["x1"], ffn_norm_g[i][None], dh2, dcur, "rmsnorm_bwd")
        g_mix = mix_norm_g[i][None]
        if i % 2 == 0:
            do = _matmul(dx1b, lw["w_out"], "nt", F32, "attn_out_dx", M=S, N=D, K=D)
            d_w_out = _matmul(sv["o"], dx1b, "tn", F32, "attn_out_dw", M=D, N=D, K=S)
            aux_do, dob = _attn_delta(do, sv["o"], "attn_delta")
            ffn_hs, ffn_hb = half_sums(ffn_big, "ffn")
            dqx, dkx, dvb, r2 = _flash_bwd(sv["q2"], sv["kn"], sv["vb"], dob, sv["aux_q"], sv["aux_k"], aux_do,
                                           "fox_attention_bwd", side=_exchange_side(ffn_hb))
            finish_reduce(i, "ffn", ffn_hs, r2)
            dproj, dgq, dgk, dc = _qkv_post_bwd(sv["proj"], dqx, dkx, dvb, fox_q_norm_g[j][None],
                                                fox_k_norm_g[j][None], "qk_norm_bwd")
            dfl, dbf = _fgate_bwd(dc, sv["fl"], sv["bf"], "forget_cumsum_bwd")
            d_w_qkv = _matmul(sv["h1b"], dproj, "tn", F32, "qkv_dw", M=D, N=3 * D, K=S)
            d_w_f = _matmul(sv["h1b"], dfl, "tn", F32, "forget_dw", M=D, N=LANES, K=S)
            dh1f = _matmul(dfl, lw["w_f"], "nt", F32, "forget_dx", M=S, N=D, K=LANES)
            dh1 = _matmul(dproj, lw["w_qkv"], "nt", F32, "qkv_dx", M=S, N=D, K=3 * D, res=dh1f)
            d_w_in = jnp.concatenate([d_w_qkv, d_w_f[:, :H]], axis=1)
            big, group = [_shards_from_cols(d_w_in), d_w_out.reshape(N_CHIPS, D // N_CHIPS, D)], "mix"
            g_bf[j], g_gq[j], g_gk[j] = dbf[0, :H], dgq[0], dgk[0]
        else:
            dyc, d_wp, dpb, dps = _pool_bwd_mix(dx1, sv["yb"], lw["w_pool"], pool_b_full[j][None],
                                                pool_scale_full[j][None], "pool_bwd_mix")
            dh1 = _pool_bwd_window(dyc, "pool_bwd_window")
            big, group = [d_wp.reshape(NG, N_CHIPS, G // N_CHIPS, G).transpose(1, 0, 2, 3)] + ffn_big, "all"
            g_pb[j], g_ps[j] = dpb[0], dps[0]
        dcur, dcur_b, g_mix_rows[i] = _rmsnorm_bwd(sv["x"], g_mix, dh1, dx1, "rmsnorm_bwd")
        pending = (i, group) + half_sums(big, group)
    finish_reduce(pending[0], pending[1], pending[2],
                  _run_side(_exchange_side(pending[3]), "grads_to_chips_last"))

    small_shapes = [(depth, D), (depth, D), fox_b_f.shape, fox_q_norm_g.shape, fox_k_norm_g.shape,
                    (n_pool, D), (n_pool, D)]
    small = [jnp.stack(g_mix_rows)[:, 0], jnp.stack(g_ffn_rows)[:, 0], jnp.stack(g_bf), jnp.stack(g_gq),
             jnp.stack(g_gk), jnp.stack(g_pb), jnp.stack(g_ps)]
    (gr_mix, gr_ffn, gr_bf, gr_gq, gr_gk, gr_pb_full, gr_ps_full) = _unpack(
        _allreduce_small(_pack(small, _packed_rows(small_shapes)), "allreduce_small_grads"), small_shapes)
    gr_pb = lax.dynamic_slice(gr_pb_full, (0, chip * Dq), (n_pool, Dq))
    gr_ps = lax.dynamic_slice(gr_ps_full, (0, chip * Dq), (n_pool, Dq))

    fox_layers = [i for i in range(depth) if i % 2 == 0]
    pool_layers = [i for i in range(depth) if i % 2 == 1]
    gr_w_in = jnp.stack([red_mix[i][0].reshape(fox_w_in.shape[1:]) for i in fox_layers])
    gr_w_out = jnp.stack([red_mix[i][1].reshape(fox_w_out.shape[1:]) for i in fox_layers])
    gr_pool_w = jnp.stack([red_mix[i][0].reshape(pool_w.shape[1:]) for i in pool_layers])
    gr_gu = jnp.stack([red_ffn[i][0].reshape(ffn_w_gate_up.shape[1:]) for i in range(depth)])
    gr_down = jnp.stack([red_ffn[i][1].reshape(ffn_w_down.shape[1:]) for i in range(depth)])

    def update_big(w, g, m, v, name):
        flat = lambda a: a.reshape(-1, a.shape[-1])
        return [o.reshape(w.shape) for o in _adamw(flat(w), flat(g), flat(m), flat(v), name)]

    names = ["mix_norm_g", "ffn_norm_g", "fox_w_in", "fox_b_f", "fox_q_norm_g", "fox_k_norm_g", "fox_w_out",
             "pool_w", "pool_b", "pool_scale", "ffn_w_gate_up", "ffn_w_down"]
    ws = dict(zip(names, [mix_norm_g, ffn_norm_g, fox_w_in, fox_b_f, fox_q_norm_g, fox_k_norm_g, fox_w_out,
                          pool_w, pool_b, pool_scale, ffn_w_gate_up, ffn_w_down]))
    ms = dict(zip(names, [m_mix_norm_g, m_ffn_norm_g, m_fox_w_in, m_fox_b_f, m_fox_q_norm_g, m_fox_k_norm_g,
                          m_fox_w_out, m_pool_w, m_pool_b, m_pool_scale, m_ffn_w_gate_up, m_ffn_w_down]))
    vs = dict(zip(names, [v_mix_norm_g, v_ffn_norm_g, v_fox_w_in, v_fox_b_f, v_fox_q_norm_g, v_fox_k_norm_g,
                          v_fox_w_out, v_pool_w, v_pool_b, v_pool_scale, v_ffn_w_gate_up, v_ffn_w_down]))
    grads = dict(mix_norm_g=gr_mix, ffn_norm_g=gr_ffn, fox_w_in=gr_w_in, fox_b_f=gr_bf, fox_q_norm_g=gr_gq,
                 fox_k_norm_g=gr_gk, fox_w_out=gr_w_out, pool_w=gr_pool_w, pool_b=gr_pb, pool_scale=gr_ps,
                 ffn_w_gate_up=gr_gu, ffn_w_down=gr_down)
    big_names = ["fox_w_in", "fox_w_out", "pool_w", "ffn_w_gate_up", "ffn_w_down"]
    small_names = [n for n in names if n not in big_names]
    delta, new_m, new_v = {}, {}, {}
    for n in big_names:
        delta[n], new_m[n], new_v[n] = update_big(ws[n], grads[n], ms[n], vs[n], "adamw_" + n)
    shapes = [ws[n].shape for n in small_names]
    rows = _packed_rows(shapes)
    packed = _adamw(_pack([ws[n] for n in small_names], rows), _pack([grads[n] for n in small_names], rows),
                    _pack([ms[n] for n in small_names], rows), _pack([vs[n] for n in small_names], rows),
                    "adamw_small")
    for dst, pk in zip((delta, new_m, new_v), packed):
        for n, a in zip(small_names, _unpack(pk, shapes)):
            dst[n] = a

    return (loss, dcur[None], *[grads[n] for n in names], *[delta[n] for n in names],
            *[new_m[n] for n in names], *[new_v[n] for n in names])
```
